```python
import math
import jax, jax.numpy as jnp
from jax import lax
import numpy as np

D_MODEL = 1024
BATCH = 2
SEQ = 8192
DEPTH = 4
DEC_BATCH = 128
DEC_SEQ = 8
PAST_LEN = 2048
PAGE_SIZE = 128

N_MIXERS = 3
N_CONV_LAYERS = (DEPTH + 2) // N_MIXERS
N_GLA_LAYERS = (DEPTH + 1) // N_MIXERS
N_ATT_LAYERS = DEPTH // N_MIXERS
CONV_CHANNELS = D_MODEL
CONV_SIZE = 31
GLA_HEADS = 4
GLA_DK = D_MODEL // 2 // GLA_HEADS
GLA_DV = D_MODEL // GLA_HEADS
GLA_RANK = 16
GLA_TAU = 16.0
GLA_CHUNK = 64
ATT_GROUPS = ((128, 1), (512, 4), (2048, 16))
ATT_HEADS = 8
ATT_HEAD_DIM = 64
ATT_WIDTH = ATT_HEADS * ATT_HEAD_DIM
Q_BLOCK = 128
ROPE_THETA = 10000.0
EPS = 1e-6

kernel_name = 'hybrid_conv_gla_dilated_swa_step'


def _rms_norm(x, g):
    x32 = x.astype(jnp.float32)
    y = x32 * lax.rsqrt(jnp.mean(x32 * x32, axis=-1, keepdims=True) + EPS)
    return (y * g.astype(jnp.float32)).astype(x.dtype)


def _layer_norm(x, g, b):
    x32 = x.astype(jnp.float32)
    xc = x32 - jnp.mean(x32, axis=-1, keepdims=True)
    y = xc * lax.rsqrt(jnp.mean(xc * xc, axis=-1, keepdims=True) + EPS)
    return (y * g.astype(jnp.float32) + b.astype(jnp.float32)).astype(x.dtype)


def _conv_mixer(h, buf, w_in, w_dw, b_dw, g_ln, b_ln, w_out):
    a, a_gate, z = jnp.split(h @ w_in, 3, axis=-1)
    u = a * jax.nn.sigmoid(a_gate)
    u_ext = jnp.concatenate([buf.astype(u.dtype), u], axis=1)
    y = lax.conv_general_dilated(u_ext, w_dw[:, None, :].astype(u.dtype), window_strides=(1,), padding='VALID',
                                 dimension_numbers=('NWC', 'WIO', 'NWC'),
                                 feature_group_count=CONV_CHANNELS) + b_dw
    y = jax.nn.silu(_layer_norm(y, g_ln, b_ln)) * jax.nn.silu(z)
    return y @ w_out, u_ext[:, -(CONV_SIZE - 1):]


def _gla_chunked(q, k, v, log_a, s0):
    B, L, H, K = q.shape
    V = v.shape[-1]
    C = math.gcd(L, GLA_CHUNK)
    n = L // C

    def chunks(t):
        return t.astype(jnp.float32).reshape(B, n, C, H, t.shape[-1]).swapaxes(0, 1)

    causal = jnp.tril(jnp.ones((C, C), dtype=bool))

    def step(S, xs):
        qc, kc, vc, ac = xs
        bc = jnp.cumsum(ac, axis=1)
        diff = jnp.where(causal[None, :, :, None, None], bc[:, :, None] - bc[:, None, :], -jnp.inf)
        attn = jnp.einsum('bihk,bjhk,bijhk->bhij', qc, kc, jnp.exp(diff))
        o = (jnp.einsum('bhij,bjhv->bihv', attn, vc)
             + jnp.einsum('bihk,bhkv->bihv', qc * jnp.exp(bc), S))
        btot = bc[:, -1]
        S = (jnp.exp(btot)[..., None] * S
             + jnp.einsum('bjhk,bjhv->bhkv', kc * jnp.exp(btot[:, None] - bc), vc))
        return S, o

    S, o = lax.scan(step, s0.astype(jnp.float32), (chunks(q), chunks(k), chunks(v), chunks(log_a)))
    return o.swapaxes(0, 1).reshape(B, L, H, V), S


def _gla_mixer(h, s0, w_in, w_a1, w_a2, b_a, g_norm, w_out):
    B, L, _ = h.shape
    qk = GLA_HEADS * GLA_DK
    vw = GLA_HEADS * GLA_DV
    q, k, v, r = jnp.split(h @ w_in, [qk, 2 * qk, 2 * qk + vw], axis=-1)
    q = q.reshape(B, L, GLA_HEADS, GLA_DK) * (GLA_DK ** -0.5)
    k = k.reshape(B, L, GLA_HEADS, GLA_DK)
    v = v.reshape(B, L, GLA_HEADS, GLA_DV)
    log_a = jax.nn.log_sigmoid(((h @ w_a1) @ w_a2 + b_a).astype(jnp.float32)) / GLA_TAU
    log_a = log_a.reshape(B, L, GLA_HEADS, GLA_DK)
    o, s = _gla_chunked(q, k, v, log_a, s0)
    o = _rms_norm(o.astype(h.dtype), g_norm).reshape(B, L, vw) * jax.nn.silu(r)
    return o @ w_out, s


def _rope(x, pos):
    half = ATT_HEAD_DIM // 2
    inv = ROPE_THETA ** (-jnp.arange(half, dtype=jnp.float32) / half)
    ang = pos.astype(jnp.float32)[:, None] * inv[None, :]
    cos = jnp.cos(ang)[None, :, None, None, :]
    sin = jnp.sin(ang)[None, :, None, None, :]
    x32 = x.astype(jnp.float32)
    x1, x2 = x32[..., :half], x32[..., half:]
    return jnp.concatenate([x1 * cos - x2 * sin, x2 * cos + x1 * sin], axis=-1).astype(x.dtype)


def _dilated_attention(q, ks_ext, vs_ext, first_valid):
    B, L, G, H, Dh = q.shape
    blk = math.gcd(L, Q_BLOCK)
    nblk = L // blk
    qb = q.reshape(B, nblk, blk, G, H, Dh).swapaxes(0, 1)
    scale = Dh ** -0.5

    def block(args):
        bi, qblk = args
        start = bi * blk
        outs, lses = [], []
        for g, (W, d) in enumerate(ATT_GROUPS):
            n_keys = W // d + 1
            idx = W + jnp.arange(blk)[:, None] - d * jnp.arange(n_keys)[None, :]
            valid = (start + idx) >= first_valid[g]
            k_slab = lax.dynamic_slice_in_dim(ks_ext[g], start, W + blk, axis=1)
            v_slab = lax.dynamic_slice_in_dim(vs_ext[g], start, W + blk, axis=1)
            k_gat = jnp.take(k_slab, idx, axis=1)
            v_gat = jnp.take(v_slab, idx, axis=1)
            s = jnp.einsum('bqhd,bqkhd->bqhk', qblk[:, :, g], k_gat,
                           preferred_element_type=jnp.float32) * scale
            s = jnp.where(valid[None, :, None, :], s, -1e30)
            m = jnp.max(s, axis=-1, keepdims=True)
            p = jnp.exp(s - m)
            den = jnp.sum(p, axis=-1)
            o = jnp.einsum('bqhk,bqkhd->bqhd', p.astype(v_gat.dtype), v_gat,
                           preferred_element_type=jnp.float32) / den[..., None]
            outs.append(o)
            lses.append(m[..., 0] + jnp.log(den))
        w = jax.nn.softmax(jnp.stack(lses), axis=0)
        return jnp.sum(w[..., None] * jnp.stack(outs), axis=0).astype(q.dtype)

    out = lax.map(block, (jnp.arange(nblk), qb))
    return out.swapaxes(0, 1).reshape(B, L, H, Dh)


def _dilated_mixer(h, pos, bufs_k, bufs_v, w_in, w_out):
    B, L, _ = h.shape
    G = len(ATT_GROUPS)
    gw = G * ATT_WIDTH
    q, k, v, z = jnp.split(h @ w_in, [gw, 2 * gw, 3 * gw], axis=-1)
    shp = (B, L, G, ATT_HEADS, ATT_HEAD_DIM)
    q = _rope(q.reshape(shp), pos)
    k = _rope(k.reshape(shp), pos)
    v = v.reshape(shp)
    ks_ext, vs_ext, first_valid, new_k, new_v = [], [], [], [], []
    for g, (W, _) in enumerate(ATT_GROUPS):
        kg, vg = k[:, :, g], v[:, :, g]
        if bufs_k is None:
            prev_k = jnp.zeros((B, W, ATT_HEADS, ATT_HEAD_DIM), kg.dtype)
            prev_v = prev_k
            fv = W
            keep = min(W, L)
            new_k.append(kg[:, L - keep:])
            new_v.append(vg[:, L - keep:])
        else:
            nb = bufs_k[g].shape[1]
            pad = jnp.zeros((B, W - nb, ATT_HEADS, ATT_HEAD_DIM), kg.dtype)
            prev_k = jnp.concatenate([pad, bufs_k[g].astype(kg.dtype)], axis=1)
            prev_v = jnp.concatenate([pad, bufs_v[g].astype(vg.dtype)], axis=1)
            fv = W - nb
            new_k.append(kg)
            new_v.append(vg)
        ks_ext.append(jnp.concatenate([prev_k, kg], axis=1))
        vs_ext.append(jnp.concatenate([prev_v, vg], axis=1))
        first_valid.append(fv)
    o = _dilated_attention(q, ks_ext, vs_ext, first_valid).reshape(B, L, ATT_WIDTH) * jax.nn.silu(z)
    return o @ w_out, new_k, new_v


def setup_inputs(seed: int = 0) -> dict:
    key = jax.random.key(seed)
    ks = iter(jax.random.split(key, 40))
    f32 = jnp.float32

    def nrm(shape, scale=1.0):
        return jax.random.normal(next(ks), shape, f32) * scale

    D = D_MODEL
    C = CONV_CHANNELS
    gw = len(ATT_GROUPS) * ATT_WIDTH
    gla_cols = 2 * GLA_HEADS * GLA_DK + 2 * GLA_HEADS * GLA_DV
    inp = {}
    inp['x_prompt'] = nrm((BATCH, SEQ, D))
    inp['x_sample'] = nrm((DEC_BATCH, DEC_SEQ, D))
    inp['state_conv'] = nrm((N_CONV_LAYERS, DEC_BATCH, CONV_SIZE - 1, C), 0.5)
    inp['state_gla'] = nrm((N_GLA_LAYERS, DEC_BATCH, GLA_HEADS, GLA_DK, GLA_DV), 0.5)
    for i, (W, _) in enumerate(ATT_GROUPS):
        nb = min(W, PAST_LEN)
        inp['cache_k_g%d' % i] = nrm((N_ATT_LAYERS, DEC_BATCH, nb, ATT_HEADS, ATT_HEAD_DIM))
        inp['cache_v_g%d' % i] = nrm((N_ATT_LAYERS, DEC_BATCH, nb, ATT_HEADS, ATT_HEAD_DIM))
    inp['c_prompt'] = nrm((BATCH, D))
    inp['c_sample'] = nrm((DEC_BATCH, D))
    inp['w_ada'] = nrm((DEPTH, D, 3 * D), 0.5 * D ** -0.5)
    inp['b_ada'] = nrm((DEPTH, 3 * D), 0.02)
    inp['g_pre'] = 1.0 + nrm((DEPTH, D), 0.05)
    inp['g_post'] = 1.0 + nrm((DEPTH, D), 0.05)
    inp['w_conv_in'] = nrm((N_CONV_LAYERS, D, 3 * C), D ** -0.5)
    inp['w_dw'] = nrm((N_CONV_LAYERS, CONV_SIZE, C), CONV_SIZE ** -0.5)
    inp['b_dw'] = nrm((N_CONV_LAYERS, C), 0.02)
    inp['g_conv_ln'] = 1.0 + nrm((N_CONV_LAYERS, C), 0.05)
    inp['b_conv_ln'] = nrm((N_CONV_LAYERS, C), 0.02)
    inp['w_conv_out'] = nrm((N_CONV_LAYERS, C, D), C ** -0.5)
    inp['w_gla_in'] = nrm((N_GLA_LAYERS, D, gla_cols), D ** -0.5)
    inp['w_gla_a1'] = nrm((N_GLA_LAYERS, D, GLA_RANK), D ** -0.5)
    inp['w_gla_a2'] = nrm((N_GLA_LAYERS, GLA_RANK, GLA_HEADS * GLA_DK), GLA_RANK ** -0.5)
    inp['b_gla_a'] = nrm((N_GLA_LAYERS, GLA_HEADS * GLA_DK), 0.02)
    inp['g_gla_norm'] = 1.0 + nrm((N_GLA_LAYERS, GLA_DV), 0.05)
    inp['w_gla_out'] = nrm((N_GLA_LAYERS, GLA_HEADS * GLA_DV, D), (GLA_HEADS * GLA_DV) ** -0.5)
    inp['w_att_in'] = nrm((N_ATT_LAYERS, D, 3 * gw + ATT_WIDTH), D ** -0.5)
    inp['w_att_out'] = nrm((N_ATT_LAYERS, ATT_WIDTH, D), ATT_WIDTH ** -0.5)
    return inp


def reference(x_prompt, x_sample, state_conv, state_gla, cache_k_g0, cache_v_g0, cache_k_g1, cache_v_g1,
              cache_k_g2, cache_v_g2, c_prompt, c_sample, w_ada, b_ada, g_pre, g_post,
              w_conv_in, w_dw, b_dw, g_conv_ln, b_conv_ln, w_conv_out,
              w_gla_in, w_gla_a1, w_gla_a2, b_gla_a, g_gla_norm, w_gla_out,
              w_att_in, w_att_out):
    xs = [x_prompt, x_sample]
    cs = [c_prompt, c_sample]
    pos = [jnp.arange(x_prompt.shape[1]), PAST_LEN + jnp.arange(x_sample.shape[1])]
    caches_k = (cache_k_g0, cache_k_g1, cache_k_g2)
    caches_v = (cache_v_g0, cache_v_g1, cache_v_g2)
    n_groups = len(ATT_GROUPS)
    conv_new = ([], [])
    gla_new = ([], [])
    k_new = ([[] for _ in range(n_groups)], [[] for _ in range(n_groups)])
    v_new = ([[] for _ in range(n_groups)], [[] for _ in range(n_groups)])

    for l in range(DEPTH):
        kind, j = l % N_MIXERS, l // N_MIXERS
        for grp in range(2):
            x = xs[grp]
            B = x.shape[0]
            mod = cs[grp] @ w_ada[l] + b_ada[l]
            shift, scale, gate = [t[:, None, :] for t in jnp.split(mod, 3, axis=-1)]
            h = _rms_norm(x, g_pre[l]) * (1 + scale) + shift
            if kind == 0:
                buf = (jnp.zeros((B, CONV_SIZE - 1, CONV_CHANNELS), x.dtype) if grp == 0 else state_conv[j])
                o, st = _conv_mixer(h, buf, w_conv_in[j], w_dw[j], b_dw[j], g_conv_ln[j], b_conv_ln[j], w_conv_out[j])
                conv_new[grp].append(st.astype(state_conv.dtype))
            elif kind == 1:
                s0 = (jnp.zeros((B, GLA_HEADS, GLA_DK, GLA_DV), jnp.float32) if grp == 0 else state_gla[j])
                o, st = _gla_mixer(h, s0, w_gla_in[j], w_gla_a1[j], w_gla_a2[j], b_gla_a[j], g_gla_norm[j], w_gla_out[j])
                gla_new[grp].append(st.astype(state_gla.dtype))
            else:
                bk = None if grp == 0 else [c[j] for c in caches_k]
                bv = None if grp == 0 else [c[j] for c in caches_v]
                o, nk, nv = _dilated_mixer(h, pos[grp], bk, bv, w_att_in[j], w_att_out[j])
                for g in range(n_groups):
                    k_new[grp][g].append(nk[g])
                    v_new[grp][g].append(nv[g])
            xs[grp] = x + gate * _rms_norm(o, g_post[l])

    y_prompt, y_sample = xs
    new_state_conv_prompt = jnp.stack(conv_new[0])
    new_state_conv_sample = jnp.stack(conv_new[1])
    new_state_gla_prompt = jnp.stack(gla_new[0])
    new_state_gla_sample = jnp.stack(gla_new[1])
    new_k_g0_prompt = jnp.stack(k_new[0][0])
    new_v_g0_prompt = jnp.stack(v_new[0][0])
    new_k_g1_prompt = jnp.stack(k_new[0][1])
    new_v_g1_prompt = jnp.stack(v_new[0][1])
    new_k_g2_prompt = jnp.stack(k_new[0][2])
    new_v_g2_prompt = jnp.stack(v_new[0][2])
    new_k_g0_sample = jnp.stack(k_new[1][0])
    new_v_g0_sample = jnp.stack(v_new[1][0])
    new_k_g1_sample = jnp.stack(k_new[1][1])
    new_v_g1_sample = jnp.stack(v_new[1][1])
    new_k_g2_sample = jnp.stack(k_new[1][2])
    new_v_g2_sample = jnp.stack(v_new[1][2])
    return (y_prompt, y_sample, new_state_conv_prompt, new_state_conv_sample,
            new_state_gla_prompt, new_state_gla_sample,
            new_k_g0_prompt, new_v_g0_prompt, new_k_g1_prompt, new_v_g1_prompt, new_k_g2_prompt, new_v_g2_prompt,
            new_k_g0_sample, new_v_g0_sample, new_k_g1_sample, new_v_g1_sample, new_k_g2_sample, new_v_g2_sample)
```

```python
import functools
import math

import jax
import jax.numpy as jnp
from jax import lax
from jax.experimental import pallas as pl
from jax.experimental.pallas import tpu as pltpu

F32 = jnp.float32
BF16 = jnp.bfloat16
HIGHEST = lax.Precision.HIGHEST

PAST_LEN = 2048
N_MIXERS = 3
CONV_SIZE = 31
GLA_HEADS = 4
GLA_RANK = 16
GLA_TAU = 16.0
ATT_GROUPS = ((128, 1), (512, 4), (2048, 16))
ATT_HEADS = 8
ATT_HEAD_DIM = 64
ATT_WIDTH = ATT_HEADS * ATT_HEAD_DIM
ROPE_THETA = 10000.0
EPS = 1e-6
NEG_INF = -1e30

LANES = 128
SUBLANES = 8
VMEM_LIMIT_BYTES = 56 * 1024 * 1024

PROJ_ROWS = 512
CONV_ROWS = 128
CONV_CHUNK = 32
CONV_HALO = 32
GLA_ROWS = 256
GLA_CHUNK = 64
ATT_BLOCK = 128


def _params(*sem):
    return pltpu.CompilerParams(dimension_semantics=sem, vmem_limit_bytes=VMEM_LIMIT_BYTES)


def _token_tiling(B, L, rows=PROJ_ROWS):
    if L >= rows:
        return 1, rows
    nb = max(1, rows // L)
    while B % nb:
        nb //= 2
    return nb, L


def _mod_kernel(c_ref, w_ref, b_ref, o_ref):
    o_ref[0] = jnp.dot(c_ref[...], w_ref[0], precision=HIGHEST, preferred_element_type=F32) + b_ref[0]


def _modulation(c_all, w_ada, b_ada):
    depth, D, N = w_ada.shape
    R = c_all.shape[0]
    tn = 1024
    return pl.pallas_call(
        _mod_kernel,
        grid=(depth, N // tn),
        in_specs=[pl.BlockSpec((R, D), lambda l, j: (0, 0)),
                  pl.BlockSpec((1, D, tn), lambda l, j: (l, 0, j)),
                  pl.BlockSpec((1, 1, tn), lambda l, j: (l, 0, j))],
        out_specs=pl.BlockSpec((1, R, tn), lambda l, j: (l, 0, j)),
        out_shape=jax.ShapeDtypeStruct((depth, R, N), F32),
        compiler_params=_params("arbitrary", "arbitrary"),
        name="adaln_mod",
    )(c_all, w_ada, b_ada.reshape(depth, 1, N))


def _modulated_norm(x_ref, mod_ref, g_ref):
    x = x_ref[...]
    nb, tl, D = x.shape
    ms = jnp.mean(x * x, axis=-1, keepdims=True)
    y = x * lax.rsqrt(ms + EPS) * g_ref[...]
    shift = mod_ref[:, :, 0:D]
    scale = mod_ref[:, :, D:2 * D]
    h = y * (1.0 + scale) + shift
    return h.reshape(nb * tl, D).astype(BF16)


def _pre_conv_kernel(x_ref, mod_ref, g_ref, w_ref, u_ref, sz_ref):
    h = _modulated_norm(x_ref, mod_ref, g_ref)
    res = jnp.dot(h, w_ref[...], preferred_element_type=F32)
    C = u_ref.shape[-1]
    u = res[:, :C] * jax.nn.sigmoid(res[:, C:2 * C])
    z = res[:, 2 * C:]
    u_ref[...] = u.reshape(u_ref.shape)
    sz_ref[...] = (z * jax.nn.sigmoid(z)).reshape(sz_ref.shape)


def _pre_gla_kernel(x_ref, mod_ref, g_ref, w_ref, wa1_ref, wa2_ref, ba_ref,
                    q_ref, k_ref, v_ref, sr_ref, la_ref):
    h = _modulated_norm(x_ref, mod_ref, g_ref)
    res = jnp.dot(h, w_ref[...], preferred_element_type=F32)
    qk = q_ref.shape[-1]
    vw = v_ref.shape[-1]
    dk = qk // GLA_HEADS
    q_ref[...] = (res[:, :qk] * (dk ** -0.5)).reshape(q_ref.shape)
    k_ref[...] = res[:, qk:2 * qk].reshape(k_ref.shape)
    v_ref[...] = res[:, 2 * qk:2 * qk + vw].reshape(v_ref.shape)
    r = res[:, 2 * qk + vw:]
    sr_ref[...] = (r * jax.nn.sigmoid(r)).reshape(sr_ref.shape)
    low = jnp.dot(h, wa1_ref[...], preferred_element_type=F32)
    zg = jnp.dot(low.astype(BF16), wa2_ref[...], preferred_element_type=F32) + ba_ref[...]
    log_sig = jnp.minimum(zg, 0.0) - jnp.log(1.0 + jnp.exp(-jnp.abs(zg)))
    la_ref[...] = (log_sig * (1.0 / GLA_TAU)).reshape(la_ref.shape)


def _swap_halves(x):
    half = ATT_HEAD_DIM // 2
    lane = lax.broadcasted_iota(jnp.int32, x.shape, 1)
    lower = (lane % ATT_HEAD_DIM) < half
    return jnp.where(lower, pltpu.roll(x, LANES - half, 1), pltpu.roll(x, half, 1))


def _pre_att_kernel(x_ref, mod_ref, g_ref, w_ref, cos_ref, sin_ref, q_ref, k_ref, v_ref, sz_ref):
    h = _modulated_norm(x_ref, mod_ref, g_ref)
    res = jnp.dot(h, w_ref[...], preferred_element_type=F32)
    nb, tl, gw = q_ref.shape
    rows = nb * tl
    cos = cos_ref[...]
    sin = sin_ref[...]

    def rope(col0, out_ref, mult):
        for c in range(gw // LANES):
            xc = res[:, col0 + c * LANES: col0 + (c + 1) * LANES].reshape(nb, tl, LANES)
            sw = _swap_halves(xc.reshape(rows, LANES)).reshape(nb, tl, LANES)
            val = xc * cos + sw * sin
            if mult != 1.0:
                val = val * mult
            out_ref[:, :, c * LANES:(c + 1) * LANES] = val

    rope(0, q_ref, ATT_HEAD_DIM ** -0.5)
    rope(gw, k_ref, 1.0)
    v_ref[...] = res[:, 2 * gw:3 * gw].reshape(v_ref.shape)
    z = res[:, 3 * gw:]
    sz_ref[...] = (z * jax.nn.sigmoid(z)).reshape(sz_ref.shape)


def _pre_call(kernel, name, x, mod, g_pre, weights, extra_specs, extras, out_widths):
    B, L, D = x.shape
    nb, tl = _token_tiling(B, L, PROJ_ROWS if sum(out_widths) <= 3 * D else PROJ_ROWS // 2)
    tok = lambda b, i: (b, i, 0)
    in_specs = [pl.BlockSpec((nb, tl, D), tok),
                pl.BlockSpec((nb, 1, mod.shape[-1]), lambda b, i: (b, 0, 0)),
                pl.BlockSpec((1, D), lambda b, i: (0, 0))]
    in_specs += [pl.BlockSpec(w.shape, lambda b, i: (0, 0)) for w in weights]
    in_specs += extra_specs(nb, tl)
    return pl.pallas_call(
        kernel,
        grid=(B // nb, L // tl),
        in_specs=in_specs,
        out_specs=[pl.BlockSpec((nb, tl, n), tok) for n in out_widths],
        out_shape=[jax.ShapeDtypeStruct((B, L, n), F32) for n in out_widths],
        compiler_params=_params("arbitrary", "arbitrary"),
        name=name,
    )(x, mod, g_pre.reshape(1, D), *weights, *extras)


def _gated_residual(y_bf16, w_ref, g_ref, mod_ref, x_ref, o_ref):
    nb, tl, D = x_ref.shape
    o = jnp.dot(y_bf16, w_ref[...], preferred_element_type=F32)
    ms = jnp.mean(o * o, axis=-1, keepdims=True)
    o = (o * lax.rsqrt(ms + EPS) * g_ref[...]).reshape(nb, tl, D)
    gate = mod_ref[:, :, 2 * D:3 * D]
    o_ref[...] = x_ref[...] + gate * o


def _post_kernel(y_ref, w_ref, g_ref, mod_ref, x_ref, o_ref):
    nb, tl, K = y_ref.shape
    _gated_residual(y_ref[...].reshape(nb * tl, K).astype(BF16), w_ref, g_ref, mod_ref, x_ref, o_ref)


def _post_att_kernel(o0_ref, o1_ref, o2_ref, l0_ref, l1_ref, l2_ref, sz_ref, w_ref, g_ref, mod_ref, x_ref, o_ref):
    nb, tl, K = sz_ref.shape
    l0, l1, l2 = l0_ref[...], l1_ref[...], l2_ref[...]
    m = jnp.maximum(jnp.maximum(l0, l1), l2)
    e0, e1, e2 = jnp.exp(l0 - m), jnp.exp(l1 - m), jnp.exp(l2 - m)
    att = (e0 * o0_ref[...] + e1 * o1_ref[...] + e2 * o2_ref[...]) / (e0 + e1 + e2)
    y = att * sz_ref[...]
    _gated_residual(y.reshape(nb * tl, K).astype(BF16), w_ref, g_ref, mod_ref, x_ref, o_ref)


def _post_call(kernel, name, ys, w_out, g_post, mod, x):
    B, L, D = x.shape
    K = w_out.shape[0]
    nb, tl = _token_tiling(B, L)
    tok = lambda b, i: (b, i, 0)
    in_specs = [pl.BlockSpec((nb, tl, K), tok) for _ in ys]
    in_specs += [pl.BlockSpec((K, D), lambda b, i: (0, 0)),
                 pl.BlockSpec((1, D), lambda b, i: (0, 0)),
                 pl.BlockSpec((nb, 1, mod.shape[-1]), lambda b, i: (b, 0, 0)),
                 pl.BlockSpec((nb, tl, D), tok)]
    return pl.pallas_call(
        kernel,
        grid=(B // nb, L // tl),
        in_specs=in_specs,
        out_specs=pl.BlockSpec((nb, tl, D), tok),
        out_shape=jax.ShapeDtypeStruct((B, L, D), F32),
        compiler_params=_params("arbitrary", "arbitrary"),
        name=name,
    )(*ys, w_out, g_post.reshape(1, D), mod, x)


def _conv_kernel(u_ref, sz_ref, buf_ref, w_ref, b_ref, g_ref, bl_ref, y_ref, st_ref, ext_ref):
    nb, tl, C = u_ref.shape
    keep = CONV_SIZE - 1
    lead = CONV_HALO - keep

    @pl.when(pl.program_id(1) == 0)
    def _():
        ext_ref[:, lead:CONV_HALO, :] = buf_ref[...]

    ext_ref[:, CONV_HALO:CONV_HALO + tl, :] = u_ref[...]
    chunk = min(tl, CONV_CHUNK)
    bias = b_ref[...]
    gamma = g_ref[...]
    beta = bl_ref[...]
    for n in range(nb):
        for r0 in range(0, tl, chunk):
            acc = jnp.broadcast_to(bias, (chunk, C))
            for j in range(CONV_SIZE):
                acc = acc + w_ref[j:j + 1, :] * ext_ref[n, lead + r0 + j: lead + r0 + j + chunk, :]
            mu = jnp.mean(acc, axis=-1, keepdims=True)
            xc = acc - mu
            var = jnp.mean(xc * xc, axis=-1, keepdims=True)
            ln = xc * lax.rsqrt(var + EPS) * gamma + beta
            y_ref[n, r0:r0 + chunk, :] = ln * jax.nn.sigmoid(ln) * sz_ref[n, r0:r0 + chunk, :]
    tail = ext_ref[:, lead + tl: CONV_HALO + tl, :]
    st_ref[...] = tail
    ext_ref[:, lead:CONV_HALO, :] = tail


def _conv_mixer(u, sz, buf, w_dw, b_dw, g_ln, b_ln):
    B, L, C = u.shape
    keep = CONV_SIZE - 1
    if L >= CONV_ROWS:
        nb, tl = 1, CONV_ROWS
    else:
        nb, tl = SUBLANES, L
    tok = lambda b, i: (b, i, 0)
    vec = lambda b, i: (0, 0)
    return pl.pallas_call(
        _conv_kernel,
        grid=(B // nb, L // tl),
        in_specs=[pl.BlockSpec((nb, tl, C), tok),
                  pl.BlockSpec((nb, tl, C), tok),
                  pl.BlockSpec((nb, keep, C), lambda b, i: (b, 0, 0)),
                  pl.BlockSpec((CONV_SIZE, C), vec),
                  pl.BlockSpec((1, C), vec), pl.BlockSpec((1, C), vec), pl.BlockSpec((1, C), vec)],
        out_specs=[pl.BlockSpec((nb, tl, C), tok),
                   pl.BlockSpec((nb, keep, C), lambda b, i: (b, 0, 0))],
        out_shape=[jax.ShapeDtypeStruct((B, L, C), F32),
                   jax.ShapeDtypeStruct((B, keep, C), F32)],
        scratch_shapes=[pltpu.VMEM((nb, CONV_HALO + tl, C), F32)],
        compiler_params=_params("arbitrary", "arbitrary"),
        name="conv_mixer",
    )(u, sz, buf, w_dw, b_dw.reshape(1, C), g_ln.reshape(1, C), b_ln.reshape(1, C))


def _gla_kernel(q_ref, k_ref, v_ref, sr_ref, la_ref, s0_ref, g_ref, y_ref, sout_ref, S_ref, *, chunk):
    _, tl, qk = q_ref.shape
    vw = v_ref.shape[-1]
    H = GLA_HEADS
    dk, dv = qk // H, vw // H

    @pl.when(pl.program_id(1) == 0)
    def _():
        S_ref[...] = s0_ref[0]

    row = lax.broadcasted_iota(jnp.int32, (chunk, chunk), 0)
    col = lax.broadcasted_iota(jnp.int32, (chunk, chunk), 1)
    causal = row >= col
    tril = causal.astype(F32)
    ones = jnp.ones((chunk, dv), F32)
    gamma = g_ref[...]
    tn = (((0,), (0,)), ((), ()))
    nt = (((1,), (1,)), ((), ()))

    for c0 in range(0, tl, chunk):
        rows = slice(c0, c0 + chunk)
        la = la_ref[0, rows, :]
        bc = jnp.dot(tril, la, precision=HIGHEST, preferred_element_type=F32)
        btot = bc[chunk - 1:chunk, :]
        qt = q_ref[0, rows, :] * jnp.exp(bc)
        kk = k_ref[0, rows, :]
        kt = kk * jnp.exp(-bc)
        kh = kk * jnp.exp(btot - bc)
        for h in range(H):
            ks = slice(h * dk, (h + 1) * dk)
            vs = slice(h * dv, (h + 1) * dv)
            qh = qt[:, ks].astype(BF16)
            vh = v_ref[0, rows, vs].astype(BF16)
            att = lax.dot_general(qh, kt[:, ks].astype(BF16), nt, preferred_element_type=F32)
            att = jnp.where(causal, att, 0.0)
            S = S_ref[h]
            o = (jnp.dot(att.astype(BF16), vh, preferred_element_type=F32)
                 + jnp.dot(qh, S.astype(BF16), preferred_element_type=F32))
            decay = jnp.exp(lax.dot_general(la[:, ks], ones, tn, precision=HIGHEST,
                                            preferred_element_type=F32))
            S_ref[h] = decay * S + lax.dot_general(kh[:, ks].astype(BF16), vh, tn, preferred_element_type=F32)
            ms = jnp.mean(o * o, axis=-1, keepdims=True)
            y_ref[0, rows, vs] = o * lax.rsqrt(ms + EPS) * gamma * sr_ref[0, rows, vs]

    sout_ref[0] = S_ref[...]


def _gla_mixer(q, k, v, sr, la, s0, g_norm):
    B, L, qk = q.shape
    vw = v.shape[-1]
    H = GLA_HEADS
    dk, dv = qk // H, vw // H
    tl = min(L, GLA_ROWS)
    chunk = min(L, GLA_CHUNK)
    tok = lambda b, i: (b, i, 0)
    st = lambda b, i: (b, 0, 0, 0)
    return pl.pallas_call(
        functools.partial(_gla_kernel, chunk=chunk),
        grid=(B, L // tl),
        in_specs=[pl.BlockSpec((1, tl, qk), tok), pl.BlockSpec((1, tl, qk), tok),
                  pl.BlockSpec((1, tl, vw), tok), pl.BlockSpec((1, tl, vw), tok),
                  pl.BlockSpec((1, tl, qk), tok),
                  pl.BlockSpec((1, H, dk, dv), st),
                  pl.BlockSpec((1, dv), lambda b, i: (0, 0))],
        out_specs=[pl.BlockSpec((1, tl, vw), tok),
                   pl.BlockSpec((1, H, dk, dv), st)],
        out_shape=[jax.ShapeDtypeStruct((B, L, vw), F32),
                   jax.ShapeDtypeStruct((B, H, dk, dv), F32)],
        scratch_shapes=[pltpu.VMEM((H, dk, dv), F32)],
        compiler_params=_params("arbitrary", "arbitrary"),
        name="gla_mixer",
    )(q, k, v, sr, la, s0, g_norm.reshape(1, dv))


def _att_prompt_kernel(q_ref, kp_ref, kc_ref, vp_ref, vc_ref, o_ref, l_ref):
    blk = q_ref.shape[1]
    first = pl.program_id(2) == 0
    row = lax.broadcasted_iota(jnp.int32, (blk, 2 * blk), 0)
    col = lax.broadcasted_iota(jnp.int32, (blk, 2 * blk), 1)
    valid = (col >= row) & (col <= row + blk) & jnp.logical_or(col >= blk, jnp.logical_not(first))
    q = q_ref[0].astype(BF16)
    kk = jnp.concatenate([kp_ref[0], kc_ref[0]], axis=0).astype(BF16)
    vv = jnp.concatenate([vp_ref[0], vc_ref[0]], axis=0).astype(BF16)
    nt = (((1,), (1,)), ((), ()))
    for h in range(ATT_HEADS):
        hs = slice(h * ATT_HEAD_DIM, (h + 1) * ATT_HEAD_DIM)
        s = lax.dot_general(q[:, hs], kk[:, hs], nt, preferred_element_type=F32)
        s = jnp.where(valid, s, NEG_INF)
        m = jnp.max(s, axis=-1, keepdims=True)
        p = jnp.exp(s - m)
        den = jnp.sum(p, axis=-1, keepdims=True)
        o = jnp.dot(p.astype(BF16), vv[:, hs], preferred_element_type=F32) / den
        o_ref[0, :, hs] = o
        l_ref[0, :, hs] = jnp.broadcast_to(m + jnp.log(den), (blk, ATT_HEAD_DIM))


def _att_prompt_group(q, k, v, g):
    B, L, gw = q.shape
    G = len(ATT_GROUPS)
    width = gw // G
    W, d = ATT_GROUPS[g]
    blk = W // d
    assert blk == ATT_BLOCK and L % (d * blk) == 0
    Ld = L // d
    qv, kv_, vv = (t.reshape(B, Ld, d * gw) for t in (q, k, v))
    cur = lambda b, r, i: (b, i, r * G + g)
    prev = lambda b, r, i: (b, jnp.maximum(i - 1, 0), r * G + g)
    out = lambda b, r, i: (b, i, r)
    o, lse = pl.pallas_call(
        _att_prompt_kernel,
        grid=(B, d, Ld // blk),
        in_specs=[pl.BlockSpec((1, blk, width), cur),
                  pl.BlockSpec((1, blk, width), prev), pl.BlockSpec((1, blk, width), cur),
                  pl.BlockSpec((1, blk, width), prev), pl.BlockSpec((1, blk, width), cur)],
        out_specs=[pl.BlockSpec((1, blk, width), out), pl.BlockSpec((1, blk, width), out)],
        out_shape=[jax.ShapeDtypeStruct((B, Ld, d * width), F32),
                   jax.ShapeDtypeStruct((B, Ld, d * width), F32)],
        compiler_params=_params("arbitrary", "arbitrary", "arbitrary"),
        name="att_prompt_g%d" % g,
    )(qv, kv_, kv_, vv, vv)
    return o.reshape(B, L, width), lse.reshape(B, L, width)


def _att_sample_kernel(q_ref, k_ref, v_ref, sz_ref, e_ref, et_ref,
                       ck0_ref, cv0_ref, ck1_ref, cv1_ref, ck2_ref, cv2_ref, y_ref, *, nbuf):
    n_new = q_ref.shape[1]
    width = sz_ref.shape[-1]
    E = e_ref[...]
    Et = et_ref[...]
    caches = ((ck0_ref, cv0_ref), (ck1_ref, cv1_ref), (ck2_ref, cv2_ref))
    for i in range(n_new):
        scores, values = [], []
        for g, (W, d) in enumerate(ATT_GROUPS):
            ck_ref, cv_ref = caches[g]
            gs = slice(g * width, (g + 1) * width)
            qi = q_ref[0, i:i + 1, gs]
            res = i % d
            cs = slice(res * width, (res + 1) * width)
            sc = jnp.dot((ck_ref[0, :, cs] * qi).astype(BF16), E, preferred_element_type=F32)
            m_idx = lax.broadcasted_iota(jnp.int32, sc.shape, 0)
            sc = jnp.where(res + d * m_idx >= i - (W - nbuf[g]), sc, NEG_INF)
            sn = jnp.dot((k_ref[0, :, gs] * qi).astype(BF16), E, preferred_element_type=F32)
            j_idx = lax.broadcasted_iota(jnp.int32, sn.shape, 0)
            ok = (j_idx <= i) & (((i - j_idx) & (d - 1)) == 0)
            sn = jnp.where(ok, sn, NEG_INF)
            scores += [sc, sn]
            values += [cv_ref[0, :, cs], v_ref[0, :, gs]]
        m = functools.reduce(jnp.maximum, [jnp.max(s, axis=0, keepdims=True) for s in scores])
        ps = [jnp.exp(s - m) for s in scores]
        den = functools.reduce(jnp.add, [jnp.sum(p, axis=0, keepdims=True) for p in ps])
        inv = 1.0 / den
        acc = jnp.zeros((1, width), F32)
        for p, val in zip(ps, values):
            pe = jnp.dot((p * inv).astype(BF16), Et, preferred_element_type=F32)
            acc = acc + jnp.sum(pe * val, axis=0, keepdims=True)
        y_ref[0, i:i + 1, :] = acc * sz_ref[0, i:i + 1, :]


def _att_sample(q, k, v, sz, caches_k, caches_v):
    B, n_new, gw = q.shape
    width = sz.shape[-1]
    lane = jnp.arange(LANES)[None, :]
    E = (jnp.arange(width)[:, None] // ATT_HEAD_DIM == lane).astype(BF16)
    specs, args, nbuf = [], [], []
    for g, (W, d) in enumerate(ATT_GROUPS):
        nb = caches_k[g].shape[1]
        assert nb % d == 0 and (W - nb) % d == 0
        nbuf.append(nb)
        n_res = min(d, n_new)
        assert all(((i + W - nb) % d) < n_res for i in range(n_new))
        for c in (caches_k[g], caches_v[g]):
            args.append(c.reshape(B, nb // d, d * width))
            specs.append(pl.BlockSpec((1, nb // d, n_res * width), lambda b: (b, 0, 0)))
    seq = lambda b: (b, 0, 0)
    return pl.pallas_call(
        functools.partial(_att_sample_kernel, nbuf=tuple(nbuf)),
        grid=(B,),
        in_specs=[pl.BlockSpec((1, n_new, gw), seq)] * 3 + [pl.BlockSpec((1, n_new, width), seq),
                  pl.BlockSpec((width, LANES), lambda b: (0, 0)),
                  pl.BlockSpec((LANES, width), lambda b: (0, 0))] + specs,
        out_specs=pl.BlockSpec((1, n_new, width), seq),
        out_shape=jax.ShapeDtypeStruct((B, n_new, width), F32),
        compiler_params=_params("arbitrary"),
        name="att_sample",
    )(q, k, v, sz, E, E.T, *args)


def _rope_tables(pos):
    half = ATT_HEAD_DIM // 2
    inv = ROPE_THETA ** (-jnp.arange(half, dtype=F32) / half)
    ang = pos.astype(F32)[:, None] * inv[None, :]
    cos, sin = jnp.cos(ang), jnp.sin(ang)
    reps = LANES // ATT_HEAD_DIM
    cos_t = jnp.tile(jnp.concatenate([cos, cos], axis=-1), (1, reps))
    sin_t = jnp.tile(jnp.concatenate([-sin, sin], axis=-1), (1, reps))
    return cos_t, sin_t


def kernel(x_prompt, x_sample, state_conv, state_gla, cache_k_g0, cache_v_g0, cache_k_g1, cache_v_g1, cache_k_g2, cache_v_g2, c_prompt, c_sample, w_ada, b_ada, g_pre, g_post, w_conv_in, w_dw, b_dw, g_conv_ln, b_conv_ln, w_conv_out, w_gla_in, w_gla_a1, w_gla_a2, b_gla_a, g_gla_norm, w_gla_out, w_att_in, w_att_out):
    depth = w_ada.shape[0]
    D = x_prompt.shape[-1]
    G = len(ATT_GROUPS)
    xs = [x_prompt, x_sample]
    nseq = [x_prompt.shape[0], x_sample.shape[0]]
    caches_k = (cache_k_g0, cache_k_g1, cache_k_g2)
    caches_v = (cache_v_g0, cache_v_g1, cache_v_g2)

    n_c = nseq[0] + nseq[1]
    pad = (-n_c) % SUBLANES
    c_all = jnp.concatenate([c_prompt, c_sample, jnp.zeros((pad, D), F32)], axis=0)
    mod_all = _modulation(c_all, w_ada, b_ada)
    mods = [[mod_all[l, :nseq[0]].reshape(nseq[0], 1, 3 * D),
             mod_all[l, nseq[0]:n_c].reshape(nseq[1], 1, 3 * D)] for l in range(depth)]

    pos = [jnp.arange(x_prompt.shape[1]), PAST_LEN + jnp.arange(x_sample.shape[1])]
    rope = [_rope_tables(p) for p in pos]

    conv_new, gla_new = ([], []), ([], [])
    k_new = ([[] for _ in range(G)], [[] for _ in range(G)])
    v_new = ([[] for _ in range(G)], [[] for _ in range(G)])
    no_extra = lambda nb, tl: []

    for l in range(depth):
        kind, j = l % N_MIXERS, l // N_MIXERS
        for grp in range(2):
            x, mod = xs[grp], mods[l][grp]
            B, L, _ = x.shape
            if kind == 0:
                C = w_dw.shape[-1]
                u, sz = _pre_call(_pre_conv_kernel, "pre_conv", x, mod, g_pre[l],
                                  [w_conv_in[j].astype(BF16)], no_extra, [], [C, C])
                buf = jnp.zeros((B, CONV_SIZE - 1, C), F32) if grp == 0 else state_conv[j]
                y, st = _conv_mixer(u, sz, buf, w_dw[j], b_dw[j], g_conv_ln[j], b_conv_ln[j])
                conv_new[grp].append(st)
                xs[grp] = _post_call(_post_kernel, "post_conv", [y], w_conv_out[j].astype(BF16), g_post[l], mod, x)
            elif kind == 1:
                qk = w_gla_a2.shape[-1]
                vw = w_gla_out.shape[1]
                wa1 = jnp.zeros((D, LANES), F32).at[:, :GLA_RANK].set(w_gla_a1[j]).astype(BF16)
                wa2 = jnp.zeros((LANES, qk), F32).at[:GLA_RANK].set(w_gla_a2[j]).astype(BF16)
                q, k, v, sr, la = _pre_call(
                    _pre_gla_kernel, "pre_gla", x, mod, g_pre[l],
                    [w_gla_in[j].astype(BF16), wa1, wa2, b_gla_a[j].reshape(1, qk)], no_extra, [],
                    [qk, qk, vw, vw, qk])
                dk, dv = qk // GLA_HEADS, vw // GLA_HEADS
                s0 = jnp.zeros((B, GLA_HEADS, dk, dv), F32) if grp == 0 else state_gla[j]
                y, st = _gla_mixer(q, k, v, sr, la, s0, g_gla_norm[j])
                gla_new[grp].append(st)
                xs[grp] = _post_call(_post_kernel, "post_gla", [y], w_gla_out[j].astype(BF16), g_post[l], mod, x)
            else:
                gw = G * ATT_WIDTH
                cos_t, sin_t = rope[grp]
                rope_specs = lambda nb, tl: [pl.BlockSpec((tl, LANES), lambda b, i: (i, 0))] * 2
                q, k, v, sz = _pre_call(_pre_att_kernel, "pre_att", x, mod, g_pre[l],
                                        [w_att_in[j].astype(BF16)], rope_specs, [cos_t, sin_t],
                                        [gw, gw, gw, ATT_WIDTH])
                heads = (ATT_HEADS, ATT_HEAD_DIM)
                if grp == 0:
                    outs = [_att_prompt_group(q, k, v, g) for g in range(G)]
                    ys = [o for o, _ in outs] + [s for _, s in outs] + [sz]
                    xs[grp] = _post_call(_post_att_kernel, "post_att_prompt", ys, w_att_out[j].astype(BF16),
                                         g_post[l], mod, x)
                    for g, (W, _) in enumerate(ATT_GROUPS):
                        keep = min(W, L)
                        gs = slice(g * ATT_WIDTH, (g + 1) * ATT_WIDTH)
                        k_new[grp][g].append(k[:, L - keep:, gs].reshape(B, keep, *heads))
                        v_new[grp][g].append(v[:, L - keep:, gs].reshape(B, keep, *heads))
                else:
                    ck = [c[j] for c in caches_k]
                    cv = [c[j] for c in caches_v]
                    y = _att_sample(q, k, v, sz, ck, cv)
                    xs[grp] = _post_call(_post_kernel, "post_att_sample", [y], w_att_out[j].astype(BF16),
                                         g_post[l], mod, x)
                    for g in range(G):
                        gs = slice(g * ATT_WIDTH, (g + 1) * ATT_WIDTH)
                        k_new[grp][g].append(k[:, :, gs].reshape(B, L, *heads))
                        v_new[grp][g].append(v[:, :, gs].reshape(B, L, *heads))

    outs = [xs[0], xs[1], jnp.stack(conv_new[0]), jnp.stack(conv_new[1]),
            jnp.stack(gla_new[0]), jnp.stack(gla_new[1])]
    for grp in range(2):
        for g in range(G):
            outs += [jnp.stack(k_new[grp][g]), jnp.stack(v_new[grp][g])]
    return tuple(outs)
```

```python
import functools
import math

import jax
import jax.numpy as jnp
from jax import lax
from jax.experimental import pallas as pl
from jax.experimental.pallas import tpu as pltpu

F32 = jnp.float32
BF16 = jnp.bfloat16
HIGHEST = lax.Precision.HIGHEST

PAST_LEN = 2048
N_MIXERS = 3
CONV_SIZE = 31
GLA_HEADS = 4
GLA_RANK = 16
GLA_TAU = 16.0
ATT_GROUPS = ((128, 1), (512, 4), (2048, 16))
ATT_HEADS = 8
ATT_HEAD_DIM = 64
ATT_WIDTH = ATT_HEADS * ATT_HEAD_DIM
ROPE_THETA = 10000.0
EPS = 1e-6
NEG_INF = -1e30

LANES = 128
SUBLANES = 8
VMEM_LIMIT_BYTES = 56 * 1024 * 1024

PROJ_ROWS = 512
CONV_ROWS = 256
CONV_CHUNK = 32
CONV_HALO = 32
GLA_ROWS = 256
GLA_CHUNK = 64
ATT_BLOCK = 128


def _params(*sem):
    return pltpu.CompilerParams(dimension_semantics=sem, vmem_limit_bytes=VMEM_LIMIT_BYTES)


def _token_tiling(B, L, rows=PROJ_ROWS):
    if L >= rows:
        return 1, rows
    nb = max(1, rows // L)
    while B % nb:
        nb //= 2
    return nb, L


def _mod_kernel(c_ref, w_ref, b_ref, o_ref):
    o_ref[0] = jnp.dot(c_ref[...], w_ref[0], precision=HIGHEST, preferred_element_type=F32) + b_ref[0]


def _modulation(c_all, w_ada, b_ada):
    depth, D, N = w_ada.shape
    R = c_all.shape[0]
    tn = 1024
    return pl.pallas_call(
        _mod_kernel,
        grid=(depth, N // tn),
        in_specs=[pl.BlockSpec((R, D), lambda l, j: (0, 0)),
                  pl.BlockSpec((1, D, tn), lambda l, j: (l, 0, j)),
                  pl.BlockSpec((1, 1, tn), lambda l, j: (l, 0, j))],
        out_specs=pl.BlockSpec((1, R, tn), lambda l, j: (l, 0, j)),
        out_shape=jax.ShapeDtypeStruct((depth, R, N), F32),
        compiler_params=_params("arbitrary", "arbitrary"),
        name="adaln_mod",
    )(c_all, w_ada, b_ada.reshape(depth, 1, N))


def _modulated_norm(x_ref, mod_ref, g_ref):
    x = x_ref[...]
    nb, tl, D = x.shape
    ms = jnp.mean(x * x, axis=-1, keepdims=True)
    y = x * lax.rsqrt(ms + EPS) * g_ref[...]
    shift = mod_ref[:, :, 0:D]
    scale = mod_ref[:, :, D:2 * D]
    h = y * (1.0 + scale) + shift
    return h.reshape(nb * tl, D).astype(BF16)


def _pre_conv_kernel(x_ref, mod_ref, g_ref, w_ref, u_ref, sz_ref):
    h = _modulated_norm(x_ref, mod_ref, g_ref)
    res = jnp.dot(h, w_ref[...], preferred_element_type=F32)
    C = u_ref.shape[-1]
    u = res[:, :C] * jax.nn.sigmoid(res[:, C:2 * C])
    z = res[:, 2 * C:]
    u_ref[...] = u.reshape(u_ref.shape)
    sz_ref[...] = (z * jax.nn.sigmoid(z)).reshape(sz_ref.shape)


def _pre_gla_kernel(x_ref, mod_ref, g_ref, w_ref, wa1_ref, wa2_ref, ba_ref,
                    q_ref, k_ref, v_ref, sr_ref, la_ref):
    h = _modulated_norm(x_ref, mod_ref, g_ref)
    res = jnp.dot(h, w_ref[...], preferred_element_type=F32)
    qk = q_ref.shape[-1]
    vw = v_ref.shape[-1]
    dk = qk // GLA_HEADS
    q_ref[...] = (res[:, :qk] * (dk ** -0.5)).reshape(q_ref.shape)
    k_ref[...] = res[:, qk:2 * qk].reshape(k_ref.shape)
    v_ref[...] = res[:, 2 * qk:2 * qk + vw].reshape(v_ref.shape)
    r = res[:, 2 * qk + vw:]
    sr_ref[...] = (r * jax.nn.sigmoid(r)).reshape(sr_ref.shape)
    low = jnp.dot(h, wa1_ref[...], preferred_element_type=F32)
    zg = jnp.dot(low.astype(BF16), wa2_ref[...], preferred_element_type=F32) + ba_ref[...]
    log_sig = jnp.minimum(zg, 0.0) - jnp.log(1.0 + jnp.exp(-jnp.abs(zg)))
    la_ref[...] = (log_sig * (1.0 / GLA_TAU)).reshape(la_ref.shape)


def _swap_halves(x):
    half = ATT_HEAD_DIM // 2
    lane = lax.broadcasted_iota(jnp.int32, x.shape, 1)
    lower = (lane % ATT_HEAD_DIM) < half
    return jnp.where(lower, pltpu.roll(x, LANES - half, 1), pltpu.roll(x, half, 1))


def _pre_att_kernel(x_ref, mod_ref, g_ref, w_ref, cos_ref, sin_ref, q_ref, k_ref, v_ref, sz_ref):
    h = _modulated_norm(x_ref, mod_ref, g_ref)
    res = jnp.dot(h, w_ref[...], preferred_element_type=F32)
    nb, tl, gw = q_ref.shape
    rows = nb * tl
    cos = cos_ref[...]
    sin = sin_ref[...]

    def rope(col0, out_ref, mult):
        for c in range(gw // LANES):
            xc = res[:, col0 + c * LANES: col0 + (c + 1) * LANES].reshape(nb, tl, LANES)
            sw = _swap_halves(xc.reshape(rows, LANES)).reshape(nb, tl, LANES)
            val = xc * cos + sw * sin
            if mult != 1.0:
                val = val * mult
            out_ref[:, :, c * LANES:(c + 1) * LANES] = val

    rope(0, q_ref, ATT_HEAD_DIM ** -0.5)
    rope(gw, k_ref, 1.0)
    v_ref[...] = res[:, 2 * gw:3 * gw].reshape(v_ref.shape)
    z = res[:, 3 * gw:]
    sz_ref[...] = (z * jax.nn.sigmoid(z)).reshape(sz_ref.shape)


def _pre_call(kernel, name, x, mod, g_pre, weights, extra_specs, extras, out_widths):
    B, L, D = x.shape
    nb, tl = _token_tiling(B, L, PROJ_ROWS if sum(out_widths) <= 3 * D else PROJ_ROWS // 2)
    tok = lambda b, i: (b, i, 0)
    in_specs = [pl.BlockSpec((nb, tl, D), tok),
                pl.BlockSpec((nb, 1, mod.shape[-1]), lambda b, i: (b, 0, 0)),
                pl.BlockSpec((1, D), lambda b, i: (0, 0))]
    in_specs += [pl.BlockSpec(w.shape, lambda b, i: (0, 0)) for w in weights]
    in_specs += extra_specs(nb, tl)
    return pl.pallas_call(
        kernel,
        grid=(B // nb, L // tl),
        in_specs=in_specs,
        out_specs=[pl.BlockSpec((nb, tl, n), tok) for n in out_widths],
        out_shape=[jax.ShapeDtypeStruct((B, L, n), F32) for n in out_widths],
        compiler_params=_params("arbitrary", "arbitrary"),
        name=name,
    )(x, mod, g_pre.reshape(1, D), *weights, *extras)


def _gated_residual(y_bf16, w_ref, g_ref, mod_ref, x_ref, o_ref):
    nb, tl, D = x_ref.shape
    o = jnp.dot(y_bf16, w_ref[...], preferred_element_type=F32)
    ms = jnp.mean(o * o, axis=-1, keepdims=True)
    o = (o * lax.rsqrt(ms + EPS) * g_ref[...]).reshape(nb, tl, D)
    gate = mod_ref[:, :, 2 * D:3 * D]
    o_ref[...] = x_ref[...] + gate * o


def _post_kernel(y_ref, w_ref, g_ref, mod_ref, x_ref, o_ref):
    nb, tl, K = y_ref.shape
    _gated_residual(y_ref[...].reshape(nb * tl, K).astype(BF16), w_ref, g_ref, mod_ref, x_ref, o_ref)


def _post_gate_kernel(y_ref, sz_ref, w_ref, g_ref, mod_ref, x_ref, o_ref):
    nb, tl, K = y_ref.shape
    y = y_ref[...] * sz_ref[...]
    _gated_residual(y.reshape(nb * tl, K).astype(BF16), w_ref, g_ref, mod_ref, x_ref, o_ref)


def _post_att_kernel(o0_ref, o1_ref, o2_ref, l0_ref, l1_ref, l2_ref, sz_ref, w_ref, g_ref, mod_ref, x_ref, o_ref):
    nb, tl, K = sz_ref.shape
    l0, l1, l2 = l0_ref[...], l1_ref[...], l2_ref[...]
    m = jnp.maximum(jnp.maximum(l0, l1), l2)
    e0, e1, e2 = jnp.exp(l0 - m), jnp.exp(l1 - m), jnp.exp(l2 - m)
    att = (e0 * o0_ref[...] + e1 * o1_ref[...] + e2 * o2_ref[...]) / (e0 + e1 + e2)
    y = att * sz_ref[...]
    _gated_residual(y.reshape(nb * tl, K).astype(BF16), w_ref, g_ref, mod_ref, x_ref, o_ref)


def _post_call(kernel, name, ys, w_out, g_post, mod, x):
    B, L, D = x.shape
    K = w_out.shape[0]
    nb, tl = _token_tiling(B, L)
    tok = lambda b, i: (b, i, 0)
    in_specs = [pl.BlockSpec((nb, tl, K), tok) for _ in ys]
    in_specs += [pl.BlockSpec((K, D), lambda b, i: (0, 0)),
                 pl.BlockSpec((1, D), lambda b, i: (0, 0)),
                 pl.BlockSpec((nb, 1, mod.shape[-1]), lambda b, i: (b, 0, 0)),
                 pl.BlockSpec((nb, tl, D), tok)]
    return pl.pallas_call(
        kernel,
        grid=(B // nb, L // tl),
        in_specs=in_specs,
        out_specs=pl.BlockSpec((nb, tl, D), tok),
        out_shape=jax.ShapeDtypeStruct((B, L, D), F32),
        compiler_params=_params("arbitrary", "arbitrary"),
        name=name,
    )(*ys, w_out, g_post.reshape(1, D), mod, x)


def _conv_kernel(u_ref, sz_ref, buf_ref, w_ref, b_ref, g_ref, bl_ref, y_ref, st_ref, ext_ref, sh_ref):
    nb, tl, C = u_ref.shape
    keep = CONV_SIZE - 1
    lead = CONV_HALO - keep
    rows_sh = sh_ref.shape[1]

    @pl.when(pl.program_id(1) == 0)
    def _():
        ext_ref[:, lead:CONV_HALO, :] = buf_ref[...]

    ext_ref[:, CONV_HALO:CONV_HALO + tl, :] = u_ref[...]
    chunk = min(tl, CONV_CHUNK)
    bias = b_ref[...]
    gamma = g_ref[...]
    beta = bl_ref[...]
    for n in range(nb):
        for s in range(1, SUBLANES):
            sh_ref[s - 1] = ext_ref[n, s:s + rows_sh, :]

        def rows_block(r0):
            acc = jnp.broadcast_to(bias, (chunk, C))
            for j in range(CONV_SIZE):
                a, s = divmod(lead + j, SUBLANES)
                start = pl.multiple_of(r0 + a * SUBLANES, SUBLANES)
                if s == 0:
                    win = ext_ref[n, pl.ds(start, chunk), :]
                else:
                    win = sh_ref[s - 1, pl.ds(start, chunk), :]
                acc = acc + w_ref[j:j + 1, :] * win
            mu = jnp.mean(acc, axis=-1, keepdims=True)
            xc = acc - mu
            var = jnp.mean(xc * xc, axis=-1, keepdims=True)
            ln = xc * lax.rsqrt(var + EPS) * gamma + beta
            out_rows = pl.ds(pl.multiple_of(r0, SUBLANES), chunk)
            y_ref[n, out_rows, :] = ln * jax.nn.sigmoid(ln) * sz_ref[n, out_rows, :]

        if tl == chunk:
            rows_block(0)
        else:
            lax.fori_loop(0, tl // chunk, lambda ci, c: (rows_block(ci * chunk), c)[1], 0)
    tail = ext_ref[:, lead + tl: CONV_HALO + tl, :]
    st_ref[...] = tail
    ext_ref[:, lead:CONV_HALO, :] = tail


def _conv_mixer(u, sz, buf, w_dw, b_dw, g_ln, b_ln):
    B, L, C = u.shape
    keep = CONV_SIZE - 1
    if L >= CONV_ROWS:
        nb, tl = 1, CONV_ROWS
    else:
        nb, tl = SUBLANES, L
    rows_sh = tl + CONV_HALO - SUBLANES
    tok = lambda b, i: (b, i, 0)
    vec = lambda b, i: (0, 0)
    return pl.pallas_call(
        _conv_kernel,
        grid=(B // nb, L // tl),
        in_specs=[pl.BlockSpec((nb, tl, C), tok),
                  pl.BlockSpec((nb, tl, C), tok),
                  pl.BlockSpec((nb, keep, C), lambda b, i: (b, 0, 0)),
                  pl.BlockSpec((CONV_SIZE, C), vec),
                  pl.BlockSpec((1, C), vec), pl.BlockSpec((1, C), vec), pl.BlockSpec((1, C), vec)],
        out_specs=[pl.BlockSpec((nb, tl, C), tok),
                   pl.BlockSpec((nb, keep, C), lambda b, i: (b, 0, 0))],
        out_shape=[jax.ShapeDtypeStruct((B, L, C), F32),
                   jax.ShapeDtypeStruct((B, keep, C), F32)],
        scratch_shapes=[pltpu.VMEM((nb, CONV_HALO + tl, C), F32),
                        pltpu.VMEM((SUBLANES - 1, rows_sh, C), F32)],
        compiler_params=_params("arbitrary", "arbitrary"),
        name="conv_mixer",
    )(u, sz, buf, w_dw, b_dw.reshape(1, C), g_ln.reshape(1, C), b_ln.reshape(1, C))


def _gla_kernel(q_ref, k_ref, v_ref, sr_ref, la_ref, s0_ref, g_ref, y_ref, sout_ref, S_ref, *, chunk):
    _, tl, qk = q_ref.shape
    vw = v_ref.shape[-1]
    H = GLA_HEADS
    dk, dv = qk // H, vw // H

    @pl.when(pl.program_id(1) == 0)
    def _():
        S_ref[...] = s0_ref[0]

    row = lax.broadcasted_iota(jnp.int32, (chunk, chunk), 0)
    col = lax.broadcasted_iota(jnp.int32, (chunk, chunk), 1)
    causal = row >= col
    tril = causal.astype(F32)
    ones = jnp.ones((chunk, dv), F32)
    gamma = g_ref[...]
    tn = (((0,), (0,)), ((), ()))
    nt = (((1,), (1,)), ((), ()))

    for c0 in range(0, tl, chunk):
        rows = slice(c0, c0 + chunk)
        la = la_ref[0, rows, :]
        bc = jnp.dot(tril, la, precision=HIGHEST, preferred_element_type=F32)
        btot = bc[chunk - 1:chunk, :]
        qt = q_ref[0, rows, :] * jnp.exp(bc)
        kk = k_ref[0, rows, :]
        kt = kk * jnp.exp(-bc)
        kh = kk * jnp.exp(btot - bc)
        for h in range(H):
            ks = slice(h * dk, (h + 1) * dk)
            vs = slice(h * dv, (h + 1) * dv)
            qh = qt[:, ks].astype(BF16)
            vh = v_ref[0, rows, vs].astype(BF16)
            att = lax.dot_general(qh, kt[:, ks].astype(BF16), nt, preferred_element_type=F32)
            att = jnp.where(causal, att, 0.0)
            S = S_ref[h]
            o = (jnp.dot(att.astype(BF16), vh, preferred_element_type=F32)
                 + jnp.dot(qh, S.astype(BF16), preferred_element_type=F32))
            decay = jnp.exp(lax.dot_general(la[:, ks], ones, tn, precision=HIGHEST,
                                            preferred_element_type=F32))
            S_ref[h] = decay * S + lax.dot_general(kh[:, ks].astype(BF16), vh, tn, preferred_element_type=F32)
            ms = jnp.mean(o * o, axis=-1, keepdims=True)
            y_ref[0, rows, vs] = o * lax.rsqrt(ms + EPS) * gamma * sr_ref[0, rows, vs]

    sout_ref[0] = S_ref[...]


def _gla_mixer(q, k, v, sr, la, s0, g_norm):
    B, L, qk = q.shape
    vw = v.shape[-1]
    H = GLA_HEADS
    dk, dv = qk // H, vw // H
    tl = min(L, GLA_ROWS)
    chunk = min(L, GLA_CHUNK)
    tok = lambda b, i: (b, i, 0)
    st = lambda b, i: (b, 0, 0, 0)
    return pl.pallas_call(
        functools.partial(_gla_kernel, chunk=chunk),
        grid=(B, L // tl),
        in_specs=[pl.BlockSpec((1, tl, qk), tok), pl.BlockSpec((1, tl, qk), tok),
                  pl.BlockSpec((1, tl, vw), tok), pl.BlockSpec((1, tl, vw), tok),
                  pl.BlockSpec((1, tl, qk), tok),
                  pl.BlockSpec((1, H, dk, dv), st),
                  pl.BlockSpec((1, dv), lambda b, i: (0, 0))],
        out_specs=[pl.BlockSpec((1, tl, vw), tok),
                   pl.BlockSpec((1, H, dk, dv), st)],
        out_shape=[jax.ShapeDtypeStruct((B, L, vw), F32),
                   jax.ShapeDtypeStruct((B, H, dk, dv), F32)],
        scratch_shapes=[pltpu.VMEM((H, dk, dv), F32)],
        compiler_params=_params("arbitrary", "arbitrary"),
        name="gla_mixer",
    )(q, k, v, sr, la, s0, g_norm.reshape(1, dv))


def _att_prompt_kernel(q_ref, kp_ref, kc_ref, vp_ref, vc_ref, o_ref, l_ref):
    blk = q_ref.shape[1]
    first = pl.program_id(2) == 0
    row = lax.broadcasted_iota(jnp.int32, (blk, 2 * blk), 0)
    col = lax.broadcasted_iota(jnp.int32, (blk, 2 * blk), 1)
    valid = (col >= row) & (col <= row + blk) & jnp.logical_or(col >= blk, jnp.logical_not(first))
    q = q_ref[0].astype(BF16)
    kk = jnp.concatenate([kp_ref[0], kc_ref[0]], axis=0).astype(BF16)
    vv = jnp.concatenate([vp_ref[0], vc_ref[0]], axis=0).astype(BF16)
    nt = (((1,), (1,)), ((), ()))
    for h in range(ATT_HEADS):
        hs = slice(h * ATT_HEAD_DIM, (h + 1) * ATT_HEAD_DIM)
        s = lax.dot_general(q[:, hs], kk[:, hs], nt, preferred_element_type=F32)
        s = jnp.where(valid, s, NEG_INF)
        m = jnp.max(s, axis=-1, keepdims=True)
        p = jnp.exp(s - m)
        den = jnp.sum(p, axis=-1, keepdims=True)
        o = jnp.dot(p.astype(BF16), vv[:, hs], preferred_element_type=F32) / den
        o_ref[0, :, hs] = o
        l_ref[0, :, hs] = jnp.broadcast_to(m + jnp.log(den), (blk, ATT_HEAD_DIM))


def _att_prompt_group(q, k, v, g):
    B, L, gw = q.shape
    G = len(ATT_GROUPS)
    width = gw // G
    W, d = ATT_GROUPS[g]
    blk = W // d
    assert blk == ATT_BLOCK and L % (d * blk) == 0
    Ld = L // d
    qv, kv_, vv = (t.reshape(B, Ld, d * gw) for t in (q, k, v))
    cur = lambda b, r, i: (b, i, r * G + g)
    prev = lambda b, r, i: (b, jnp.maximum(i - 1, 0), r * G + g)
    out = lambda b, r, i: (b, i, r)
    o, lse = pl.pallas_call(
        _att_prompt_kernel,
        grid=(B, d, Ld // blk),
        in_specs=[pl.BlockSpec((1, blk, width), cur),
                  pl.BlockSpec((1, blk, width), prev), pl.BlockSpec((1, blk, width), cur),
                  pl.BlockSpec((1, blk, width), prev), pl.BlockSpec((1, blk, width), cur)],
        out_specs=[pl.BlockSpec((1, blk, width), out), pl.BlockSpec((1, blk, width), out)],
        out_shape=[jax.ShapeDtypeStruct((B, Ld, d * width), F32),
                   jax.ShapeDtypeStruct((B, Ld, d * width), F32)],
        compiler_params=_params("arbitrary", "arbitrary", "arbitrary"),
        name="att_prompt_g%d" % g,
    )(qv, kv_, kv_, vv, vv)
    return o.reshape(B, L, width), lse.reshape(B, L, width)


def _sample_segments(n_new, nbuf):
    segs = []
    for g, (W, d) in enumerate(ATT_GROUPS):
        for res in range(min(d, n_new)):
            segs.append((g, res, nbuf[g] // d))
        segs.append((g, None, n_new))
    return segs


def _sample_bias(seg, n_new, nbuf):
    g, res, nrow = seg
    W, d = ATT_GROUPS[g]
    H = ATT_HEADS
    assert H & (H - 1) == 0
    shift = H.bit_length() - 1
    shape = (n_new * H, nrow * H)
    r = lax.broadcasted_iota(jnp.int32, shape, 0)
    c = lax.broadcasted_iota(jnp.int32, shape, 1)
    i, hq = r >> shift, r & (H - 1)
    key, hk = c >> shift, c & (H - 1)
    if res is None:
        ok = (key <= i) & (((i - key) & (d - 1)) == 0)
    else:
        ok = ((i & (d - 1)) == res) & (res + d * key >= i - (W - nbuf[g]))
    return jnp.where(ok & (hq == hk), 0.0, NEG_INF).astype(F32)


def _att_sample_kernel(q_ref, k_ref, v_ref, ck0_ref, cv0_ref, ck1_ref, cv1_ref, ck2_ref, cv2_ref,
                       y_ref, bias_ref, s_ref, *, nbuf):
    n_new, G, H, Dh = q_ref.shape[1:]
    segs = _sample_segments(n_new, nbuf)
    caches = ((ck0_ref, cv0_ref), (ck1_ref, cv1_ref), (ck2_ref, cv2_ref))

    @pl.when(pl.program_id(0) == 0)
    def _():
        for n, seg in enumerate(segs):
            cols = seg[2] * H
            bias_ref[n, :, :cols] = _sample_bias(seg, n_new, nbuf)

    def keys_values(seg):
        g, res, nrow = seg
        if res is None:
            kk, vv = k_ref[0, :, g], v_ref[0, :, g]
        else:
            kk, vv = caches[g][0][0, :, res], caches[g][1][0, :, res]
        return kk.reshape(nrow * H, Dh).astype(BF16), vv.reshape(nrow * H, Dh).astype(BF16)

    nt = (((1,), (1,)), ((), ()))
    qs = [q_ref[0, :, g].reshape(n_new * H, Dh).astype(BF16) for g in range(G)]
    m = jnp.full((n_new * H, LANES), NEG_INF, F32)
    for n, seg in enumerate(segs):
        cols = seg[2] * H
        kk, _ = keys_values(seg)
        s = lax.dot_general(qs[seg[0]], kk, nt, preferred_element_type=F32) + bias_ref[n, :, :cols]
        s_ref[n, :, :cols] = s
        for c0 in range(0, cols, LANES):
            part = s[:, c0:c0 + LANES]
            if part.shape[1] < LANES:
                part = jnp.concatenate([part, jnp.full((n_new * H, LANES - part.shape[1]), NEG_INF, F32)], axis=1)
            m = jnp.maximum(m, part)
    m = jnp.max(m, axis=-1, keepdims=True)
    acc = jnp.zeros((n_new * H, Dh), F32)
    den = jnp.zeros((n_new * H, 1), F32)
    for n, seg in enumerate(segs):
        cols = seg[2] * H
        _, vv = keys_values(seg)
        p = jnp.exp(s_ref[n, :, :cols] - m).astype(BF16)
        den = den + jnp.sum(p.astype(F32), axis=-1, keepdims=True)
        acc = acc + jnp.dot(p, vv, preferred_element_type=F32)
    y_ref[0] = (acc / den).reshape(n_new, H, Dh)


def _att_sample(q, k, v, caches_k, caches_v):
    B, n_new, G, H, Dh = q.shape
    specs, args, nbuf = [], [], []
    for g, (W, d) in enumerate(ATT_GROUPS):
        nb = caches_k[g].shape[1]
        assert d & (d - 1) == 0 and nb % d == 0 and (W - nb) % d == 0
        nbuf.append(nb)
        n_res = min(d, n_new)
        for c in (caches_k[g], caches_v[g]):
            args.append(c.reshape(B, nb // d, d, H, Dh))
            specs.append(pl.BlockSpec((1, nb // d, n_res, H, Dh), lambda b: (b, 0, 0, 0, 0)))
    segs = _sample_segments(n_new, nbuf)
    max_cols = max(seg[2] for seg in segs) * H
    new = pl.BlockSpec((1, n_new, G, H, Dh), lambda b: (b, 0, 0, 0, 0))
    return pl.pallas_call(
        functools.partial(_att_sample_kernel, nbuf=tuple(nbuf)),
        grid=(B,),
        in_specs=[new, new, new] + specs,
        out_specs=pl.BlockSpec((1, n_new, H, Dh), lambda b: (b, 0, 0, 0)),
        out_shape=jax.ShapeDtypeStruct((B, n_new, H, Dh), F32),
        scratch_shapes=[pltpu.VMEM((len(segs), n_new * H, max_cols), F32),
                        pltpu.VMEM((len(segs), n_new * H, max_cols), F32)],
        compiler_params=_params("arbitrary"),
        name="att_sample",
    )(q, k, v, *args)


def _rope_tables(pos):
    half = ATT_HEAD_DIM // 2
    inv = ROPE_THETA ** (-jnp.arange(half, dtype=F32) / half)
    ang = pos.astype(F32)[:, None] * inv[None, :]
    cos, sin = jnp.cos(ang), jnp.sin(ang)
    reps = LANES // ATT_HEAD_DIM
    cos_t = jnp.tile(jnp.concatenate([cos, cos], axis=-1), (1, reps))
    sin_t = jnp.tile(jnp.concatenate([-sin, sin], axis=-1), (1, reps))
    return cos_t, sin_t


def kernel(x_prompt, x_sample, state_conv, state_gla, cache_k_g0, cache_v_g0, cache_k_g1, cache_v_g1, cache_k_g2, cache_v_g2, c_prompt, c_sample, w_ada, b_ada, g_pre, g_post, w_conv_in, w_dw, b_dw, g_conv_ln, b_conv_ln, w_conv_out, w_gla_in, w_gla_a1, w_gla_a2, b_gla_a, g_gla_norm, w_gla_out, w_att_in, w_att_out):
    depth = w_ada.shape[0]
    D = x_prompt.shape[-1]
    G = len(ATT_GROUPS)
    xs = [x_prompt, x_sample]
    nseq = [x_prompt.shape[0], x_sample.shape[0]]
    caches_k = (cache_k_g0, cache_k_g1, cache_k_g2)
    caches_v = (cache_v_g0, cache_v_g1, cache_v_g2)

    n_c = nseq[0] + nseq[1]
    pad = (-n_c) % SUBLANES
    c_all = jnp.concatenate([c_prompt, c_sample, jnp.zeros((pad, D), F32)], axis=0)
    mod_all = _modulation(c_all, w_ada, b_ada)
    mods = [[mod_all[l, :nseq[0]].reshape(nseq[0], 1, 3 * D),
             mod_all[l, nseq[0]:n_c].reshape(nseq[1], 1, 3 * D)] for l in range(depth)]

    pos = [jnp.arange(x_prompt.shape[1]), PAST_LEN + jnp.arange(x_sample.shape[1])]
    rope = [_rope_tables(p) for p in pos]

    conv_new, gla_new = ([], []), ([], [])
    k_new = ([[] for _ in range(G)], [[] for _ in range(G)])
    v_new = ([[] for _ in range(G)], [[] for _ in range(G)])
    no_extra = lambda nb, tl: []

    for l in range(depth):
        kind, j = l % N_MIXERS, l // N_MIXERS
        for grp in range(2):
            x, mod = xs[grp], mods[l][grp]
            B, L, _ = x.shape
            if kind == 0:
                C = w_dw.shape[-1]
                u, sz = _pre_call(_pre_conv_kernel, "pre_conv", x, mod, g_pre[l],
                                  [w_conv_in[j].astype(BF16)], no_extra, [], [C, C])
                buf = jnp.zeros((B, CONV_SIZE - 1, C), F32) if grp == 0 else state_conv[j]
                y, st = _conv_mixer(u, sz, buf, w_dw[j], b_dw[j], g_conv_ln[j], b_conv_ln[j])
                conv_new[grp].append(st)
                xs[grp] = _post_call(_post_kernel, "post_conv", [y], w_conv_out[j].astype(BF16), g_post[l], mod, x)
            elif kind == 1:
                qk = w_gla_a2.shape[-1]
                vw = w_gla_out.shape[1]
                wa1 = jnp.zeros((D, LANES), F32).at[:, :GLA_RANK].set(w_gla_a1[j]).astype(BF16)
                wa2 = jnp.zeros((LANES, qk), F32).at[:GLA_RANK].set(w_gla_a2[j]).astype(BF16)
                q, k, v, sr, la = _pre_call(
                    _pre_gla_kernel, "pre_gla", x, mod, g_pre[l],
                    [w_gla_in[j].astype(BF16), wa1, wa2, b_gla_a[j].reshape(1, qk)], no_extra, [],
                    [qk, qk, vw, vw, qk])
                dk, dv = qk // GLA_HEADS, vw // GLA_HEADS
                s0 = jnp.zeros((B, GLA_HEADS, dk, dv), F32) if grp == 0 else state_gla[j]
                y, st = _gla_mixer(q, k, v, sr, la, s0, g_gla_norm[j])
                gla_new[grp].append(st)
                xs[grp] = _post_call(_post_kernel, "post_gla", [y], w_gla_out[j].astype(BF16), g_post[l], mod, x)
            else:
                gw = G * ATT_WIDTH
                cos_t, sin_t = rope[grp]
                rope_specs = lambda nb, tl: [pl.BlockSpec((tl, LANES), lambda b, i: (i, 0))] * 2
                q, k, v, sz = _pre_call(_pre_att_kernel, "pre_att", x, mod, g_pre[l],
                                        [w_att_in[j].astype(BF16)], rope_specs, [cos_t, sin_t],
                                        [gw, gw, gw, ATT_WIDTH])
                heads = (ATT_HEADS, ATT_HEAD_DIM)
                if grp == 0:
                    outs = [_att_prompt_group(q, k, v, g) for g in range(G)]
                    ys = [o for o, _ in outs] + [s for _, s in outs] + [sz]
                    xs[grp] = _post_call(_post_att_kernel, "post_att_prompt", ys, w_att_out[j].astype(BF16),
                                         g_post[l], mod, x)
                    for g, (W, _) in enumerate(ATT_GROUPS):
                        keep = min(W, L)
                        gs = slice(g * ATT_WIDTH, (g + 1) * ATT_WIDTH)
                        k_new[grp][g].append(k[:, L - keep:, gs].reshape(B, keep, *heads))
                        v_new[grp][g].append(v[:, L - keep:, gs].reshape(B, keep, *heads))
                else:
                    ck = [c[j] for c in caches_k]
                    cv = [c[j] for c in caches_v]
                    q5, k5, v5 = (t.reshape(B, L, G, *heads) for t in (q, k, v))
                    y = _att_sample(q5, k5, v5, ck, cv).reshape(B, L, ATT_WIDTH)
                    xs[grp] = _post_call(_post_gate_kernel, "post_att_sample", [y, sz],
                                         w_att_out[j].astype(BF16), g_post[l], mod, x)
                    for g in range(G):
                        k_new[grp][g].append(k5[:, :, g])
                        v_new[grp][g].append(v5[:, :, g])

    outs = [xs[0], xs[1], jnp.stack(conv_new[0]), jnp.stack(conv_new[1]),
            jnp.stack(gla_new[0]), jnp.stack(gla_new[1])]
    for grp in range(2):
        for g in range(G):
            outs += [jnp.stack(k_new[grp][g]), jnp.stack(v_new[grp][g])]
    return tuple(outs)
```

```python
import functools

import jax
import jax.numpy as jnp
from jax import lax
from jax.experimental import pallas as pl
from jax.experimental.pallas import tpu as pltpu

F32 = jnp.float32
BF16 = jnp.bfloat16
HIGHEST = lax.Precision.HIGHEST

PAST_LEN = 2048
N_MIXERS = 3
CONV_SIZE = 31
GLA_HEADS = 4
GLA_RANK = 16
GLA_TAU = 16.0
ATT_GROUPS = ((128, 1), (512, 4), (2048, 16))
ATT_HEADS = 8
ATT_HEAD_DIM = 64
ATT_WIDTH = ATT_HEADS * ATT_HEAD_DIM
ROPE_THETA = 10000.0
EPS = 1e-6
NEG_INF = -1e30

LANES = 128
SUBLANES = 8
VMEM_LIMIT_BYTES = 56 * 1024 * 1024

PROJ_ROWS = 512
CONV_ROWS = 256
CONV_CHUNK = 32
CONV_HALO = 32
GLA_ROWS = 256
GLA_CHUNK = 64
GLA_SUB = 8
ATT_BLOCK = 128
ATT_PROJ_ROWS = 256


def _params(*sem):
    return pltpu.CompilerParams(dimension_semantics=sem, vmem_limit_bytes=VMEM_LIMIT_BYTES)


def _log2(n):
    assert n > 0 and n & (n - 1) == 0, n
    return n.bit_length() - 1


def _token_tiling(B, L, rows=PROJ_ROWS):
    if L >= rows:
        return 1, rows
    nb = max(1, rows // L)
    while B % nb:
        nb //= 2
    return nb, L


def _mod_kernel(c_ref, w_ref, b_ref, o_ref):
    o_ref[0] = jnp.dot(c_ref[...], w_ref[0], precision=HIGHEST, preferred_element_type=F32) + b_ref[0]


def _modulation(c_all, w_ada, b_ada):
    depth, D, N = w_ada.shape
    R = c_all.shape[0]
    tn = 1024
    return pl.pallas_call(
        _mod_kernel,
        grid=(depth, N // tn),
        in_specs=[pl.BlockSpec((R, D), lambda l, j: (0, 0)),
                  pl.BlockSpec((1, D, tn), lambda l, j: (l, 0, j)),
                  pl.BlockSpec((1, 1, tn), lambda l, j: (l, 0, j))],
        out_specs=pl.BlockSpec((1, R, tn), lambda l, j: (l, 0, j)),
        out_shape=jax.ShapeDtypeStruct((depth, R, N), F32),
        compiler_params=_params("arbitrary", "arbitrary"),
        name="adaln_mod",
    )(c_all, w_ada, b_ada.reshape(depth, 1, N))


def _modulated_norm(x_ref, mod_ref, g_ref):
    x = x_ref[...]
    nb, tl, D = x.shape
    ms = jnp.mean(x * x, axis=-1, keepdims=True)
    y = x * lax.rsqrt(ms + EPS) * g_ref[...]
    shift = mod_ref[:, :, 0:D]
    scale = mod_ref[:, :, D:2 * D]
    h = y * (1.0 + scale) + shift
    return h.reshape(nb * tl, D).astype(BF16)


def _pre_conv_kernel(x_ref, mod_ref, g_ref, w_ref, u_ref, sz_ref):
    h = _modulated_norm(x_ref, mod_ref, g_ref)
    res = jnp.dot(h, w_ref[...], preferred_element_type=F32)
    C = u_ref.shape[-1]
    u = res[:, :C] * jax.nn.sigmoid(res[:, C:2 * C])
    z = res[:, 2 * C:]
    u_ref[...] = u.reshape(u_ref.shape)
    sz_ref[...] = (z * jax.nn.sigmoid(z)).reshape(sz_ref.shape)


def _pre_gla_kernel(x_ref, mod_ref, g_ref, w_ref, wa1_ref, wa2_ref, ba_ref,
                    q_ref, k_ref, v_ref, sr_ref, la_ref):
    h = _modulated_norm(x_ref, mod_ref, g_ref)
    res = jnp.dot(h, w_ref[...], preferred_element_type=F32)
    qk = q_ref.shape[-1]
    vw = v_ref.shape[-1]
    dk = qk // GLA_HEADS
    q_ref[...] = (res[:, :qk] * (dk ** -0.5)).reshape(q_ref.shape)
    k_ref[...] = res[:, qk:2 * qk].reshape(k_ref.shape)
    v_ref[...] = res[:, 2 * qk:2 * qk + vw].reshape(v_ref.shape)
    r = res[:, 2 * qk + vw:]
    sr_ref[...] = (r * jax.nn.sigmoid(r)).reshape(sr_ref.shape)
    low = jnp.dot(h, wa1_ref[...], preferred_element_type=F32)
    zg = jnp.dot(low.astype(BF16), wa2_ref[...], preferred_element_type=F32) + ba_ref[...]
    log_sig = jnp.minimum(zg, 0.0) - jnp.log(1.0 + jnp.exp(-jnp.abs(zg)))
    la_ref[...] = (log_sig * (1.0 / GLA_TAU)).reshape(la_ref.shape)


def _swap_halves(x):
    half = ATT_HEAD_DIM // 2
    lane = lax.broadcasted_iota(jnp.int32, x.shape, 1)
    lower = (lane % ATT_HEAD_DIM) < half
    return jnp.where(lower, pltpu.roll(x, LANES - half, 1), pltpu.roll(x, half, 1))


def _rope_cols(res, col0, width, cos, sin, mult):
    pieces = []
    for c in range(width // LANES):
        xc = res[:, col0 + c * LANES: col0 + (c + 1) * LANES]
        val = xc * cos + _swap_halves(xc) * sin
        pieces.append(val * mult if mult != 1.0 else val)
    return pieces


def _pre_att_kernel(x_ref, mod_ref, g_ref, w_ref, cos_ref, sin_ref, q_ref, k_ref, v_ref, sz_ref):
    h = _modulated_norm(x_ref, mod_ref, g_ref)
    res = jnp.dot(h, w_ref[...], preferred_element_type=F32)
    nb, tl, gw = q_ref.shape
    cos = jnp.broadcast_to(cos_ref[...], (nb, tl, LANES)).reshape(nb * tl, LANES)
    sin = jnp.broadcast_to(sin_ref[...], (nb, tl, LANES)).reshape(nb * tl, LANES)
    for c, val in enumerate(_rope_cols(res, 0, gw, cos, sin, ATT_HEAD_DIM ** -0.5)):
        q_ref[:, :, c * LANES:(c + 1) * LANES] = val.reshape(nb, tl, LANES)
    for c, val in enumerate(_rope_cols(res, gw, gw, cos, sin, 1.0)):
        k_ref[:, :, c * LANES:(c + 1) * LANES] = val.reshape(nb, tl, LANES)
    v_ref[...] = res[:, 2 * gw:3 * gw].reshape(v_ref.shape)
    z = res[:, 3 * gw:]
    sz_ref[...] = (z * jax.nn.sigmoid(z)).reshape(sz_ref.shape)


def _pre_call(kernel, name, x, mod, g_pre, weights, extra_specs, extras, out_widths):
    B, L, D = x.shape
    nb, tl = _token_tiling(B, L, PROJ_ROWS if sum(out_widths) <= 3 * D else PROJ_ROWS // 2)
    tok = lambda b, i: (b, i, 0)
    in_specs = [pl.BlockSpec((nb, tl, D), tok),
                pl.BlockSpec((nb, 1, mod.shape[-1]), lambda b, i: (b, 0, 0)),
                pl.BlockSpec((1, D), lambda b, i: (0, 0))]
    in_specs += [pl.BlockSpec(w.shape, lambda b, i: (0, 0)) for w in weights]
    in_specs += extra_specs(nb, tl)
    return pl.pallas_call(
        kernel,
        grid=(B // nb, L // tl),
        in_specs=in_specs,
        out_specs=[pl.BlockSpec((nb, tl, n), tok) for n in out_widths],
        out_shape=[jax.ShapeDtypeStruct((B, L, n), F32) for n in out_widths],
        compiler_params=_params("arbitrary", "arbitrary"),
        name=name,
    )(x, mod, g_pre.reshape(1, D), *weights, *extras)


def _class_order_source(idx, per, d):
    return (idx & (per - 1)) * d + (idx >> _log2(per))


def _pre_att_prompt_kernel(x_ref, mod_ref, g_ref, w0_ref, w1_ref, w2_ref, wz_ref,
                           c0_ref, s0_ref, c1_ref, s1_ref, c2_ref, s2_ref,
                           q0_ref, k0_ref, v0_ref, q1_ref, k1_ref, v1_ref, q2_ref, k2_ref, v2_ref, sz_ref,
                           kt0_ref, vt0_ref, kt1_ref, vt1_ref, kt2_ref, vt2_ref, *, first_tail):
    h = _modulated_norm(x_ref, mod_ref, g_ref)
    tm = h.shape[0]
    width = sz_ref.shape[-1]
    z = jnp.dot(h, wz_ref[...], preferred_element_type=F32)
    sz_ref[0] = z * jax.nn.sigmoid(z)
    step = pl.program_id(1)
    groups = ((w0_ref, c0_ref, s0_ref, q0_ref, k0_ref, v0_ref, kt0_ref, vt0_ref),
              (w1_ref, c1_ref, s1_ref, q1_ref, k1_ref, v1_ref, kt1_ref, vt1_ref),
              (w2_ref, c2_ref, s2_ref, q2_ref, k2_ref, v2_ref, kt2_ref, vt2_ref))
    for g, (w_ref, cos_ref, sin_ref, q_ref, k_ref, v_ref, kt_ref, vt_ref) in enumerate(groups):
        d = ATT_GROUPS[g][1]
        per = tm // d
        if d > 1:
            dst = lax.broadcasted_iota(jnp.int32, (tm, tm), 0)
            src = lax.broadcasted_iota(jnp.int32, (tm, tm), 1)
            perm = jnp.where(src == _class_order_source(dst, per, d), 1.0, 0.0).astype(BF16)
            hg = jnp.dot(perm, h, preferred_element_type=F32).astype(BF16)
        else:
            hg = h
        res = jnp.dot(hg, w_ref[...], preferred_element_type=F32)
        cos, sin = cos_ref[...], sin_ref[...]
        qf = jnp.concatenate(_rope_cols(res, 0, width, cos, sin, ATT_HEAD_DIM ** -0.5), axis=1)
        kf = jnp.concatenate(_rope_cols(res, width, width, cos, sin, 1.0), axis=1)
        vf = res[:, 2 * width:]
        qb, kb, vb = qf.astype(BF16), kf.astype(BF16), vf.astype(BF16)
        for r in range(d):
            rs = slice(r * per, (r + 1) * per)
            cs = slice(r * width, (r + 1) * width)
            q_ref[0, :, cs] = qb[rs]
            k_ref[0, :, cs] = kb[rs]
            v_ref[0, :, cs] = vb[rs]

        @pl.when(step >= first_tail[g])
        def _(kf=kf, vf=vf, kt_ref=kt_ref, vt_ref=vt_ref, d=d, per=per):
            for r in range(d):
                rs = slice(r * per, (r + 1) * per)
                cs = slice(r * width, (r + 1) * width)
                kt_ref[0, :, cs] = kf[rs]
                vt_ref[0, :, cs] = vf[rs]


def _pre_att_prompt(x, mod, g_pre, w_att_in, pos):
    B, L, D = x.shape
    G = len(ATT_GROUPS)
    width = ATT_WIDTH
    gw = G * width
    tm = ATT_PROJ_ROWS
    n_steps = L // tm
    w = w_att_in.astype(BF16)
    weights, tables, first_tail = [], [], []
    out_specs, out_shapes = [], []
    tail_specs, tail_shapes = [], []
    for g, (W, d) in enumerate(ATT_GROUPS):
        gs = slice(g * width, (g + 1) * width)
        weights.append(jnp.concatenate([w[:, gs], w[:, gw:][:, gs], w[:, 2 * gw:][:, gs]], axis=1))
        per = tm // d
        assert tm % d == 0 and per % 16 == 0 and L % tm == 0
        a = jnp.arange(tm)
        order = (jnp.arange(n_steps)[:, None] * tm + _class_order_source(a, per, d)[None, :]).reshape(-1)
        tables += list(_rope_tables(pos[order]))
        for _ in range(3):
            out_specs.append(pl.BlockSpec((1, per, d * width), lambda b, i: (b, i, 0)))
            out_shapes.append(jax.ShapeDtypeStruct((B, L // d, d * width), BF16))
        tail_rows = max(min(W, L) // d, per)
        n_tail = tail_rows // per
        first_tail.append(n_steps - n_tail)
        for _ in range(2):
            tail_specs.append(pl.BlockSpec(
                (1, per, d * width), lambda b, i, first=n_steps - n_tail: (b, jnp.maximum(i - first, 0), 0)))
            tail_shapes.append(jax.ShapeDtypeStruct((B, tail_rows, d * width), F32))
    weights.append(w[:, 3 * gw:])
    const = lambda b, i: (0, 0)
    in_specs = [pl.BlockSpec((1, tm, D), lambda b, i: (b, i, 0)),
                pl.BlockSpec((1, 1, mod.shape[-1]), lambda b, i: (b, 0, 0)),
                pl.BlockSpec((1, D), const)]
    in_specs += [pl.BlockSpec(wt.shape, const) for wt in weights]
    in_specs += [pl.BlockSpec((tm, LANES), lambda b, i: (i, 0)) for _ in tables]
    sz_spec = pl.BlockSpec((1, tm, width), lambda b, i: (b, i, 0))
    outs = pl.pallas_call(
        functools.partial(_pre_att_prompt_kernel, first_tail=tuple(first_tail)),
        grid=(B, n_steps),
        in_specs=in_specs,
        out_specs=out_specs + [sz_spec] + tail_specs,
        out_shape=out_shapes + [jax.ShapeDtypeStruct((B, L, width), F32)] + tail_shapes,
        compiler_params=_params("arbitrary", "arbitrary"),
        name="pre_att_prompt",
    )(x, mod, g_pre.reshape(1, D), *weights, *tables)
    qkv = [tuple(outs[3 * g:3 * g + 3]) for g in range(G)]
    sz = outs[3 * G]
    tails = [tuple(outs[3 * G + 1 + 2 * g: 3 * G + 3 + 2 * g]) for g in range(G)]
    return qkv, sz, tails


def _gated_residual(y_bf16, w_ref, g_ref, mod_ref, x_ref, o_ref):
    nb, tl, D = x_ref.shape
    o = jnp.dot(y_bf16, w_ref[...], preferred_element_type=F32)
    ms = jnp.mean(o * o, axis=-1, keepdims=True)
    o = (o * lax.rsqrt(ms + EPS) * g_ref[...]).reshape(nb, tl, D)
    gate = mod_ref[:, :, 2 * D:3 * D]
    o_ref[...] = x_ref[...] + gate * o


def _post_kernel(y_ref, w_ref, g_ref, mod_ref, x_ref, o_ref):
    nb, tl, K = y_ref.shape
    _gated_residual(y_ref[...].reshape(nb * tl, K).astype(BF16), w_ref, g_ref, mod_ref, x_ref, o_ref)


def _post_gate_kernel(y_ref, sz_ref, w_ref, g_ref, mod_ref, x_ref, o_ref):
    nb, tl, K = y_ref.shape
    y = y_ref[...] * sz_ref[...]
    _gated_residual(y.reshape(nb * tl, K).astype(BF16), w_ref, g_ref, mod_ref, x_ref, o_ref)


def _post_call(kernel, name, ys, w_out, g_post, mod, x):
    B, L, D = x.shape
    K = w_out.shape[0]
    nb, tl = _token_tiling(B, L)
    tok = lambda b, i: (b, i, 0)
    in_specs = [pl.BlockSpec((nb, tl, K), tok) for _ in ys]
    in_specs += [pl.BlockSpec((K, D), lambda b, i: (0, 0)),
                 pl.BlockSpec((1, D), lambda b, i: (0, 0)),
                 pl.BlockSpec((nb, 1, mod.shape[-1]), lambda b, i: (b, 0, 0)),
                 pl.BlockSpec((nb, tl, D), tok)]
    return pl.pallas_call(
        kernel,
        grid=(B // nb, L // tl),
        in_specs=in_specs,
        out_specs=pl.BlockSpec((nb, tl, D), tok),
        out_shape=jax.ShapeDtypeStruct((B, L, D), F32),
        compiler_params=_params("arbitrary", "arbitrary"),
        name=name,
    )(*ys, w_out, g_post.reshape(1, D), mod, x)


def _class_rows(ref, d):
    width = ref.shape[-1] // d
    return jnp.concatenate([ref[0, :, r * width:(r + 1) * width] for r in range(d)], axis=0)


def _post_att_kernel(o0_ref, l0_ref, o1_ref, l1_ref, o2_ref, l2_ref, sz_ref, w_ref, g_ref, mod_ref, x_ref, o_ref):
    _, tm, K = sz_ref.shape
    outs, lses = [], []
    for g, (o_g_ref, l_g_ref) in enumerate(((o0_ref, l0_ref), (o1_ref, l1_ref), (o2_ref, l2_ref))):
        d = ATT_GROUPS[g][1]
        if d == 1:
            outs.append(o_g_ref[0].astype(F32))
            lses.append(l_g_ref[0])
            continue
        per = tm // d
        dst = lax.broadcasted_iota(jnp.int32, (tm, tm), 0)
        src = lax.broadcasted_iota(jnp.int32, (tm, tm), 1)
        back = jnp.where(dst == _class_order_source(src, per, d), 1.0, 0.0).astype(BF16)
        outs.append(jnp.dot(back, _class_rows(o_g_ref, d), preferred_element_type=F32))
        lse = _class_rows(l_g_ref, d)
        hi = lse.astype(BF16)
        rest = lse - hi.astype(F32)
        mid = rest.astype(BF16)
        lo = (rest - mid.astype(F32)).astype(BF16)
        lses.append(jnp.dot(back, hi, preferred_element_type=F32)
                    + jnp.dot(back, mid, preferred_element_type=F32)
                    + jnp.dot(back, lo, preferred_element_type=F32))
    l0, l1, l2 = lses
    m = jnp.maximum(jnp.maximum(l0, l1), l2)
    e0, e1, e2 = jnp.exp(l0 - m), jnp.exp(l1 - m), jnp.exp(l2 - m)
    att = (e0 * outs[0] + e1 * outs[1] + e2 * outs[2]) / (e0 + e1 + e2)
    y = att * sz_ref[0]
    _gated_residual(y.astype(BF16), w_ref, g_ref, mod_ref, x_ref, o_ref)


def _post_att_prompt(att_outs, sz, w_out, g_post, mod, x):
    B, L, D = x.shape
    K = w_out.shape[0]
    tm = ATT_PROJ_ROWS
    tok = lambda b, i: (b, i, 0)
    in_specs, args = [], []
    for g, (W, d) in enumerate(ATT_GROUPS):
        assert tm % (16 * d) == 0
        for t in att_outs[g]:
            in_specs.append(pl.BlockSpec((1, tm // d, d * K), tok))
            args.append(t)
    in_specs += [pl.BlockSpec((1, tm, K), tok),
                 pl.BlockSpec((K, D), lambda b, i: (0, 0)),
                 pl.BlockSpec((1, D), lambda b, i: (0, 0)),
                 pl.BlockSpec((1, 1, mod.shape[-1]), lambda b, i: (b, 0, 0)),
                 pl.BlockSpec((1, tm, D), tok)]
    return pl.pallas_call(
        _post_att_kernel,
        grid=(B, L // tm),
        in_specs=in_specs,
        out_specs=pl.BlockSpec((1, tm, D), tok),
        out_shape=jax.ShapeDtypeStruct((B, L, D), F32),
        compiler_params=_params("arbitrary", "arbitrary"),
        name="post_att_prompt",
    )(*args, sz, w_out, g_post.reshape(1, D), mod, x)


def _conv_kernel(u_ref, sz_ref, buf_ref, w_ref, b_ref, g_ref, bl_ref, y_ref, st_ref, ext_ref, sh_ref):
    nb, tl, C = u_ref.shape
    keep = CONV_SIZE - 1
    lead = CONV_HALO - keep
    rows_sh = sh_ref.shape[1]

    @pl.when(pl.program_id(1) == 0)
    def _():
        ext_ref[:, lead:CONV_HALO, :] = buf_ref[...]

    ext_ref[:, CONV_HALO:CONV_HALO + tl, :] = u_ref[...]
    chunk = min(tl, CONV_CHUNK)
    bias = b_ref[...]
    gamma = g_ref[...]
    beta = bl_ref[...]
    for n in range(nb):
        for s in range(1, SUBLANES):
            sh_ref[s - 1] = ext_ref[n, s:s + rows_sh, :]

        def rows_block(r0):
            acc = jnp.broadcast_to(bias, (chunk, C))
            for j in range(CONV_SIZE):
                a, s = divmod(lead + j, SUBLANES)
                start = pl.multiple_of(r0 + a * SUBLANES, SUBLANES)
                if s == 0:
                    win = ext_ref[n, pl.ds(start, chunk), :]
                else:
                    win = sh_ref[s - 1, pl.ds(start, chunk), :]
                acc = acc + w_ref[j:j + 1, :] * win
            mu = jnp.mean(acc, axis=-1, keepdims=True)
            xc = acc - mu
            var = jnp.mean(xc * xc, axis=-1, keepdims=True)
            ln = xc * lax.rsqrt(var + EPS) * gamma + beta
            out_rows = pl.ds(pl.multiple_of(r0, SUBLANES), chunk)
            y_ref[n, out_rows, :] = ln * jax.nn.sigmoid(ln) * sz_ref[n, out_rows, :]

        if tl == chunk:
            rows_block(0)
        else:
            lax.fori_loop(0, tl // chunk, lambda ci, c: (rows_block(ci * chunk), c)[1], 0)
    tail = ext_ref[:, lead + tl: CONV_HALO + tl, :]
    st_ref[...] = tail
    ext_ref[:, lead:CONV_HALO, :] = tail


def _conv_mixer(u, sz, buf, w_dw, b_dw, g_ln, b_ln):
    B, L, C = u.shape
    keep = CONV_SIZE - 1
    if L >= CONV_ROWS:
        nb, tl = 1, CONV_ROWS
    else:
        nb, tl = SUBLANES, L
    rows_sh = tl + CONV_HALO - SUBLANES
    tok = lambda b, i: (b, i, 0)
    vec = lambda b, i: (0, 0)
    return pl.pallas_call(
        _conv_kernel,
        grid=(B // nb, L // tl),
        in_specs=[pl.BlockSpec((nb, tl, C), tok),
                  pl.BlockSpec((nb, tl, C), tok),
                  pl.BlockSpec((nb, keep, C), lambda b, i: (b, 0, 0)),
                  pl.BlockSpec((CONV_SIZE, C), vec),
                  pl.BlockSpec((1, C), vec), pl.BlockSpec((1, C), vec), pl.BlockSpec((1, C), vec)],
        out_specs=[pl.BlockSpec((nb, tl, C), tok),
                   pl.BlockSpec((nb, keep, C), lambda b, i: (b, 0, 0))],
        out_shape=[jax.ShapeDtypeStruct((B, L, C), F32),
                   jax.ShapeDtypeStruct((B, keep, C), F32)],
        scratch_shapes=[pltpu.VMEM((nb, CONV_HALO + tl, C), F32),
                        pltpu.VMEM((SUBLANES - 1, rows_sh, C), F32)],
        compiler_params=_params("arbitrary", "arbitrary"),
        name="conv_mixer",
    )(u, sz, buf, w_dw, b_dw.reshape(1, C), g_ln.reshape(1, C), b_ln.reshape(1, C))


def _gla_kernel(q_ref, k_ref, v_ref, sr_ref, la_ref, s0_ref, g_ref, y_ref, sout_ref, S_ref, *, chunk):
    _, tl, qk = q_ref.shape
    vw = v_ref.shape[-1]
    H = GLA_HEADS
    dk, dv = qk // H, vw // H
    sub = min(chunk, GLA_SUB)
    nblk = chunk // sub

    @pl.when(pl.program_id(1) == 0)
    def _():
        S_ref[...] = s0_ref[0]

    row = lax.broadcasted_iota(jnp.int32, (chunk, chunk), 0)
    col = lax.broadcasted_iota(jnp.int32, (chunk, chunk), 1)
    tril = jnp.where(row >= col, 1.0, 0.0).astype(F32)
    row1 = lax.broadcasted_iota(jnp.int32, (chunk, 1), 0)
    sub_row = lax.broadcasted_iota(jnp.int32, (nblk, sub, 1), 1)
    ones = jnp.ones((chunk, dv), F32)
    gamma = g_ref[...]
    tn = (((0,), (0,)), ((), ()))
    nt = (((1,), (1,)), ((), ()))
    halves = []
    m = chunk // 2
    while m >= sub:
        halves.append(m)
        m //= 2

    for c0 in range(0, tl, chunk):
        rows = slice(c0, c0 + chunk)
        la = la_ref[0, rows, :]
        bc = jnp.dot(tril, la, precision=HIGHEST, preferred_element_type=F32)
        btot = bc[chunk - 1:chunk, :]
        qq = q_ref[0, rows, :]
        kk = k_ref[0, rows, :]
        q0 = qq * jnp.exp(bc)
        kh = kk * jnp.exp(btot - bc)

        level_ops = []
        for m in halves:
            nb2 = chunk // (2 * m)
            bc3 = bc.reshape(nb2, 2 * m, qk)
            ref = jnp.broadcast_to(bc3[:, m - 1:m, :], (nb2, 2 * m, qk)).reshape(chunk, qk)
            e = jnp.exp(-jnp.abs(bc - ref))
            upper = ((row1 >> _log2(m)) & 1) == 1
            level_ops.append((jnp.where(upper, qq * e, 0.0), jnp.where(upper, 0.0, kk * e)))

        bc3 = bc.reshape(nblk, sub, qk)
        q3 = qq.reshape(nblk, sub, qk)
        k3 = kk.reshape(nblk, sub, qk)
        diag_terms = []
        for jp in range(sub):
            ref = jnp.broadcast_to(bc3[:, jp:jp + 1, :], (nblk, sub, qk))
            kj = jnp.broadcast_to(k3[:, jp:jp + 1, :], (nblk, sub, qk))
            wgt = jnp.where(sub_row >= jp, jnp.exp(jnp.minimum(bc3 - ref, 0.0)), 0.0)
            diag_terms.append((q3 * kj * wgt).reshape(chunk, qk))

        for h in range(H):
            ks = slice(h * dk, (h + 1) * dk)
            vs = slice(h * dv, (h + 1) * dv)
            vh = v_ref[0, rows, vs].astype(BF16)
            att = jnp.zeros((chunk, chunk), F32)
            for m, (qm, km) in zip(halves, level_ops):
                part = lax.dot_general(qm[:, ks].astype(BF16), km[:, ks].astype(BF16), nt,
                                       preferred_element_type=F32)
                if 2 * m < chunk:
                    same = (row >> _log2(2 * m)) == (col >> _log2(2 * m))
                    part = jnp.where(same, part, 0.0)
                att = att + part
            blk0 = (row >> _log2(sub)) << _log2(sub)
            for jp in range(sub):
                rs = jnp.sum(diag_terms[jp][:, ks], axis=-1, keepdims=True)
                att = att + jnp.where(col == blk0 + jp, rs, 0.0)
            S = S_ref[h]
            o = (jnp.dot(att.astype(BF16), vh, preferred_element_type=F32)
                 + jnp.dot(q0[:, ks].astype(BF16), S.astype(BF16), preferred_element_type=F32))
            decay = jnp.exp(lax.dot_general(la[:, ks], ones, tn, precision=HIGHEST,
                                            preferred_element_type=F32))
            S_ref[h] = decay * S + lax.dot_general(kh[:, ks].astype(BF16), vh, tn, preferred_element_type=F32)
            ms = jnp.mean(o * o, axis=-1, keepdims=True)
            y_ref[0, rows, vs] = o * lax.rsqrt(ms + EPS) * gamma * sr_ref[0, rows, vs]

    sout_ref[0] = S_ref[...]


def _gla_mixer(q, k, v, sr, la, s0, g_norm):
    B, L, qk = q.shape
    vw = v.shape[-1]
    H = GLA_HEADS
    dk, dv = qk // H, vw // H
    tl = min(L, GLA_ROWS)
    chunk = min(L, GLA_CHUNK)
    tok = lambda b, i: (b, i, 0)
    st = lambda b, i: (b, 0, 0, 0)
    return pl.pallas_call(
        functools.partial(_gla_kernel, chunk=chunk),
        grid=(B, L // tl),
        in_specs=[pl.BlockSpec((1, tl, qk), tok), pl.BlockSpec((1, tl, qk), tok),
                  pl.BlockSpec((1, tl, vw), tok), pl.BlockSpec((1, tl, vw), tok),
                  pl.BlockSpec((1, tl, qk), tok),
                  pl.BlockSpec((1, H, dk, dv), st),
                  pl.BlockSpec((1, dv), lambda b, i: (0, 0))],
        out_specs=[pl.BlockSpec((1, tl, vw), tok),
                   pl.BlockSpec((1, H, dk, dv), st)],
        out_shape=[jax.ShapeDtypeStruct((B, L, vw), F32),
                   jax.ShapeDtypeStruct((B, H, dk, dv), F32)],
        scratch_shapes=[pltpu.VMEM((H, dk, dv), F32)],
        compiler_params=_params("arbitrary", "arbitrary"),
        name="gla_mixer",
    )(q, k, v, sr, la, s0, g_norm.reshape(1, dv))


def _att_prompt_kernel(q_ref, kp_ref, kc_ref, vp_ref, vc_ref, o_ref, l_ref):
    blk = q_ref.shape[1]
    first = pl.program_id(2) == 0
    row = lax.broadcasted_iota(jnp.int32, (blk, 2 * blk), 0)
    col = lax.broadcasted_iota(jnp.int32, (blk, 2 * blk), 1)
    valid = (col >= row) & (col <= row + blk) & jnp.logical_or(col >= blk, jnp.logical_not(first))
    q = q_ref[0]
    kk = jnp.concatenate([kp_ref[0], kc_ref[0]], axis=0)
    vv = jnp.concatenate([vp_ref[0], vc_ref[0]], axis=0)
    nt = (((1,), (1,)), ((), ()))
    for h in range(ATT_HEADS):
        hs = slice(h * ATT_HEAD_DIM, (h + 1) * ATT_HEAD_DIM)
        s = lax.dot_general(q[:, hs], kk[:, hs], nt, preferred_element_type=F32)
        s = jnp.where(valid, s, NEG_INF)
        m = jnp.max(s, axis=-1, keepdims=True)
        p = jnp.exp(s - m)
        den = jnp.sum(p, axis=-1, keepdims=True)
        o = jnp.dot(p.astype(BF16), vv[:, hs], preferred_element_type=F32) / den
        o_ref[0, :, hs] = o.astype(o_ref.dtype)
        l_ref[0, :, hs] = jnp.broadcast_to(m + jnp.log(den), (blk, ATT_HEAD_DIM))


def _att_prompt_group(q, k, v, g):
    W, d = ATT_GROUPS[g]
    B, Ld, dw = q.shape
    width = dw // d
    blk = min(W, Ld * d) // d
    assert blk == ATT_BLOCK and Ld % blk == 0
    cur = lambda b, r, i: (b, i, r)
    prev = lambda b, r, i: (b, jnp.maximum(i - 1, 0), r)
    return pl.pallas_call(
        _att_prompt_kernel,
        grid=(B, d, Ld // blk),
        in_specs=[pl.BlockSpec((1, blk, width), cur),
                  pl.BlockSpec((1, blk, width), prev), pl.BlockSpec((1, blk, width), cur),
                  pl.BlockSpec((1, blk, width), prev), pl.BlockSpec((1, blk, width), cur)],
        out_specs=[pl.BlockSpec((1, blk, width), cur), pl.BlockSpec((1, blk, width), cur)],
        out_shape=[jax.ShapeDtypeStruct((B, Ld, dw), BF16),
                   jax.ShapeDtypeStruct((B, Ld, dw), F32)],
        compiler_params=_params("arbitrary", "arbitrary", "arbitrary"),
        name="att_prompt_g%d" % g,
    )(q, k, k, v, v)


def _att_sample_kernel(q_ref, k_ref, v_ref, ck0_ref, cv0_ref, ck1_ref, cv1_ref, ck2_ref, cv2_ref, y_ref, *, nbuf):
    G, H, n_new, Dh = q_ref.shape[1:]
    caches = ((ck0_ref, cv0_ref), (ck1_ref, cv1_ref), (ck2_ref, cv2_ref))
    nt = (((1,), (1,)), ((), ()))
    biases = []
    for g, (W, d) in enumerate(ATT_GROUPS):
        nb = nbuf[g]
        i = lax.broadcasted_iota(jnp.int32, (n_new, nb), 0)
        c = lax.broadcasted_iota(jnp.int32, (n_new, nb), 1)
        ok = ((c & (d - 1)) == (i & (d - 1))) & (c >= i - (W - nb))
        i2 = lax.broadcasted_iota(jnp.int32, (n_new, n_new), 0)
        j2 = lax.broadcasted_iota(jnp.int32, (n_new, n_new), 1)
        ok2 = (j2 <= i2) & (((i2 - j2) & (d - 1)) == 0)
        biases.append((jnp.where(ok, 0.0, NEG_INF).astype(F32), jnp.where(ok2, 0.0, NEG_INF).astype(F32)))
    for h in range(H):
        scores = []
        for g in range(G):
            qh = q_ref[0, g, h].astype(BF16)
            kt = caches[g][0][0, h].astype(BF16)
            scores.append(jnp.dot(qh, kt, preferred_element_type=F32) + biases[g][0])
            kn = k_ref[0, g, h].astype(BF16)
            scores.append(lax.dot_general(qh, kn, nt, preferred_element_type=F32) + biases[g][1])
        m = functools.reduce(jnp.maximum, [jnp.max(s, axis=-1, keepdims=True) for s in scores])
        acc = jnp.zeros((n_new, Dh), F32)
        den = jnp.zeros((n_new, 1), F32)
        for g in range(G):
            pc = jnp.exp(scores[2 * g] - m).astype(BF16)
            pn = jnp.exp(scores[2 * g + 1] - m).astype(BF16)
            den = den + jnp.sum(pc.astype(F32), axis=-1, keepdims=True) + jnp.sum(pn.astype(F32), axis=-1, keepdims=True)
            vt = caches[g][1][0, h].astype(BF16)
            acc = acc + lax.dot_general(pc, vt, nt, preferred_element_type=F32)
            acc = acc + jnp.dot(pn, v_ref[0, g, h].astype(BF16), preferred_element_type=F32)
        y_ref[0, h] = acc / den


def _att_sample(q, k, v, caches_k, caches_v):
    B, G, H, n_new, Dh = q.shape
    specs, nbuf = [], []
    for g, (W, d) in enumerate(ATT_GROUPS):
        nb = caches_k[g].shape[-1]
        assert d & (d - 1) == 0 and (W - nb) % d == 0
        nbuf.append(nb)
        specs += [pl.BlockSpec((1, H, Dh, nb), lambda b: (b, 0, 0, 0))] * 2
    new = pl.BlockSpec((1, G, H, n_new, Dh), lambda b: (b, 0, 0, 0, 0))
    args = [t for g in range(G) for t in (caches_k[g], caches_v[g])]
    return pl.pallas_call(
        functools.partial(_att_sample_kernel, nbuf=tuple(nbuf)),
        grid=(B,),
        in_specs=[new, new, new] + specs,
        out_specs=pl.BlockSpec((1, H, n_new, Dh), lambda b: (b, 0, 0, 0)),
        out_shape=jax.ShapeDtypeStruct((B, H, n_new, Dh), F32),
        compiler_params=_params("arbitrary"),
        name="att_sample",
    )(q, k, v, *args)


def _rope_tables(pos):
    half = ATT_HEAD_DIM // 2
    inv = ROPE_THETA ** (-jnp.arange(half, dtype=F32) / half)
    ang = pos.astype(F32)[:, None] * inv[None, :]
    cos, sin = jnp.cos(ang), jnp.sin(ang)
    reps = LANES // ATT_HEAD_DIM
    cos_t = jnp.tile(jnp.concatenate([cos, cos], axis=-1), (1, reps))
    sin_t = jnp.tile(jnp.concatenate([-sin, sin], axis=-1), (1, reps))
    return cos_t, sin_t


def kernel(x_prompt, x_sample, state_conv, state_gla, cache_k_g0, cache_v_g0, cache_k_g1, cache_v_g1, cache_k_g2, cache_v_g2, c_prompt, c_sample, w_ada, b_ada, g_pre, g_post, w_conv_in, w_dw, b_dw, g_conv_ln, b_conv_ln, w_conv_out, w_gla_in, w_gla_a1, w_gla_a2, b_gla_a, g_gla_norm, w_gla_out, w_att_in, w_att_out):
    depth = w_ada.shape[0]
    D = x_prompt.shape[-1]
    G = len(ATT_GROUPS)
    xs = [x_prompt, x_sample]
    nseq = [x_prompt.shape[0], x_sample.shape[0]]
    caches_k = (cache_k_g0, cache_k_g1, cache_k_g2)
    caches_v = (cache_v_g0, cache_v_g1, cache_v_g2)

    n_c = nseq[0] + nseq[1]
    pad = (-n_c) % SUBLANES
    c_all = jnp.concatenate([c_prompt, c_sample, jnp.zeros((pad, D), F32)], axis=0)
    mod_all = _modulation(c_all, w_ada, b_ada)
    mods = [[mod_all[l, :nseq[0]].reshape(nseq[0], 1, 3 * D),
             mod_all[l, nseq[0]:n_c].reshape(nseq[1], 1, 3 * D)] for l in range(depth)]

    pos = [jnp.arange(x_prompt.shape[1]), PAST_LEN + jnp.arange(x_sample.shape[1])]
    rope = [_rope_tables(p) for p in pos]

    conv_new, gla_new = ([], []), ([], [])
    k_new = ([[] for _ in range(G)], [[] for _ in range(G)])
    v_new = ([[] for _ in range(G)], [[] for _ in range(G)])
    no_extra = lambda nb, tl: []
    heads = (ATT_HEADS, ATT_HEAD_DIM)

    for l in range(depth):
        kind, j = l % N_MIXERS, l // N_MIXERS
        for grp in range(2):
            x, mod = xs[grp], mods[l][grp]
            B, L, _ = x.shape
            if kind == 0:
                C = w_dw.shape[-1]
                u, sz = _pre_call(_pre_conv_kernel, "pre_conv", x, mod, g_pre[l],
                                  [w_conv_in[j].astype(BF16)], no_extra, [], [C, C])
                buf = jnp.zeros((B, CONV_SIZE - 1, C), F32) if grp == 0 else state_conv[j]
                y, st = _conv_mixer(u, sz, buf, w_dw[j], b_dw[j], g_conv_ln[j], b_conv_ln[j])
                conv_new[grp].append(st)
                xs[grp] = _post_call(_post_kernel, "post_conv", [y], w_conv_out[j].astype(BF16), g_post[l], mod, x)
            elif kind == 1:
                qk = w_gla_a2.shape[-1]
                vw = w_gla_out.shape[1]
                wa1 = jnp.zeros((D, LANES), F32).at[:, :GLA_RANK].set(w_gla_a1[j]).astype(BF16)
                wa2 = jnp.zeros((LANES, qk), F32).at[:GLA_RANK].set(w_gla_a2[j]).astype(BF16)
                q, k, v, sr, la = _pre_call(
                    _pre_gla_kernel, "pre_gla", x, mod, g_pre[l],
                    [w_gla_in[j].astype(BF16), wa1, wa2, b_gla_a[j].reshape(1, qk)], no_extra, [],
                    [qk, qk, vw, vw, qk])
                dk, dv = qk // GLA_HEADS, vw // GLA_HEADS
                s0 = jnp.zeros((B, GLA_HEADS, dk, dv), F32) if grp == 0 else state_gla[j]
                y, st = _gla_mixer(q, k, v, sr, la, s0, g_gla_norm[j])
                gla_new[grp].append(st)
                xs[grp] = _post_call(_post_kernel, "post_gla", [y], w_gla_out[j].astype(BF16), g_post[l], mod, x)
            elif grp == 0:
                qkv, sz, tails = _pre_att_prompt(x, mod, g_pre[l], w_att_in[j], pos[grp])
                att_outs = [_att_prompt_group(*qkv[g], g) for g in range(G)]
                xs[grp] = _post_att_prompt(att_outs, sz, w_att_out[j].astype(BF16), g_post[l], mod, x)
                for g, (W, d) in enumerate(ATT_GROUPS):
                    keep = min(W, L)
                    for store, t in ((k_new, tails[g][0]), (v_new, tails[g][1])):
                        t = t[:, t.shape[1] - keep // d:, :]
                        store[grp][g].append(t.reshape(B, keep, *heads))
            else:
                gw = G * ATT_WIDTH
                cos_t, sin_t = rope[grp]
                rope_specs = lambda nb, tl: [pl.BlockSpec((tl, LANES), lambda b, i: (i, 0))] * 2
                q, k, v, sz = _pre_call(_pre_att_kernel, "pre_att", x, mod, g_pre[l],
                                        [w_att_in[j].astype(BF16)], rope_specs, [cos_t, sin_t],
                                        [gw, gw, gw, ATT_WIDTH])
                ck = [jnp.transpose(c[j], (0, 2, 3, 1)) for c in caches_k]
                cv = [jnp.transpose(c[j], (0, 2, 3, 1)) for c in caches_v]
                q5, k5, v5 = (t.reshape(B, L, G, *heads) for t in (q, k, v))
                qt, kt, vt = (jnp.transpose(t, (0, 2, 3, 1, 4)) for t in (q5, k5, v5))
                y = _att_sample(qt, kt, vt, ck, cv)
                y = jnp.transpose(y, (0, 2, 1, 3)).reshape(B, L, ATT_WIDTH)
                xs[grp] = _post_call(_post_gate_kernel, "post_att_sample", [y, sz],
                                     w_att_out[j].astype(BF16), g_post[l], mod, x)
                for g in range(G):
                    k_new[grp][g].append(k5[:, :, g])
                    v_new[grp][g].append(v5[:, :, g])

    outs = [xs[0], xs[1], jnp.stack(conv_new[0]), jnp.stack(conv_new[1]),
            jnp.stack(gla_new[0]), jnp.stack(gla_new[1])]
    for grp in range(2):
        for g in range(G):
            outs += [jnp.stack(k_new[grp][g]), jnp.stack(v_new[grp][g])]
    return tuple(outs)
```

```python
import functools

import jax
import jax.numpy as jnp
from jax import lax
from jax.experimental import pallas as pl
from jax.experimental.pallas import tpu as pltpu

F32 = jnp.float32
BF16 = jnp.bfloat16
HIGHEST = lax.Precision.HIGHEST

PAST_LEN = 2048
N_MIXERS = 3
CONV_SIZE = 31
GLA_HEADS = 4
GLA_RANK = 16
GLA_TAU = 16.0
ATT_GROUPS = ((128, 1), (512, 4), (2048, 16))
ATT_HEADS = 8
ATT_HEAD_DIM = 64
ATT_WIDTH = ATT_HEADS * ATT_HEAD_DIM
ROPE_THETA = 10000.0
EPS = 1e-6
NEG_INF = -1e30

LANES = 128
SUBLANES = 8
VMEM_LIMIT_BYTES = 56 * 1024 * 1024

PROJ_ROWS = 512
CONV_ROWS = 256
CONV_CHUNK = 32
CONV_BLOCK_ROWS = 32
CONV_HALO = 32
GLA_ROWS = 256
GLA_CHUNK = 64
GLA_SUB = 8
GLA_SEQS = 4
ATT_BLOCK = 128
ATT_PROJ_ROWS = 256


def _params(*sem):
    return pltpu.CompilerParams(dimension_semantics=sem, vmem_limit_bytes=VMEM_LIMIT_BYTES)


def _log2(n):
    assert n > 0 and n & (n - 1) == 0, n
    return n.bit_length() - 1


def _token_tiling(B, L, rows=PROJ_ROWS):
    if L >= rows:
        return 1, rows
    nb = max(1, rows // L)
    while B % nb:
        nb //= 2
    return nb, L


def _mod_kernel(c_ref, w_ref, b_ref, o_ref):
    o_ref[0] = jnp.dot(c_ref[...], w_ref[0], precision=HIGHEST, preferred_element_type=F32) + b_ref[0]


def _modulation(c_all, w_ada, b_ada):
    depth, D, N = w_ada.shape
    R = c_all.shape[0]
    tn = 1024
    return pl.pallas_call(
        _mod_kernel,
        grid=(depth, N // tn),
        in_specs=[pl.BlockSpec((R, D), lambda l, j: (0, 0)),
                  pl.BlockSpec((1, D, tn), lambda l, j: (l, 0, j)),
                  pl.BlockSpec((1, 1, tn), lambda l, j: (l, 0, j))],
        out_specs=pl.BlockSpec((1, R, tn), lambda l, j: (l, 0, j)),
        out_shape=jax.ShapeDtypeStruct((depth, R, N), F32),
        compiler_params=_params("arbitrary", "arbitrary"),
        name="adaln_mod",
    )(c_all, w_ada, b_ada.reshape(depth, 1, N))


def _modulated_norm(x_ref, mod_ref, g_ref):
    x = x_ref[...]
    nb, tl, D = x.shape
    ms = jnp.mean(x * x, axis=-1, keepdims=True)
    y = x * lax.rsqrt(ms + EPS) * g_ref[...]
    shift = mod_ref[:, :, 0:D]
    scale = mod_ref[:, :, D:2 * D]
    h = y * (1.0 + scale) + shift
    return h.reshape(nb * tl, D).astype(BF16)


def _pre_conv_kernel(x_ref, mod_ref, g_ref, w_ref, u_ref, sz_ref):
    h = _modulated_norm(x_ref, mod_ref, g_ref)
    res = jnp.dot(h, w_ref[...], preferred_element_type=F32)
    C = u_ref.shape[-1]
    u = res[:, :C] * jax.nn.sigmoid(res[:, C:2 * C])
    z = res[:, 2 * C:]
    u_ref[...] = u.reshape(u_ref.shape)
    sz_ref[...] = (z * jax.nn.sigmoid(z)).reshape(sz_ref.shape)


def _pre_gla_kernel(x_ref, mod_ref, g_ref, w_ref, wa1_ref, wa2_ref, ba_ref,
                    q_ref, k_ref, v_ref, sr_ref, la_ref):
    h = _modulated_norm(x_ref, mod_ref, g_ref)
    res = jnp.dot(h, w_ref[...], preferred_element_type=F32)
    qk = q_ref.shape[-1]
    vw = v_ref.shape[-1]
    dk = qk // GLA_HEADS
    q_ref[...] = (res[:, :qk] * (dk ** -0.5)).reshape(q_ref.shape)
    k_ref[...] = res[:, qk:2 * qk].reshape(k_ref.shape)
    v_ref[...] = res[:, 2 * qk:2 * qk + vw].reshape(v_ref.shape)
    r = res[:, 2 * qk + vw:]
    sr_ref[...] = (r * jax.nn.sigmoid(r)).reshape(sr_ref.shape)
    low = jnp.dot(h, wa1_ref[...], preferred_element_type=F32)
    zg = jnp.dot(low.astype(BF16), wa2_ref[...], preferred_element_type=F32) + ba_ref[...]
    log_sig = jnp.minimum(zg, 0.0) - jnp.log(1.0 + jnp.exp(-jnp.abs(zg)))
    la_ref[...] = (log_sig * (1.0 / GLA_TAU)).reshape(la_ref.shape)


def _swap_halves(x):
    half = ATT_HEAD_DIM // 2
    lane = lax.broadcasted_iota(jnp.int32, x.shape, 1)
    lower = (lane % ATT_HEAD_DIM) < half
    return jnp.where(lower, pltpu.roll(x, LANES - half, 1), pltpu.roll(x, half, 1))


def _rope_cols(res, col0, width, cos, sin, mult):
    pieces = []
    for c in range(width // LANES):
        xc = res[:, col0 + c * LANES: col0 + (c + 1) * LANES]
        val = xc * cos + _swap_halves(xc) * sin
        pieces.append(val * mult if mult != 1.0 else val)
    return pieces


def _pre_att_kernel(x_ref, mod_ref, g_ref, w_ref, cos_ref, sin_ref, q_ref, k_ref, v_ref, sz_ref):
    h = _modulated_norm(x_ref, mod_ref, g_ref)
    res = jnp.dot(h, w_ref[...], preferred_element_type=F32)
    nb, tl, gw = q_ref.shape
    cos = jnp.broadcast_to(cos_ref[...], (nb, tl, LANES)).reshape(nb * tl, LANES)
    sin = jnp.broadcast_to(sin_ref[...], (nb, tl, LANES)).reshape(nb * tl, LANES)
    for c, val in enumerate(_rope_cols(res, 0, gw, cos, sin, ATT_HEAD_DIM ** -0.5)):
        q_ref[:, :, c * LANES:(c + 1) * LANES] = val.reshape(nb, tl, LANES)
    for c, val in enumerate(_rope_cols(res, gw, gw, cos, sin, 1.0)):
        k_ref[:, :, c * LANES:(c + 1) * LANES] = val.reshape(nb, tl, LANES)
    v_ref[...] = res[:, 2 * gw:3 * gw].reshape(v_ref.shape)
    z = res[:, 3 * gw:]
    sz_ref[...] = (z * jax.nn.sigmoid(z)).reshape(sz_ref.shape)


def _pre_call(kernel, name, x, mod, g_pre, weights, extra_specs, extras, out_widths):
    B, L, D = x.shape
    nb, tl = _token_tiling(B, L, PROJ_ROWS if sum(out_widths) <= 3 * D else PROJ_ROWS // 2)
    tok = lambda b, i: (b, i, 0)
    in_specs = [pl.BlockSpec((nb, tl, D), tok),
                pl.BlockSpec((nb, 1, mod.shape[-1]), lambda b, i: (b, 0, 0)),
                pl.BlockSpec((1, D), lambda b, i: (0, 0))]
    in_specs += [pl.BlockSpec(w.shape, lambda b, i: (0, 0)) for w in weights]
    in_specs += extra_specs(nb, tl)
    return pl.pallas_call(
        kernel,
        grid=(B // nb, L // tl),
        in_specs=in_specs,
        out_specs=[pl.BlockSpec((nb, tl, n), tok) for n in out_widths],
        out_shape=[jax.ShapeDtypeStruct((B, L, n), F32) for n in out_widths],
        compiler_params=_params("arbitrary", "arbitrary"),
        name=name,
    )(x, mod, g_pre.reshape(1, D), *weights, *extras)


def _class_order_source(idx, per, d):
    return (idx & (per - 1)) * d + (idx >> _log2(per))


def _pre_att_prompt_kernel(x_ref, mod_ref, g_ref, w0_ref, w1_ref, w2_ref, wz_ref,
                           c0_ref, s0_ref, c1_ref, s1_ref, c2_ref, s2_ref,
                           q0_ref, k0_ref, v0_ref, q1_ref, k1_ref, v1_ref, q2_ref, k2_ref, v2_ref, sz_ref,
                           kt0_ref, vt0_ref, kt1_ref, vt1_ref, kt2_ref, vt2_ref, *, first_tail):
    h = _modulated_norm(x_ref, mod_ref, g_ref)
    tm = h.shape[0]
    width = sz_ref.shape[-1]
    z = jnp.dot(h, wz_ref[...], preferred_element_type=F32)
    sz_ref[0] = z * jax.nn.sigmoid(z)
    step = pl.program_id(1)
    groups = ((w0_ref, c0_ref, s0_ref, q0_ref, k0_ref, v0_ref, kt0_ref, vt0_ref),
              (w1_ref, c1_ref, s1_ref, q1_ref, k1_ref, v1_ref, kt1_ref, vt1_ref),
              (w2_ref, c2_ref, s2_ref, q2_ref, k2_ref, v2_ref, kt2_ref, vt2_ref))
    for g, (w_ref, cos_ref, sin_ref, q_ref, k_ref, v_ref, kt_ref, vt_ref) in enumerate(groups):
        d = ATT_GROUPS[g][1]
        per = tm // d
        if d > 1:
            dst = lax.broadcasted_iota(jnp.int32, (tm, tm), 0)
            src = lax.broadcasted_iota(jnp.int32, (tm, tm), 1)
            perm = jnp.where(src == _class_order_source(dst, per, d), 1.0, 0.0).astype(BF16)
            hg = jnp.dot(perm, h, preferred_element_type=F32).astype(BF16)
        else:
            hg = h
        res = jnp.dot(hg, w_ref[...], preferred_element_type=F32)
        cos, sin = cos_ref[...], sin_ref[...]
        qf = jnp.concatenate(_rope_cols(res, 0, width, cos, sin, ATT_HEAD_DIM ** -0.5), axis=1)
        kf = jnp.concatenate(_rope_cols(res, width, width, cos, sin, 1.0), axis=1)
        vf = res[:, 2 * width:]
        qb, kb, vb = qf.astype(BF16), kf.astype(BF16), vf.astype(BF16)
        for r in range(d):
            rs = slice(r * per, (r + 1) * per)
            cs = slice(r * width, (r + 1) * width)
            q_ref[0, :, cs] = qb[rs]
            k_ref[0, :, cs] = kb[rs]
            v_ref[0, :, cs] = vb[rs]

        @pl.when(step >= first_tail[g])
        def _(kf=kf, vf=vf, kt_ref=kt_ref, vt_ref=vt_ref, d=d, per=per):
            for r in range(d):
                rs = slice(r * per, (r + 1) * per)
                cs = slice(r * width, (r + 1) * width)
                kt_ref[0, :, cs] = kf[rs]
                vt_ref[0, :, cs] = vf[rs]


def _pre_att_prompt(x, mod, g_pre, w_att_in, pos):
    B, L, D = x.shape
    G = len(ATT_GROUPS)
    width = ATT_WIDTH
    gw = G * width
    tm = ATT_PROJ_ROWS
    n_steps = L // tm
    w = w_att_in.astype(BF16)
    weights, tables, first_tail = [], [], []
    out_specs, out_shapes = [], []
    tail_specs, tail_shapes = [], []
    for g, (W, d) in enumerate(ATT_GROUPS):
        gs = slice(g * width, (g + 1) * width)
        weights.append(jnp.concatenate([w[:, gs], w[:, gw:][:, gs], w[:, 2 * gw:][:, gs]], axis=1))
        per = tm // d
        assert tm % d == 0 and per % 16 == 0 and L % tm == 0
        a = jnp.arange(tm)
        order = (jnp.arange(n_steps)[:, None] * tm + _class_order_source(a, per, d)[None, :]).reshape(-1)
        tables += list(_rope_tables(pos[order]))
        for _ in range(3):
            out_specs.append(pl.BlockSpec((1, per, d * width), lambda b, i: (b, i, 0)))
            out_shapes.append(jax.ShapeDtypeStruct((B, L // d, d * width), BF16))
        tail_rows = max(min(W, L) // d, per)
        n_tail = tail_rows // per
        first_tail.append(n_steps - n_tail)
        for _ in range(2):
            tail_specs.append(pl.BlockSpec(
                (1, per, d * width), lambda b, i, first=n_steps - n_tail: (b, jnp.maximum(i - first, 0), 0)))
            tail_shapes.append(jax.ShapeDtypeStruct((B, tail_rows, d * width), F32))
    weights.append(w[:, 3 * gw:])
    const = lambda b, i: (0, 0)
    in_specs = [pl.BlockSpec((1, tm, D), lambda b, i: (b, i, 0)),
                pl.BlockSpec((1, 1, mod.shape[-1]), lambda b, i: (b, 0, 0)),
                pl.BlockSpec((1, D), const)]
    in_specs += [pl.BlockSpec(wt.shape, const) for wt in weights]
    in_specs += [pl.BlockSpec((tm, LANES), lambda b, i: (i, 0)) for _ in tables]
    sz_spec = pl.BlockSpec((1, tm, width), lambda b, i: (b, i, 0))
    outs = pl.pallas_call(
        functools.partial(_pre_att_prompt_kernel, first_tail=tuple(first_tail)),
        grid=(B, n_steps),
        in_specs=in_specs,
        out_specs=out_specs + [sz_spec] + tail_specs,
        out_shape=out_shapes + [jax.ShapeDtypeStruct((B, L, width), F32)] + tail_shapes,
        compiler_params=_params("arbitrary", "arbitrary"),
        name="pre_att_prompt",
    )(x, mod, g_pre.reshape(1, D), *weights, *tables)
    qkv = [tuple(outs[3 * g:3 * g + 3]) for g in range(G)]
    sz = outs[3 * G]
    tails = [tuple(outs[3 * G + 1 + 2 * g: 3 * G + 3 + 2 * g]) for g in range(G)]
    return qkv, sz, tails


def _gated_residual(y_bf16, w_ref, g_ref, mod_ref, x_ref, o_ref):
    nb, tl, D = x_ref.shape
    o = jnp.dot(y_bf16, w_ref[...], preferred_element_type=F32)
    ms = jnp.mean(o * o, axis=-1, keepdims=True)
    o = (o * lax.rsqrt(ms + EPS) * g_ref[...]).reshape(nb, tl, D)
    gate = mod_ref[:, :, 2 * D:3 * D]
    o_ref[...] = x_ref[...] + gate * o


def _post_kernel(y_ref, w_ref, g_ref, mod_ref, x_ref, o_ref):
    nb, tl, K = y_ref.shape
    _gated_residual(y_ref[...].reshape(nb * tl, K).astype(BF16), w_ref, g_ref, mod_ref, x_ref, o_ref)


def _post_gate_kernel(y_ref, sz_ref, w_ref, g_ref, mod_ref, x_ref, o_ref):
    nb, tl, K = y_ref.shape
    y = y_ref[...] * sz_ref[...]
    _gated_residual(y.reshape(nb * tl, K).astype(BF16), w_ref, g_ref, mod_ref, x_ref, o_ref)


def _post_call(kernel, name, ys, w_out, g_post, mod, x):
    B, L, D = x.shape
    K = w_out.shape[0]
    nb, tl = _token_tiling(B, L)
    tok = lambda b, i: (b, i, 0)
    in_specs = [pl.BlockSpec((nb, tl, K), tok) for _ in ys]
    in_specs += [pl.BlockSpec((K, D), lambda b, i: (0, 0)),
                 pl.BlockSpec((1, D), lambda b, i: (0, 0)),
                 pl.BlockSpec((nb, 1, mod.shape[-1]), lambda b, i: (b, 0, 0)),
                 pl.BlockSpec((nb, tl, D), tok)]
    return pl.pallas_call(
        kernel,
        grid=(B // nb, L // tl),
        in_specs=in_specs,
        out_specs=pl.BlockSpec((nb, tl, D), tok),
        out_shape=jax.ShapeDtypeStruct((B, L, D), F32),
        compiler_params=_params("arbitrary", "arbitrary"),
        name=name,
    )(*ys, w_out, g_post.reshape(1, D), mod, x)


def _class_rows(ref, d):
    width = ref.shape[-1] // d
    return jnp.concatenate([ref[0, :, r * width:(r + 1) * width] for r in range(d)], axis=0)


def _post_att_kernel(o0_ref, l0_ref, o1_ref, l1_ref, o2_ref, l2_ref, sz_ref, w_ref, g_ref, mod_ref, x_ref, o_ref):
    _, tm, K = sz_ref.shape
    outs, lses = [], []
    for g, (o_g_ref, l_g_ref) in enumerate(((o0_ref, l0_ref), (o1_ref, l1_ref), (o2_ref, l2_ref))):
        d = ATT_GROUPS[g][1]
        if d == 1:
            outs.append(o_g_ref[0].astype(F32))
            lses.append(l_g_ref[0])
            continue
        per = tm // d
        dst = lax.broadcasted_iota(jnp.int32, (tm, tm), 0)
        src = lax.broadcasted_iota(jnp.int32, (tm, tm), 1)
        back = jnp.where(dst == _class_order_source(src, per, d), 1.0, 0.0).astype(BF16)
        outs.append(jnp.dot(back, _class_rows(o_g_ref, d), preferred_element_type=F32))
        lse = _class_rows(l_g_ref, d)
        hi = lse.astype(BF16)
        rest = lse - hi.astype(F32)
        mid = rest.astype(BF16)
        lo = (rest - mid.astype(F32)).astype(BF16)
        lses.append(jnp.dot(back, hi, preferred_element_type=F32)
                    + jnp.dot(back, mid, preferred_element_type=F32)
                    + jnp.dot(back, lo, preferred_element_type=F32))
    l0, l1, l2 = lses
    m = jnp.maximum(jnp.maximum(l0, l1), l2)
    e0, e1, e2 = jnp.exp(l0 - m), jnp.exp(l1 - m), jnp.exp(l2 - m)
    att = (e0 * outs[0] + e1 * outs[1] + e2 * outs[2]) / (e0 + e1 + e2)
    y = att * sz_ref[0]
    _gated_residual(y.astype(BF16), w_ref, g_ref, mod_ref, x_ref, o_ref)


def _post_att_prompt(att_outs, sz, w_out, g_post, mod, x):
    B, L, D = x.shape
    K = w_out.shape[0]
    tm = ATT_PROJ_ROWS
    tok = lambda b, i: (b, i, 0)
    in_specs, args = [], []
    for g, (W, d) in enumerate(ATT_GROUPS):
        assert tm % (16 * d) == 0
        for t in att_outs[g]:
            in_specs.append(pl.BlockSpec((1, tm // d, d * K), tok))
            args.append(t)
    in_specs += [pl.BlockSpec((1, tm, K), tok),
                 pl.BlockSpec((K, D), lambda b, i: (0, 0)),
                 pl.BlockSpec((1, D), lambda b, i: (0, 0)),
                 pl.BlockSpec((1, 1, mod.shape[-1]), lambda b, i: (b, 0, 0)),
                 pl.BlockSpec((1, tm, D), tok)]
    return pl.pallas_call(
        _post_att_kernel,
        grid=(B, L // tm),
        in_specs=in_specs,
        out_specs=pl.BlockSpec((1, tm, D), tok),
        out_shape=jax.ShapeDtypeStruct((B, L, D), F32),
        compiler_params=_params("arbitrary", "arbitrary"),
        name="post_att_prompt",
    )(*args, sz, w_out, g_post.reshape(1, D), mod, x)


def _conv_kernel(u_ref, sz_ref, buf_ref, w_ref, b_ref, g_ref, bl_ref, y_ref, st_ref, ext_ref, sh_ref, wb_ref):
    nb, tl, C = u_ref.shape
    keep = CONV_SIZE - 1
    lead = CONV_HALO - keep
    rows_sh = sh_ref.shape[1]

    @pl.when(pl.program_id(1) == 0)
    def _():
        ext_ref[:, lead:CONV_HALO, :] = buf_ref[...]
        for j in range(CONV_SIZE):
            wb_ref[j] = jnp.broadcast_to(w_ref[j:j + 1, :], (SUBLANES, C))

    ext_ref[:, CONV_HALO:CONV_HALO + tl, :] = u_ref[...]
    chunk = min(tl, CONV_CHUNK)
    blk_rows = min(tl, CONV_BLOCK_ROWS)
    blk_groups = blk_rows // SUBLANES
    bias = b_ref[...]
    gamma = g_ref[...]
    beta = bl_ref[...]
    for n in range(nb):
        for s in range(1, SUBLANES):
            sh_ref[s - 1] = ext_ref[n, s:s + rows_sh, :]

        def conv_rows(r0):
            acc = jnp.broadcast_to(bias, (blk_groups, SUBLANES, C))
            for j in range(CONV_SIZE):
                a, s = divmod(lead + j, SUBLANES)
                start = pl.multiple_of(r0 + a * SUBLANES, SUBLANES)
                if s == 0:
                    win = ext_ref[n, pl.ds(start, blk_rows), :]
                else:
                    win = sh_ref[s - 1, pl.ds(start, blk_rows), :]
                acc = acc + wb_ref[j] * win.reshape(blk_groups, SUBLANES, C)
            y_ref[n, pl.ds(pl.multiple_of(r0, SUBLANES), blk_rows), :] = acc.reshape(blk_rows, C)

        if tl == blk_rows:
            conv_rows(0)
        else:
            lax.fori_loop(0, tl // blk_rows, lambda ci, c: (conv_rows(ci * blk_rows), c)[1], 0)

        for r0 in range(0, tl, chunk):
            acc = y_ref[n, r0:r0 + chunk, :]
            mu = jnp.mean(acc, axis=-1, keepdims=True)
            xc = acc - mu
            var = jnp.mean(xc * xc, axis=-1, keepdims=True)
            ln = xc * lax.rsqrt(var + EPS) * gamma + beta
            y_ref[n, r0:r0 + chunk, :] = ln * jax.nn.sigmoid(ln) * sz_ref[n, r0:r0 + chunk, :]
    tail = ext_ref[:, lead + tl: CONV_HALO + tl, :]
    st_ref[...] = tail
    ext_ref[:, lead:CONV_HALO, :] = tail


def _conv_mixer(u, sz, buf, w_dw, b_dw, g_ln, b_ln):
    B, L, C = u.shape
    keep = CONV_SIZE - 1
    if L >= CONV_ROWS:
        nb, tl = 1, CONV_ROWS
    else:
        nb, tl = SUBLANES, L
    rows_sh = tl + CONV_HALO - SUBLANES
    tok = lambda b, i: (b, i, 0)
    vec = lambda b, i: (0, 0)
    return pl.pallas_call(
        _conv_kernel,
        grid=(B // nb, L // tl),
        in_specs=[pl.BlockSpec((nb, tl, C), tok),
                  pl.BlockSpec((nb, tl, C), tok),
                  pl.BlockSpec((nb, keep, C), lambda b, i: (b, 0, 0)),
                  pl.BlockSpec((CONV_SIZE, C), vec),
                  pl.BlockSpec((1, C), vec), pl.BlockSpec((1, C), vec), pl.BlockSpec((1, C), vec)],
        out_specs=[pl.BlockSpec((nb, tl, C), tok),
                   pl.BlockSpec((nb, keep, C), lambda b, i: (b, 0, 0))],
        out_shape=[jax.ShapeDtypeStruct((B, L, C), F32),
                   jax.ShapeDtypeStruct((B, keep, C), F32)],
        scratch_shapes=[pltpu.VMEM((nb, CONV_HALO + tl, C), F32),
                        pltpu.VMEM((SUBLANES - 1, rows_sh, C), F32),
                        pltpu.VMEM((CONV_SIZE, SUBLANES, C), F32)],
        compiler_params=_params("arbitrary", "arbitrary"),
        name="conv_mixer",
    )(u, sz, buf, w_dw, b_dw.reshape(1, C), g_ln.reshape(1, C), b_ln.reshape(1, C))


def _gla_kernel(q_ref, k_ref, v_ref, sr_ref, la_ref, s0_ref, g_ref, y_ref, sout_ref, S_ref, *, chunk):
    nb, tl, qk = q_ref.shape
    vw = v_ref.shape[-1]
    H = GLA_HEADS
    dk, dv = qk // H, vw // H
    sub = min(chunk, GLA_SUB)
    nblk = chunk // sub

    @pl.when(pl.program_id(1) == 0)
    def _():
        S_ref[...] = s0_ref[...]

    row = lax.broadcasted_iota(jnp.int32, (chunk, chunk), 0)
    col = lax.broadcasted_iota(jnp.int32, (chunk, chunk), 1)
    tril = jnp.where(row >= col, 1.0, 0.0).astype(BF16)
    row1 = lax.broadcasted_iota(jnp.int32, (chunk, 1), 0)
    sub_row = lax.broadcasted_iota(jnp.int32, (nblk, sub, 1), 1)
    ones = jnp.ones((chunk, LANES), BF16)
    assert dv % LANES == 0
    gamma = g_ref[...]
    tn = (((0,), (0,)), ((), ()))
    nt = (((1,), (1,)), ((), ()))
    halves = []
    m = chunk // 2
    while m >= sub:
        halves.append(m)
        m //= 2

    for n, c0 in [(n, c0) for n in range(nb) for c0 in range(0, tl, chunk)]:
        rows = slice(c0, c0 + chunk)
        la = la_ref[n, rows, :]
        la_hi = la.astype(BF16)
        la_lo = (la - la_hi.astype(F32)).astype(BF16)
        bc = (jnp.dot(tril, la_hi, preferred_element_type=F32)
              + jnp.dot(tril, la_lo, preferred_element_type=F32))
        btot = bc[chunk - 1:chunk, :]
        qq = q_ref[n, rows, :]
        kk = k_ref[n, rows, :]
        q0 = qq * jnp.exp(bc)
        kh = kk * jnp.exp(btot - bc)

        level_ops = []
        for m in halves:
            nb2 = chunk // (2 * m)
            bc3 = bc.reshape(nb2, 2 * m, qk)
            ref = jnp.broadcast_to(bc3[:, m - 1:m, :], (nb2, 2 * m, qk)).reshape(chunk, qk)
            e = jnp.exp(-jnp.abs(bc - ref))
            upper = ((row1 >> _log2(m)) & 1) == 1
            level_ops.append((jnp.where(upper, qq * e, 0.0), jnp.where(upper, 0.0, kk * e)))

        bc3 = bc.reshape(nblk, sub, qk)
        q3 = qq.reshape(nblk, sub, qk)
        k3 = kk.reshape(nblk, sub, qk)
        diag_terms = []
        for jp in range(sub):
            ref = jnp.broadcast_to(bc3[:, jp:jp + 1, :], (nblk, sub, qk))
            kj = jnp.broadcast_to(k3[:, jp:jp + 1, :], (nblk, sub, qk))
            wgt = jnp.where(sub_row >= jp, jnp.exp(jnp.minimum(bc3 - ref, 0.0)), 0.0)
            diag_terms.append((q3 * kj * wgt).reshape(chunk, qk))

        for h in range(H):
            ks = slice(h * dk, (h + 1) * dk)
            vs = slice(h * dv, (h + 1) * dv)
            vh = v_ref[n, rows, vs].astype(BF16)
            att = jnp.zeros((chunk, chunk), F32)
            for m, (qm, km) in zip(halves, level_ops):
                part = lax.dot_general(qm[:, ks].astype(BF16), km[:, ks].astype(BF16), nt,
                                       preferred_element_type=F32)
                if 2 * m < chunk:
                    same = (row >> _log2(2 * m)) == (col >> _log2(2 * m))
                    part = jnp.where(same, part, 0.0)
                att = att + part
            blk0 = (row >> _log2(sub)) << _log2(sub)
            for jp in range(sub):
                rs = jnp.sum(diag_terms[jp][:, ks], axis=-1, keepdims=True)
                att = att + jnp.where(col == blk0 + jp, rs, 0.0)
            S = S_ref[n, h]
            o = (jnp.dot(att.astype(BF16), vh, preferred_element_type=F32)
                 + jnp.dot(q0[:, ks].astype(BF16), S.astype(BF16), preferred_element_type=F32))
            tot = (lax.dot_general(la_hi[:, ks], ones, tn, preferred_element_type=F32)
                   + lax.dot_general(la_lo[:, ks], ones, tn, preferred_element_type=F32))
            decay = jnp.exp(jnp.concatenate([tot] * (dv // LANES), axis=1))
            S_ref[n, h] = decay * S + lax.dot_general(kh[:, ks].astype(BF16), vh, tn, preferred_element_type=F32)
            ms = jnp.mean(o * o, axis=-1, keepdims=True)
            y_ref[n, rows, vs] = o * lax.rsqrt(ms + EPS) * gamma * sr_ref[n, rows, vs]

    sout_ref[...] = S_ref[...]


def _gla_mixer(q, k, v, sr, la, s0, g_norm):
    B, L, qk = q.shape
    vw = v.shape[-1]
    H = GLA_HEADS
    dk, dv = qk // H, vw // H
    tl = min(L, GLA_ROWS)
    chunk = min(L, GLA_CHUNK)
    nb = 1 if L >= GLA_ROWS else GLA_SEQS
    assert B % nb == 0
    tok = lambda b, i: (b, i, 0)
    st = lambda b, i: (b, 0, 0, 0)
    return pl.pallas_call(
        functools.partial(_gla_kernel, chunk=chunk),
        grid=(B // nb, L // tl),
        in_specs=[pl.BlockSpec((nb, tl, qk), tok), pl.BlockSpec((nb, tl, qk), tok),
                  pl.BlockSpec((nb, tl, vw), tok), pl.BlockSpec((nb, tl, vw), tok),
                  pl.BlockSpec((nb, tl, qk), tok),
                  pl.BlockSpec((nb, H, dk, dv), st),
                  pl.BlockSpec((1, dv), lambda b, i: (0, 0))],
        out_specs=[pl.BlockSpec((nb, tl, vw), tok),
                   pl.BlockSpec((nb, H, dk, dv), st)],
        out_shape=[jax.ShapeDtypeStruct((B, L, vw), F32),
                   jax.ShapeDtypeStruct((B, H, dk, dv), F32)],
        scratch_shapes=[pltpu.VMEM((nb, H, dk, dv), F32)],
        compiler_params=_params("arbitrary", "arbitrary"),
        name="gla_mixer",
    )(q, k, v, sr, la, s0, g_norm.reshape(1, dv))


def _att_prompt_kernel(q_ref, kp_ref, kc_ref, vp_ref, vc_ref, o_ref, l_ref):
    blk = q_ref.shape[1]
    first = pl.program_id(2) == 0
    row = lax.broadcasted_iota(jnp.int32, (blk, 2 * blk), 0)
    col = lax.broadcasted_iota(jnp.int32, (blk, 2 * blk), 1)
    valid = (col >= row) & (col <= row + blk) & jnp.logical_or(col >= blk, jnp.logical_not(first))
    q = q_ref[0]
    kk = jnp.concatenate([kp_ref[0], kc_ref[0]], axis=0)
    vv = jnp.concatenate([vp_ref[0], vc_ref[0]], axis=0)
    nt = (((1,), (1,)), ((), ()))
    heads_per_tile = LANES // ATT_HEAD_DIM
    lane = lax.broadcasted_iota(jnp.int32, (1, LANES), 1)
    for t in range(q.shape[1] // LANES):
        ls = slice(t * LANES, (t + 1) * LANES)
        qp, kp, vp = q[:, ls], kk[:, ls], vv[:, ls]
        o_tile = jnp.zeros((blk, LANES), F32)
        l_tile = jnp.zeros((blk, LANES), F32)
        for hh in range(heads_per_tile):
            mine = (lane >= hh * ATT_HEAD_DIM) & (lane < (hh + 1) * ATT_HEAD_DIM)
            qm = qp * jnp.where(mine, 1.0, 0.0).astype(qp.dtype)
            s = lax.dot_general(qm, kp, nt, preferred_element_type=F32)
            s = jnp.where(valid, s, NEG_INF)
            m = jnp.max(s, axis=-1, keepdims=True)
            p = jnp.exp(s - m)
            den = jnp.sum(p, axis=-1, keepdims=True)
            o = jnp.dot(p.astype(BF16), vp, preferred_element_type=F32) / den
            o_tile = jnp.where(mine, o, o_tile)
            l_tile = jnp.where(mine, m + jnp.log(den), l_tile)
        o_ref[0, :, ls] = o_tile.astype(o_ref.dtype)
        l_ref[0, :, ls] = l_tile


def _att_prompt_group(q, k, v, g):
    W, d = ATT_GROUPS[g]
    B, Ld, dw = q.shape
    width = dw // d
    blk = min(W, Ld * d) // d
    assert blk == ATT_BLOCK and Ld % blk == 0
    cur = lambda b, r, i: (b, i, r)
    prev = lambda b, r, i: (b, jnp.maximum(i - 1, 0), r)
    return pl.pallas_call(
        _att_prompt_kernel,
        grid=(B, d, Ld // blk),
        in_specs=[pl.BlockSpec((1, blk, width), cur),
                  pl.BlockSpec((1, blk, width), prev), pl.BlockSpec((1, blk, width), cur),
                  pl.BlockSpec((1, blk, width), prev), pl.BlockSpec((1, blk, width), cur)],
        out_specs=[pl.BlockSpec((1, blk, width), cur), pl.BlockSpec((1, blk, width), cur)],
        out_shape=[jax.ShapeDtypeStruct((B, Ld, dw), BF16),
                   jax.ShapeDtypeStruct((B, Ld, dw), F32)],
        compiler_params=_params("arbitrary", "arbitrary", "arbitrary"),
        name="att_prompt_g%d" % g,
    )(q, k, k, v, v)


def _att_sample_kernel(q_ref, k_ref, v_ref, ck0_ref, cv0_ref, ck1_ref, cv1_ref, ck2_ref, cv2_ref, y_ref, *, nbuf):
    G, H, n_new, Dh = q_ref.shape[1:]
    caches = ((ck0_ref, cv0_ref), (ck1_ref, cv1_ref), (ck2_ref, cv2_ref))
    nt = (((1,), (1,)), ((), ()))
    biases = []
    for g, (W, d) in enumerate(ATT_GROUPS):
        nb = nbuf[g]
        i = lax.broadcasted_iota(jnp.int32, (n_new, nb), 0)
        c = lax.broadcasted_iota(jnp.int32, (n_new, nb), 1)
        ok = ((c & (d - 1)) == (i & (d - 1))) & (c >= i - (W - nb))
        i2 = lax.broadcasted_iota(jnp.int32, (n_new, n_new), 0)
        j2 = lax.broadcasted_iota(jnp.int32, (n_new, n_new), 1)
        ok2 = (j2 <= i2) & (((i2 - j2) & (d - 1)) == 0)
        biases.append((jnp.where(ok, 0.0, NEG_INF).astype(F32), jnp.where(ok2, 0.0, NEG_INF).astype(F32)))
    for h in range(H):
        scores = []
        for g in range(G):
            qh = q_ref[0, g, h].astype(BF16)
            kt = caches[g][0][0, h].astype(BF16)
            scores.append(jnp.dot(qh, kt, preferred_element_type=F32) + biases[g][0])
            kn = k_ref[0, g, h].astype(BF16)
            scores.append(lax.dot_general(qh, kn, nt, preferred_element_type=F32) + biases[g][1])
        m = functools.reduce(jnp.maximum, [jnp.max(s, axis=-1, keepdims=True) for s in scores])
        acc = jnp.zeros((n_new, Dh), F32)
        den = jnp.zeros((n_new, 1), F32)
        for g in range(G):
            pc = jnp.exp(scores[2 * g] - m).astype(BF16)
            pn = jnp.exp(scores[2 * g + 1] - m).astype(BF16)
            den = den + jnp.sum(pc.astype(F32), axis=-1, keepdims=True) + jnp.sum(pn.astype(F32), axis=-1, keepdims=True)
            vt = caches[g][1][0, h].astype(BF16)
            acc = acc + lax.dot_general(pc, vt, nt, preferred_element_type=F32)
            acc = acc + jnp.dot(pn, v_ref[0, g, h].astype(BF16), preferred_element_type=F32)
        y_ref[0, h] = acc / den


def _att_sample(q, k, v, caches_k, caches_v):
    B, G, H, n_new, Dh = q.shape
    specs, nbuf = [], []
    for g, (W, d) in enumerate(ATT_GROUPS):
        nb = caches_k[g].shape[-1]
        assert d & (d - 1) == 0 and (W - nb) % d == 0
        nbuf.append(nb)
        specs += [pl.BlockSpec((1, H, Dh, nb), lambda b: (b, 0, 0, 0))] * 2
    new = pl.BlockSpec((1, G, H, n_new, Dh), lambda b: (b, 0, 0, 0, 0))
    args = [t for g in range(G) for t in (caches_k[g], caches_v[g])]
    return pl.pallas_call(
        functools.partial(_att_sample_kernel, nbuf=tuple(nbuf)),
        grid=(B,),
        in_specs=[new, new, new] + specs,
        out_specs=pl.BlockSpec((1, H, n_new, Dh), lambda b: (b, 0, 0, 0)),
        out_shape=jax.ShapeDtypeStruct((B, H, n_new, Dh), F32),
        compiler_params=_params("arbitrary"),
        name="att_sample",
    )(q, k, v, *args)


def _rope_tables(pos):
    half = ATT_HEAD_DIM // 2
    inv = ROPE_THETA ** (-jnp.arange(half, dtype=F32) / half)
    ang = pos.astype(F32)[:, None] * inv[None, :]
    cos, sin = jnp.cos(ang), jnp.sin(ang)
    reps = LANES // ATT_HEAD_DIM
    cos_t = jnp.tile(jnp.concatenate([cos, cos], axis=-1), (1, reps))
    sin_t = jnp.tile(jnp.concatenate([-sin, sin], axis=-1), (1, reps))
    return cos_t, sin_t


def kernel(x_prompt, x_sample, state_conv, state_gla, cache_k_g0, cache_v_g0, cache_k_g1, cache_v_g1, cache_k_g2, cache_v_g2, c_prompt, c_sample, w_ada, b_ada, g_pre, g_post, w_conv_in, w_dw, b_dw, g_conv_ln, b_conv_ln, w_conv_out, w_gla_in, w_gla_a1, w_gla_a2, b_gla_a, g_gla_norm, w_gla_out, w_att_in, w_att_out):
    depth = w_ada.shape[0]
    D = x_prompt.shape[-1]
    G = len(ATT_GROUPS)
    xs = [x_prompt, x_sample]
    nseq = [x_prompt.shape[0], x_sample.shape[0]]
    caches_k = (cache_k_g0, cache_k_g1, cache_k_g2)
    caches_v = (cache_v_g0, cache_v_g1, cache_v_g2)

    n_c = nseq[0] + nseq[1]
    pad = (-n_c) % SUBLANES
    c_all = jnp.concatenate([c_prompt, c_sample, jnp.zeros((pad, D), F32)], axis=0)
    mod_all = _modulation(c_all, w_ada, b_ada)
    mods = [[mod_all[l, :nseq[0]].reshape(nseq[0], 1, 3 * D),
             mod_all[l, nseq[0]:n_c].reshape(nseq[1], 1, 3 * D)] for l in range(depth)]

    pos = [jnp.arange(x_prompt.shape[1]), PAST_LEN + jnp.arange(x_sample.shape[1])]
    rope = [_rope_tables(p) for p in pos]

    conv_new, gla_new = ([], []), ([], [])
    k_new = ([[] for _ in range(G)], [[] for _ in range(G)])
    v_new = ([[] for _ in range(G)], [[] for _ in range(G)])
    no_extra = lambda nb, tl: []
    heads = (ATT_HEADS, ATT_HEAD_DIM)

    for l in range(depth):
        kind, j = l % N_MIXERS, l // N_MIXERS
        for grp in range(2):
            x, mod = xs[grp], mods[l][grp]
            B, L, _ = x.shape
            if kind == 0:
                C = w_dw.shape[-1]
                u, sz = _pre_call(_pre_conv_kernel, "pre_conv", x, mod, g_pre[l],
                                  [w_conv_in[j].astype(BF16)], no_extra, [], [C, C])
                buf = jnp.zeros((B, CONV_SIZE - 1, C), F32) if grp == 0 else state_conv[j]
                y, st = _conv_mixer(u, sz, buf, w_dw[j], b_dw[j], g_conv_ln[j], b_conv_ln[j])
                conv_new[grp].append(st)
                xs[grp] = _post_call(_post_kernel, "post_conv", [y], w_conv_out[j].astype(BF16), g_post[l], mod, x)
            elif kind == 1:
                qk = w_gla_a2.shape[-1]
                vw = w_gla_out.shape[1]
                wa1 = jnp.zeros((D, LANES), F32).at[:, :GLA_RANK].set(w_gla_a1[j]).astype(BF16)
                wa2 = jnp.zeros((LANES, qk), F32).at[:GLA_RANK].set(w_gla_a2[j]).astype(BF16)
                q, k, v, sr, la = _pre_call(
                    _pre_gla_kernel, "pre_gla", x, mod, g_pre[l],
                    [w_gla_in[j].astype(BF16), wa1, wa2, b_gla_a[j].reshape(1, qk)], no_extra, [],
                    [qk, qk, vw, vw, qk])
                dk, dv = qk // GLA_HEADS, vw // GLA_HEADS
                s0 = jnp.zeros((B, GLA_HEADS, dk, dv), F32) if grp == 0 else state_gla[j]
                y, st = _gla_mixer(q, k, v, sr, la, s0, g_gla_norm[j])
                gla_new[grp].append(st)
                xs[grp] = _post_call(_post_kernel, "post_gla", [y], w_gla_out[j].astype(BF16), g_post[l], mod, x)
            elif grp == 0:
                qkv, sz, tails = _pre_att_prompt(x, mod, g_pre[l], w_att_in[j], pos[grp])
                att_outs = [_att_prompt_group(*qkv[g], g) for g in range(G)]
                xs[grp] = _post_att_prompt(att_outs, sz, w_att_out[j].astype(BF16), g_post[l], mod, x)
                for g, (W, d) in enumerate(ATT_GROUPS):
                    keep = min(W, L)
                    for store, t in ((k_new, tails[g][0]), (v_new, tails[g][1])):
                        t = t[:, t.shape[1] - keep // d:, :]
                        store[grp][g].append(t.reshape(B, keep, *heads))
            else:
                gw = G * ATT_WIDTH
                cos_t, sin_t = rope[grp]
                rope_specs = lambda nb, tl: [pl.BlockSpec((tl, LANES), lambda b, i: (i, 0))] * 2
                q, k, v, sz = _pre_call(_pre_att_kernel, "pre_att", x, mod, g_pre[l],
                                        [w_att_in[j].astype(BF16)], rope_specs, [cos_t, sin_t],
                                        [gw, gw, gw, ATT_WIDTH])
                ck = [jnp.transpose(c[j], (0, 2, 3, 1)) for c in caches_k]
                cv = [jnp.transpose(c[j], (0, 2, 3, 1)) for c in caches_v]
                q5, k5, v5 = (t.reshape(B, L, G, *heads) for t in (q, k, v))
                qt, kt, vt = (jnp.transpose(t, (0, 2, 3, 1, 4)) for t in (q5, k5, v5))
                y = _att_sample(qt, kt, vt, ck, cv)
                y = jnp.transpose(y, (0, 2, 1, 3)).reshape(B, L, ATT_WIDTH)
                xs[grp] = _post_call(_post_gate_kernel, "post_att_sample", [y, sz],
                                     w_att_out[j].astype(BF16), g_post[l], mod, x)
                for g in range(G):
                    k_new[grp][g].append(k5[:, :, g])
                    v_new[grp][g].append(v5[:, :, g])

    outs = [xs[0], xs[1], jnp.stack(conv_new[0]), jnp.stack(conv_new[1]),
            jnp.stack(gla_new[0]), jnp.stack(gla_new[1])]
    for grp in range(2):
        for g in range(G):
            outs += [jnp.stack(k_new[grp][g]), jnp.stack(v_new[grp][g])]
    return tuple(outs)
```

```python
import functools

import jax
import jax.numpy as jnp
from jax import lax
from jax.experimental import pallas as pl
from jax.experimental.pallas import tpu as pltpu

F32 = jnp.float32
BF16 = jnp.bfloat16
HIGHEST = lax.Precision.HIGHEST

PAST_LEN = 2048
N_MIXERS = 3
CONV_SIZE = 31
GLA_HEADS = 4
GLA_RANK = 16
GLA_TAU = 16.0
ATT_GROUPS = ((128, 1), (512, 4), (2048, 16))
ATT_HEADS = 8
ATT_HEAD_DIM = 64
ATT_WIDTH = ATT_HEADS * ATT_HEAD_DIM
ROPE_THETA = 10000.0
EPS = 1e-6
NEG_INF = -1e30

LANES = 128
SUBLANES = 8
VMEM_LIMIT_BYTES = 56 * 1024 * 1024

PROJ_ROWS = 512
CONV_ROWS = 256
CONV_CHUNK = 32
CONV_BLOCK_ROWS = 32
CONV_HALO = 32
GLA_ROWS = 256
GLA_CHUNK = 64
GLA_SUB = 8
GLA_SEQS = 4
ATT_BLOCK = 128
ATT_PROJ_ROWS = 256


def _params(*sem):
    return pltpu.CompilerParams(dimension_semantics=sem, vmem_limit_bytes=VMEM_LIMIT_BYTES)


def _log2(n):
    assert n > 0 and n & (n - 1) == 0, n
    return n.bit_length() - 1


def _token_tiling(B, L, rows=PROJ_ROWS):
    if L >= rows:
        return 1, rows
    nb = max(1, rows // L)
    while B % nb:
        nb //= 2
    return nb, L


def _mod_kernel(c_ref, w_ref, b_ref, o_ref):
    o_ref[0] = jnp.dot(c_ref[...], w_ref[0], precision=HIGHEST, preferred_element_type=F32) + b_ref[0]


def _modulation(c_all, w_ada, b_ada):
    depth, D, N = w_ada.shape
    R = c_all.shape[0]
    tn = 1024
    return pl.pallas_call(
        _mod_kernel,
        grid=(depth, N // tn),
        in_specs=[pl.BlockSpec((R, D), lambda l, j: (0, 0)),
                  pl.BlockSpec((1, D, tn), lambda l, j: (l, 0, j)),
                  pl.BlockSpec((1, 1, tn), lambda l, j: (l, 0, j))],
        out_specs=pl.BlockSpec((1, R, tn), lambda l, j: (l, 0, j)),
        out_shape=jax.ShapeDtypeStruct((depth, R, N), F32),
        compiler_params=_params("arbitrary", "arbitrary"),
        name="adaln_mod",
    )(c_all, w_ada, b_ada.reshape(depth, 1, N))


def _modulated_norm(x_ref, mod_ref, g_ref):
    x = x_ref[...]
    nb, tl, D = x.shape
    ms = jnp.mean(x * x, axis=-1, keepdims=True)
    y = x * lax.rsqrt(ms + EPS) * g_ref[...]
    shift = mod_ref[:, :, 0:D]
    scale = mod_ref[:, :, D:2 * D]
    h = y * (1.0 + scale) + shift
    return h.reshape(nb * tl, D).astype(BF16)


def _pre_conv_kernel(x_ref, mod_ref, g_ref, w_ref, u_ref, sz_ref):
    h = _modulated_norm(x_ref, mod_ref, g_ref)
    res = jnp.dot(h, w_ref[...], preferred_element_type=F32)
    C = u_ref.shape[-1]
    u = res[:, :C] * jax.nn.sigmoid(res[:, C:2 * C])
    z = res[:, 2 * C:]
    u_ref[...] = u.reshape(u_ref.shape)
    sz_ref[...] = (z * jax.nn.sigmoid(z)).reshape(sz_ref.shape)


def _pre_gla_kernel(x_ref, mod_ref, g_ref, w_ref, wa1_ref, wa2_ref, ba_ref,
                    q_ref, k_ref, v_ref, sr_ref, la_ref):
    h = _modulated_norm(x_ref, mod_ref, g_ref)
    res = jnp.dot(h, w_ref[...], preferred_element_type=F32)
    qk = q_ref.shape[-1]
    vw = v_ref.shape[-1]
    dk = qk // GLA_HEADS
    q_ref[...] = (res[:, :qk] * (dk ** -0.5)).reshape(q_ref.shape)
    k_ref[...] = res[:, qk:2 * qk].reshape(k_ref.shape)
    v_ref[...] = res[:, 2 * qk:2 * qk + vw].reshape(v_ref.shape)
    r = res[:, 2 * qk + vw:]
    sr_ref[...] = (r * jax.nn.sigmoid(r)).reshape(sr_ref.shape)
    low = jnp.dot(h, wa1_ref[...], preferred_element_type=F32)
    zg = jnp.dot(low.astype(BF16), wa2_ref[...], preferred_element_type=F32) + ba_ref[...]
    log_sig = jnp.minimum(zg, 0.0) - jnp.log(1.0 + jnp.exp(-jnp.abs(zg)))
    la_ref[...] = (log_sig * (1.0 / GLA_TAU)).reshape(la_ref.shape)


def _swap_halves(x):
    half = ATT_HEAD_DIM // 2
    lane = lax.broadcasted_iota(jnp.int32, x.shape, 1)
    lower = (lane % ATT_HEAD_DIM) < half
    return jnp.where(lower, pltpu.roll(x, LANES - half, 1), pltpu.roll(x, half, 1))


def _rope_cols(res, col0, width, cos, sin, mult):
    pieces = []
    for c in range(width // LANES):
        xc = res[:, col0 + c * LANES: col0 + (c + 1) * LANES]
        val = xc * cos + _swap_halves(xc) * sin
        pieces.append(val * mult if mult != 1.0 else val)
    return pieces


def _pre_att_kernel(x_ref, mod_ref, g_ref, w_ref, cos_ref, sin_ref, q_ref, k_ref, v_ref, sz_ref):
    h = _modulated_norm(x_ref, mod_ref, g_ref)
    res = jnp.dot(h, w_ref[...], preferred_element_type=F32)
    nb, tl, gw = q_ref.shape
    cos = jnp.broadcast_to(cos_ref[...], (nb, tl, LANES)).reshape(nb * tl, LANES)
    sin = jnp.broadcast_to(sin_ref[...], (nb, tl, LANES)).reshape(nb * tl, LANES)
    for c, val in enumerate(_rope_cols(res, 0, gw, cos, sin, ATT_HEAD_DIM ** -0.5)):
        q_ref[:, :, c * LANES:(c + 1) * LANES] = val.reshape(nb, tl, LANES)
    for c, val in enumerate(_rope_cols(res, gw, gw, cos, sin, 1.0)):
        k_ref[:, :, c * LANES:(c + 1) * LANES] = val.reshape(nb, tl, LANES)
    v_ref[...] = res[:, 2 * gw:3 * gw].reshape(v_ref.shape)
    z = res[:, 3 * gw:]
    sz_ref[...] = (z * jax.nn.sigmoid(z)).reshape(sz_ref.shape)


def _pre_call(kernel, name, x, mod, g_pre, weights, extra_specs, extras, out_widths):
    B, L, D = x.shape
    nb, tl = _token_tiling(B, L, PROJ_ROWS if sum(out_widths) <= 7 * D // 2 else PROJ_ROWS // 2)
    tok = lambda b, i: (b, i, 0)
    in_specs = [pl.BlockSpec((nb, tl, D), tok),
                pl.BlockSpec((nb, 1, mod.shape[-1]), lambda b, i: (b, 0, 0)),
                pl.BlockSpec((1, D), lambda b, i: (0, 0))]
    in_specs += [pl.BlockSpec(w.shape, lambda b, i: (0, 0)) for w in weights]
    in_specs += extra_specs(nb, tl)
    return pl.pallas_call(
        kernel,
        grid=(B // nb, L // tl),
        in_specs=in_specs,
        out_specs=[pl.BlockSpec((nb, tl, n), tok) for n in out_widths],
        out_shape=[jax.ShapeDtypeStruct((B, L, n), F32) for n in out_widths],
        compiler_params=_params("arbitrary", "arbitrary"),
        name=name,
    )(x, mod, g_pre.reshape(1, D), *weights, *extras)


def _class_order_source(idx, per, d):
    return (idx & (per - 1)) * d + (idx >> _log2(per))


def _pre_att_prompt_kernel(x_ref, mod_ref, g_ref, w0_ref, w1_ref, w2_ref, wz_ref,
                           c0_ref, s0_ref, c1_ref, s1_ref, c2_ref, s2_ref,
                           q0_ref, k0_ref, v0_ref, q1_ref, k1_ref, v1_ref, q2_ref, k2_ref, v2_ref, sz_ref,
                           kt0_ref, vt0_ref, kt1_ref, vt1_ref, kt2_ref, vt2_ref, *, first_tail):
    h = _modulated_norm(x_ref, mod_ref, g_ref)
    tm = h.shape[0]
    width = sz_ref.shape[-1]
    z = jnp.dot(h, wz_ref[...], preferred_element_type=F32)
    sz_ref[0] = z * jax.nn.sigmoid(z)
    step = pl.program_id(1)
    groups = ((w0_ref, c0_ref, s0_ref, q0_ref, k0_ref, v0_ref, kt0_ref, vt0_ref),
              (w1_ref, c1_ref, s1_ref, q1_ref, k1_ref, v1_ref, kt1_ref, vt1_ref),
              (w2_ref, c2_ref, s2_ref, q2_ref, k2_ref, v2_ref, kt2_ref, vt2_ref))
    for g, (w_ref, cos_ref, sin_ref, q_ref, k_ref, v_ref, kt_ref, vt_ref) in enumerate(groups):
        d = ATT_GROUPS[g][1]
        per = tm // d
        if d > 1:
            dst = lax.broadcasted_iota(jnp.int32, (tm, tm), 0)
            src = lax.broadcasted_iota(jnp.int32, (tm, tm), 1)
            perm = jnp.where(src == _class_order_source(dst, per, d), 1.0, 0.0).astype(BF16)
            hg = jnp.dot(perm, h, preferred_element_type=F32).astype(BF16)
        else:
            hg = h
        res = jnp.dot(hg, w_ref[...], preferred_element_type=F32)
        cos, sin = cos_ref[...], sin_ref[...]
        qf = jnp.concatenate(_rope_cols(res, 0, width, cos, sin, ATT_HEAD_DIM ** -0.5), axis=1)
        kf = jnp.concatenate(_rope_cols(res, width, width, cos, sin, 1.0), axis=1)
        vf = res[:, 2 * width:]
        qb, kb, vb = qf.astype(BF16), kf.astype(BF16), vf.astype(BF16)
        for r in range(d):
            rs = slice(r * per, (r + 1) * per)
            cs = slice(r * width, (r + 1) * width)
            q_ref[0, :, cs] = qb[rs]
            k_ref[0, :, cs] = kb[rs]
            v_ref[0, :, cs] = vb[rs]

        @pl.when(step >= first_tail[g])
        def _(kf=kf, vf=vf, kt_ref=kt_ref, vt_ref=vt_ref, d=d, per=per):
            for r in range(d):
                rs = slice(r * per, (r + 1) * per)
                cs = slice(r * width, (r + 1) * width)
                kt_ref[0, :, cs] = kf[rs]
                vt_ref[0, :, cs] = vf[rs]


def _pre_att_prompt(x, mod, g_pre, w_att_in, pos):
    B, L, D = x.shape
    G = len(ATT_GROUPS)
    width = ATT_WIDTH
    gw = G * width
    tm = ATT_PROJ_ROWS
    n_steps = L // tm
    w = w_att_in.astype(BF16)
    weights, tables, first_tail = [], [], []
    out_specs, out_shapes = [], []
    tail_specs, tail_shapes = [], []
    for g, (W, d) in enumerate(ATT_GROUPS):
        gs = slice(g * width, (g + 1) * width)
        weights.append(jnp.concatenate([w[:, gs], w[:, gw:][:, gs], w[:, 2 * gw:][:, gs]], axis=1))
        per = tm // d
        assert tm % d == 0 and per % 16 == 0 and L % tm == 0
        a = jnp.arange(tm)
        order = (jnp.arange(n_steps)[:, None] * tm + _class_order_source(a, per, d)[None, :]).reshape(-1)
        tables += list(_rope_tables(pos[order]))
        for _ in range(3):
            out_specs.append(pl.BlockSpec((1, per, d * width), lambda b, i: (b, i, 0)))
            out_shapes.append(jax.ShapeDtypeStruct((B, L // d, d * width), BF16))
        tail_rows = max(min(W, L) // d, per)
        n_tail = tail_rows // per
        first_tail.append(n_steps - n_tail)
        for _ in range(2):
            tail_specs.append(pl.BlockSpec(
                (1, per, d * width), lambda b, i, first=n_steps - n_tail: (b, jnp.maximum(i - first, 0), 0)))
            tail_shapes.append(jax.ShapeDtypeStruct((B, tail_rows, d * width), F32))
    weights.append(w[:, 3 * gw:])
    const = lambda b, i: (0, 0)
    in_specs = [pl.BlockSpec((1, tm, D), lambda b, i: (b, i, 0)),
                pl.BlockSpec((1, 1, mod.shape[-1]), lambda b, i: (b, 0, 0)),
                pl.BlockSpec((1, D), const)]
    in_specs += [pl.BlockSpec(wt.shape, const) for wt in weights]
    in_specs += [pl.BlockSpec((tm, LANES), lambda b, i: (i, 0)) for _ in tables]
    sz_spec = pl.BlockSpec((1, tm, width), lambda b, i: (b, i, 0))
    outs = pl.pallas_call(
        functools.partial(_pre_att_prompt_kernel, first_tail=tuple(first_tail)),
        grid=(B, n_steps),
        in_specs=in_specs,
        out_specs=out_specs + [sz_spec] + tail_specs,
        out_shape=out_shapes + [jax.ShapeDtypeStruct((B, L, width), F32)] + tail_shapes,
        compiler_params=_params("arbitrary", "arbitrary"),
        name="pre_att_prompt",
    )(x, mod, g_pre.reshape(1, D), *weights, *tables)
    qkv = [tuple(outs[3 * g:3 * g + 3]) for g in range(G)]
    sz = outs[3 * G]
    tails = [tuple(outs[3 * G + 1 + 2 * g: 3 * G + 3 + 2 * g]) for g in range(G)]
    return qkv, sz, tails


def _gated_residual(y_bf16, w_ref, g_ref, mod_ref, x_ref, o_ref):
    nb, tl, D = x_ref.shape
    o = jnp.dot(y_bf16, w_ref[...], preferred_element_type=F32)
    ms = jnp.mean(o * o, axis=-1, keepdims=True)
    o = (o * lax.rsqrt(ms + EPS) * g_ref[...]).reshape(nb, tl, D)
    gate = mod_ref[:, :, 2 * D:3 * D]
    o_ref[...] = x_ref[...] + gate * o


def _post_kernel(y_ref, w_ref, g_ref, mod_ref, x_ref, o_ref):
    nb, tl, K = y_ref.shape
    _gated_residual(y_ref[...].reshape(nb * tl, K).astype(BF16), w_ref, g_ref, mod_ref, x_ref, o_ref)


def _post_gate_kernel(y_ref, sz_ref, w_ref, g_ref, mod_ref, x_ref, o_ref):
    nb, tl, K = y_ref.shape
    y = y_ref[...] * sz_ref[...]
    _gated_residual(y.reshape(nb * tl, K).astype(BF16), w_ref, g_ref, mod_ref, x_ref, o_ref)


def _post_call(kernel, name, ys, w_out, g_post, mod, x):
    B, L, D = x.shape
    K = w_out.shape[0]
    nb, tl = _token_tiling(B, L)
    tok = lambda b, i: (b, i, 0)
    in_specs = [pl.BlockSpec((nb, tl, K), tok) for _ in ys]
    in_specs += [pl.BlockSpec((K, D), lambda b, i: (0, 0)),
                 pl.BlockSpec((1, D), lambda b, i: (0, 0)),
                 pl.BlockSpec((nb, 1, mod.shape[-1]), lambda b, i: (b, 0, 0)),
                 pl.BlockSpec((nb, tl, D), tok)]
    return pl.pallas_call(
        kernel,
        grid=(B // nb, L // tl),
        in_specs=in_specs,
        out_specs=pl.BlockSpec((nb, tl, D), tok),
        out_shape=jax.ShapeDtypeStruct((B, L, D), F32),
        compiler_params=_params("arbitrary", "arbitrary"),
        name=name,
    )(*ys, w_out, g_post.reshape(1, D), mod, x)


def _class_rows(ref, d):
    width = ref.shape[-1] // d
    return jnp.concatenate([ref[0, :, r * width:(r + 1) * width] for r in range(d)], axis=0)


def _post_att_kernel(o0_ref, l0_ref, o1_ref, l1_ref, o2_ref, l2_ref, sz_ref, w_ref, g_ref, mod_ref, x_ref, o_ref):
    _, tm, K = sz_ref.shape
    outs, lses = [], []
    for g, (o_g_ref, l_g_ref) in enumerate(((o0_ref, l0_ref), (o1_ref, l1_ref), (o2_ref, l2_ref))):
        d = ATT_GROUPS[g][1]
        if d == 1:
            outs.append(o_g_ref[0].astype(F32))
            lses.append(l_g_ref[0])
            continue
        per = tm // d
        dst = lax.broadcasted_iota(jnp.int32, (tm, tm), 0)
        src = lax.broadcasted_iota(jnp.int32, (tm, tm), 1)
        back = jnp.where(dst == _class_order_source(src, per, d), 1.0, 0.0).astype(BF16)
        outs.append(jnp.dot(back, _class_rows(o_g_ref, d), preferred_element_type=F32))
        lse = _class_rows(l_g_ref, d)
        hi = lse.astype(BF16)
        rest = lse - hi.astype(F32)
        mid = rest.astype(BF16)
        lo = (rest - mid.astype(F32)).astype(BF16)
        lses.append(jnp.dot(back, hi, preferred_element_type=F32)
                    + jnp.dot(back, mid, preferred_element_type=F32)
                    + jnp.dot(back, lo, preferred_element_type=F32))
    l0, l1, l2 = lses
    m = jnp.maximum(jnp.maximum(l0, l1), l2)
    e0, e1, e2 = jnp.exp(l0 - m), jnp.exp(l1 - m), jnp.exp(l2 - m)
    att = (e0 * outs[0] + e1 * outs[1] + e2 * outs[2]) / (e0 + e1 + e2)
    y = att * sz_ref[0]
    _gated_residual(y.astype(BF16), w_ref, g_ref, mod_ref, x_ref, o_ref)


def _post_att_prompt(att_outs, sz, w_out, g_post, mod, x):
    B, L, D = x.shape
    K = w_out.shape[0]
    tm = ATT_PROJ_ROWS
    tok = lambda b, i: (b, i, 0)
    in_specs, args = [], []
    for g, (W, d) in enumerate(ATT_GROUPS):
        assert tm % (16 * d) == 0
        for t in att_outs[g]:
            in_specs.append(pl.BlockSpec((1, tm // d, d * K), tok))
            args.append(t)
    in_specs += [pl.BlockSpec((1, tm, K), tok),
                 pl.BlockSpec((K, D), lambda b, i: (0, 0)),
                 pl.BlockSpec((1, D), lambda b, i: (0, 0)),
                 pl.BlockSpec((1, 1, mod.shape[-1]), lambda b, i: (b, 0, 0)),
                 pl.BlockSpec((1, tm, D), tok)]
    return pl.pallas_call(
        _post_att_kernel,
        grid=(B, L // tm),
        in_specs=in_specs,
        out_specs=pl.BlockSpec((1, tm, D), tok),
        out_shape=jax.ShapeDtypeStruct((B, L, D), F32),
        compiler_params=_params("arbitrary", "arbitrary"),
        name="post_att_prompt",
    )(*args, sz, w_out, g_post.reshape(1, D), mod, x)


def _conv_kernel(u_ref, sz_ref, buf_ref, w_ref, b_ref, g_ref, bl_ref, y_ref, st_ref, ext_ref, sh_ref, wb_ref):
    nb, tl, C = u_ref.shape
    keep = CONV_SIZE - 1
    lead = CONV_HALO - keep
    rows_sh = sh_ref.shape[1]

    @pl.when(pl.program_id(1) == 0)
    def _():
        ext_ref[:, lead:CONV_HALO, :] = buf_ref[...]
        for j in range(CONV_SIZE):
            wb_ref[j] = jnp.broadcast_to(w_ref[j:j + 1, :], (SUBLANES, C))

    ext_ref[:, CONV_HALO:CONV_HALO + tl, :] = u_ref[...]
    chunk = min(tl, CONV_CHUNK)
    blk_rows = min(tl, CONV_BLOCK_ROWS)
    blk_groups = blk_rows // SUBLANES
    bias = b_ref[...]
    gamma = g_ref[...]
    beta = bl_ref[...]
    for n in range(nb):
        for s in range(1, SUBLANES):
            sh_ref[s - 1] = ext_ref[n, s:s + rows_sh, :]

        def conv_rows(r0):
            acc = jnp.broadcast_to(bias, (blk_groups, SUBLANES, C))
            for j in range(CONV_SIZE):
                a, s = divmod(lead + j, SUBLANES)
                start = pl.multiple_of(r0 + a * SUBLANES, SUBLANES)
                if s == 0:
                    win = ext_ref[n, pl.ds(start, blk_rows), :]
                else:
                    win = sh_ref[s - 1, pl.ds(start, blk_rows), :]
                acc = acc + wb_ref[j] * win.reshape(blk_groups, SUBLANES, C)
            y_ref[n, pl.ds(pl.multiple_of(r0, SUBLANES), blk_rows), :] = acc.reshape(blk_rows, C)

        if tl == blk_rows:
            conv_rows(0)
        else:
            lax.fori_loop(0, tl // blk_rows, lambda ci, c: (conv_rows(ci * blk_rows), c)[1], 0)

        for r0 in range(0, tl, chunk):
            acc = y_ref[n, r0:r0 + chunk, :]
            mu = jnp.mean(acc, axis=-1, keepdims=True)
            xc = acc - mu
            var = jnp.mean(xc * xc, axis=-1, keepdims=True)
            ln = xc * lax.rsqrt(var + EPS) * gamma + beta
            y_ref[n, r0:r0 + chunk, :] = ln * jax.nn.sigmoid(ln) * sz_ref[n, r0:r0 + chunk, :]
    tail = ext_ref[:, lead + tl: CONV_HALO + tl, :]
    st_ref[...] = tail
    ext_ref[:, lead:CONV_HALO, :] = tail


def _conv_mixer(u, sz, buf, w_dw, b_dw, g_ln, b_ln):
    B, L, C = u.shape
    keep = CONV_SIZE - 1
    if L >= CONV_ROWS:
        nb, tl = 1, CONV_ROWS
    else:
        nb, tl = SUBLANES, L
    rows_sh = tl + CONV_HALO - SUBLANES
    tok = lambda b, i: (b, i, 0)
    vec = lambda b, i: (0, 0)
    return pl.pallas_call(
        _conv_kernel,
        grid=(B // nb, L // tl),
        in_specs=[pl.BlockSpec((nb, tl, C), tok),
                  pl.BlockSpec((nb, tl, C), tok),
                  pl.BlockSpec((nb, keep, C), lambda b, i: (b, 0, 0)),
                  pl.BlockSpec((CONV_SIZE, C), vec),
                  pl.BlockSpec((1, C), vec), pl.BlockSpec((1, C), vec), pl.BlockSpec((1, C), vec)],
        out_specs=[pl.BlockSpec((nb, tl, C), tok),
                   pl.BlockSpec((nb, keep, C), lambda b, i: (b, 0, 0))],
        out_shape=[jax.ShapeDtypeStruct((B, L, C), F32),
                   jax.ShapeDtypeStruct((B, keep, C), F32)],
        scratch_shapes=[pltpu.VMEM((nb, CONV_HALO + tl, C), F32),
                        pltpu.VMEM((SUBLANES - 1, rows_sh, C), F32),
                        pltpu.VMEM((CONV_SIZE, SUBLANES, C), F32)],
        compiler_params=_params("arbitrary", "arbitrary"),
        name="conv_mixer",
    )(u, sz, buf, w_dw, b_dw.reshape(1, C), g_ln.reshape(1, C), b_ln.reshape(1, C))


def _gla_kernel(q_ref, k_ref, v_ref, sr_ref, la_ref, s0_ref, g_ref, y_ref, sout_ref, S_ref, *, chunk):
    nb, tl, qk = q_ref.shape
    vw = v_ref.shape[-1]
    H = GLA_HEADS
    dk, dv = qk // H, vw // H
    sub = min(chunk, GLA_SUB)
    nblk = chunk // sub

    @pl.when(pl.program_id(1) == 0)
    def _():
        S_ref[...] = s0_ref[...]

    row = lax.broadcasted_iota(jnp.int32, (chunk, chunk), 0)
    col = lax.broadcasted_iota(jnp.int32, (chunk, chunk), 1)
    tril = jnp.where(row >= col, 1.0, 0.0).astype(BF16)
    row1 = lax.broadcasted_iota(jnp.int32, (chunk, 1), 0)
    sub_row = lax.broadcasted_iota(jnp.int32, (nblk, sub, 1), 1)
    ones = jnp.ones((chunk, LANES), BF16)
    assert dv % LANES == 0
    gamma = g_ref[...]
    tn = (((0,), (0,)), ((), ()))
    nt = (((1,), (1,)), ((), ()))
    halves = []
    m = chunk // 2
    while m >= sub:
        halves.append(m)
        m //= 2

    for n, c0 in [(n, c0) for n in range(nb) for c0 in range(0, tl, chunk)]:
        rows = slice(c0, c0 + chunk)
        la = la_ref[n, rows, :]
        la_hi = la.astype(BF16)
        la_lo = (la - la_hi.astype(F32)).astype(BF16)
        bc = (jnp.dot(tril, la_hi, preferred_element_type=F32)
              + jnp.dot(tril, la_lo, preferred_element_type=F32))
        btot = bc[chunk - 1:chunk, :]
        qq = q_ref[n, rows, :]
        kk = k_ref[n, rows, :]
        q0 = qq * jnp.exp(bc)
        kh = kk * jnp.exp(btot - bc)

        level_ops = []
        for m in halves:
            nb2 = chunk // (2 * m)
            bc3 = bc.reshape(nb2, 2 * m, qk)
            ref = jnp.broadcast_to(bc3[:, m - 1:m, :], (nb2, 2 * m, qk)).reshape(chunk, qk)
            e = jnp.exp(-jnp.abs(bc - ref))
            upper = ((row1 >> _log2(m)) & 1) == 1
            level_ops.append((jnp.where(upper, qq * e, 0.0), jnp.where(upper, 0.0, kk * e)))

        bc3 = bc.reshape(nblk, sub, qk)
        q3 = qq.reshape(nblk, sub, qk)
        k3 = kk.reshape(nblk, sub, qk)
        diag_terms = []
        for jp in range(sub):
            ref = jnp.broadcast_to(bc3[:, jp:jp + 1, :], (nblk, sub, qk))
            kj = jnp.broadcast_to(k3[:, jp:jp + 1, :], (nblk, sub, qk))
            wgt = jnp.where(sub_row >= jp, jnp.exp(jnp.minimum(bc3 - ref, 0.0)), 0.0)
            diag_terms.append((q3 * kj * wgt).reshape(chunk, qk))

        for h in range(H):
            ks = slice(h * dk, (h + 1) * dk)
            vs = slice(h * dv, (h + 1) * dv)
            vh = v_ref[n, rows, vs].astype(BF16)
            att = jnp.zeros((chunk, chunk), F32)
            for m, (qm, km) in zip(halves, level_ops):
                part = lax.dot_general(qm[:, ks].astype(BF16), km[:, ks].astype(BF16), nt,
                                       preferred_element_type=F32)
                if 2 * m < chunk:
                    same = (row >> _log2(2 * m)) == (col >> _log2(2 * m))
                    part = jnp.where(same, part, 0.0)
                att = att + part
            blk0 = (row >> _log2(sub)) << _log2(sub)
            for jp in range(sub):
                rs = jnp.sum(diag_terms[jp][:, ks], axis=-1, keepdims=True)
                att = att + jnp.where(col == blk0 + jp, rs, 0.0)
            S = S_ref[n, h]
            o = (jnp.dot(att.astype(BF16), vh, preferred_element_type=F32)
                 + jnp.dot(q0[:, ks].astype(BF16), S.astype(BF16), preferred_element_type=F32))
            tot = (lax.dot_general(la_hi[:, ks], ones, tn, preferred_element_type=F32)
                   + lax.dot_general(la_lo[:, ks], ones, tn, preferred_element_type=F32))
            decay = jnp.exp(jnp.concatenate([tot] * (dv // LANES), axis=1))
            S_ref[n, h] = decay * S + lax.dot_general(kh[:, ks].astype(BF16), vh, tn, preferred_element_type=F32)
            ms = jnp.mean(o * o, axis=-1, keepdims=True)
            y_ref[n, rows, vs] = o * lax.rsqrt(ms + EPS) * gamma * sr_ref[n, rows, vs]

    sout_ref[...] = S_ref[...]


def _gla_mixer(q, k, v, sr, la, s0, g_norm):
    B, L, qk = q.shape
    vw = v.shape[-1]
    H = GLA_HEADS
    dk, dv = qk // H, vw // H
    tl = min(L, GLA_ROWS)
    chunk = min(L, GLA_CHUNK)
    nb = 1 if L >= GLA_ROWS else GLA_SEQS
    assert B % nb == 0
    tok = lambda b, i: (b, i, 0)
    st = lambda b, i: (b, 0, 0, 0)
    return pl.pallas_call(
        functools.partial(_gla_kernel, chunk=chunk),
        grid=(B // nb, L // tl),
        in_specs=[pl.BlockSpec((nb, tl, qk), tok), pl.BlockSpec((nb, tl, qk), tok),
                  pl.BlockSpec((nb, tl, vw), tok), pl.BlockSpec((nb, tl, vw), tok),
                  pl.BlockSpec((nb, tl, qk), tok),
                  pl.BlockSpec((nb, H, dk, dv), st),
                  pl.BlockSpec((1, dv), lambda b, i: (0, 0))],
        out_specs=[pl.BlockSpec((nb, tl, vw), tok),
                   pl.BlockSpec((nb, H, dk, dv), st)],
        out_shape=[jax.ShapeDtypeStruct((B, L, vw), F32),
                   jax.ShapeDtypeStruct((B, H, dk, dv), F32)],
        scratch_shapes=[pltpu.VMEM((nb, H, dk, dv), F32)],
        compiler_params=_params("arbitrary", "arbitrary"),
        name="gla_mixer",
    )(q, k, v, sr, la, s0, g_norm.reshape(1, dv))


def _att_block(q_ref, kp_ref, kc_ref, vp_ref, vc_ref, o_ref, l_ref, first):
    blk = q_ref.shape[1]
    row = lax.broadcasted_iota(jnp.int32, (blk, 2 * blk), 0)
    col = lax.broadcasted_iota(jnp.int32, (blk, 2 * blk), 1)
    valid = (col >= row) & (col <= row + blk) & jnp.logical_or(col >= blk, jnp.logical_not(first))
    q = q_ref[0]
    kk = jnp.concatenate([kp_ref[0], kc_ref[0]], axis=0)
    vv = jnp.concatenate([vp_ref[0], vc_ref[0]], axis=0)
    nt = (((1,), (1,)), ((), ()))
    heads_per_tile = LANES // ATT_HEAD_DIM
    lane = lax.broadcasted_iota(jnp.int32, (1, LANES), 1)
    for t in range(q.shape[1] // LANES):
        ls = slice(t * LANES, (t + 1) * LANES)
        qp, kp, vp = q[:, ls], kk[:, ls], vv[:, ls]
        o_tile = jnp.zeros((blk, LANES), F32)
        l_tile = jnp.zeros((blk, LANES), F32)
        for hh in range(heads_per_tile):
            mine = (lane >= hh * ATT_HEAD_DIM) & (lane < (hh + 1) * ATT_HEAD_DIM)
            qm = qp * jnp.where(mine, 1.0, 0.0).astype(qp.dtype)
            s = lax.dot_general(qm, kp, nt, preferred_element_type=F32)
            s = jnp.where(valid, s, NEG_INF)
            m = jnp.max(s, axis=-1, keepdims=True)
            p = jnp.exp(s - m)
            den = jnp.sum(p, axis=-1, keepdims=True)
            o = jnp.dot(p.astype(BF16), vp, preferred_element_type=F32) / den
            o_tile = jnp.where(mine, o, o_tile)
            l_tile = jnp.where(mine, m + jnp.log(den), l_tile)
        o_ref[0, :, ls] = o_tile.astype(o_ref.dtype)
        l_ref[0, :, ls] = l_tile


def _att_sample_heads(q_ref, k_ref, v_ref, caches, y_ref, nbuf, heads):
    G, H, n_new, Dh = q_ref.shape[1:]
    nt = (((1,), (1,)), ((), ()))
    biases = []
    for g, (W, d) in enumerate(ATT_GROUPS):
        nb = nbuf[g]
        i = lax.broadcasted_iota(jnp.int32, (n_new, nb), 0)
        c = lax.broadcasted_iota(jnp.int32, (n_new, nb), 1)
        ok = ((c & (d - 1)) == (i & (d - 1))) & (c >= i - (W - nb))
        i2 = lax.broadcasted_iota(jnp.int32, (n_new, n_new), 0)
        j2 = lax.broadcasted_iota(jnp.int32, (n_new, n_new), 1)
        ok2 = (j2 <= i2) & (((i2 - j2) & (d - 1)) == 0)
        biases.append((jnp.where(ok, 0.0, NEG_INF).astype(F32), jnp.where(ok2, 0.0, NEG_INF).astype(F32)))
    for h in heads:
        scores = []
        for g in range(G):
            qh = q_ref[0, g, h].astype(BF16)
            kt = caches[g][0][0, h].astype(BF16)
            scores.append(jnp.dot(qh, kt, preferred_element_type=F32) + biases[g][0])
            kn = k_ref[0, g, h].astype(BF16)
            scores.append(lax.dot_general(qh, kn, nt, preferred_element_type=F32) + biases[g][1])
        m = functools.reduce(jnp.maximum, [jnp.max(s, axis=-1, keepdims=True) for s in scores])
        acc = jnp.zeros((n_new, Dh), F32)
        den = jnp.zeros((n_new, 1), F32)
        for g in range(G):
            pc = jnp.exp(scores[2 * g] - m).astype(BF16)
            pn = jnp.exp(scores[2 * g + 1] - m).astype(BF16)
            den = den + jnp.sum(pc.astype(F32), axis=-1, keepdims=True) + jnp.sum(pn.astype(F32), axis=-1, keepdims=True)
            vt = caches[g][1][0, h].astype(BF16)
            acc = acc + lax.dot_general(pc, vt, nt, preferred_element_type=F32)
            acc = acc + jnp.dot(pn, v_ref[0, g, h].astype(BF16), preferred_element_type=F32)
        y_ref[0, h] = acc / den


def _att_kernel(*refs, nbuf, steps_g, blocks_per_class, head_phases):
    G = len(ATT_GROUPS)
    prompt_in = [refs[5 * g:5 * g + 5] for g in range(G)]
    q_ref, k_ref, v_ref = refs[5 * G:5 * G + 3]
    cache_refs = refs[5 * G + 3:7 * G + 3]
    outs = refs[7 * G + 3:]
    s = pl.program_id(0)
    for g in range(G):
        @pl.when(s // steps_g == g)
        def _(g=g):
            first = ((s - g * steps_g) % blocks_per_class[g]) == 0
            _att_block(*prompt_in[g], outs[2 * g], outs[2 * g + 1], first)

    caches = [(cache_refs[2 * g], cache_refs[2 * g + 1]) for g in range(G)]
    for phase, heads in enumerate(head_phases):
        @pl.when(s % len(head_phases) == phase)
        def _(heads=heads):
            _att_sample_heads(q_ref, k_ref, v_ref, caches, outs[2 * G], nbuf, heads)


def _attention(qkv, qs, ks, vs, caches_k, caches_v):
    G = len(ATT_GROUPS)
    blk, width = ATT_BLOCK, ATT_WIDTH
    B = qkv[0][0].shape[0]
    L = qkv[0][0].shape[1] * ATT_GROUPS[0][1]
    steps_g = B * L // blk
    Bs, _, H, n_new, Dh = qs.shape
    total = G * steps_g
    assert total % Bs == 0
    per = total // Bs
    head_phases = tuple(tuple(range(p * H // per, (p + 1) * H // per)) for p in range(per))

    in_specs, args, out_specs, out_shapes, blocks_per_class = [], [], [], [], []
    for g, (W, d) in enumerate(ATT_GROUPS):
        Ld = L // d
        assert min(W, L) // d == blk and Ld % blk == 0
        n_i = Ld // blk
        blocks_per_class.append(n_i)

        def decode(s, g=g, d=d, n_i=n_i):
            t = jnp.clip(s - g * steps_g, 0, steps_g - 1)
            br = t // n_i
            return br // d, br % d, t % n_i

        def cur(s, decode=decode):
            b, r, i = decode(s)
            return b, i, r

        def prev(s, decode=decode):
            b, r, i = decode(s)
            return b, jnp.maximum(i - 1, 0), r

        q, k, v = qkv[g]
        in_specs += [pl.BlockSpec((1, blk, width), cur),
                     pl.BlockSpec((1, blk, width), prev), pl.BlockSpec((1, blk, width), cur),
                     pl.BlockSpec((1, blk, width), prev), pl.BlockSpec((1, blk, width), cur)]
        args += [q, k, k, v, v]
        out_specs += [pl.BlockSpec((1, blk, width), cur), pl.BlockSpec((1, blk, width), cur)]
        out_shapes += [jax.ShapeDtypeStruct((B, Ld, d * width), BF16),
                       jax.ShapeDtypeStruct((B, Ld, d * width), F32)]

    nbuf = []
    seq5 = lambda s: (s // per, 0, 0, 0, 0)
    seq4 = lambda s: (s // per, 0, 0, 0)
    in_specs += [pl.BlockSpec((1, G, H, n_new, Dh), seq5)] * 3
    args += [qs, ks, vs]
    for g, (W, d) in enumerate(ATT_GROUPS):
        nb = caches_k[g].shape[-1]
        assert d & (d - 1) == 0 and (W - nb) % d == 0
        nbuf.append(nb)
        in_specs += [pl.BlockSpec((1, H, Dh, nb), seq4)] * 2
        args += [caches_k[g], caches_v[g]]
    out_specs.append(pl.BlockSpec((1, H, n_new, Dh), seq4))
    out_shapes.append(jax.ShapeDtypeStruct((Bs, H, n_new, Dh), F32))

    outs = pl.pallas_call(
        functools.partial(_att_kernel, nbuf=tuple(nbuf), steps_g=steps_g,
                          blocks_per_class=tuple(blocks_per_class), head_phases=head_phases),
        grid=(total,),
        in_specs=in_specs,
        out_specs=out_specs,
        out_shape=out_shapes,
        compiler_params=_params("arbitrary"),
        name="attention",
    )(*args)
    return [tuple(outs[2 * g:2 * g + 2]) for g in range(G)], outs[2 * G]


def _rope_tables(pos):
    half = ATT_HEAD_DIM // 2
    inv = ROPE_THETA ** (-jnp.arange(half, dtype=F32) / half)
    ang = pos.astype(F32)[:, None] * inv[None, :]
    cos, sin = jnp.cos(ang), jnp.sin(ang)
    reps = LANES // ATT_HEAD_DIM
    cos_t = jnp.tile(jnp.concatenate([cos, cos], axis=-1), (1, reps))
    sin_t = jnp.tile(jnp.concatenate([-sin, sin], axis=-1), (1, reps))
    return cos_t, sin_t


def kernel(x_prompt, x_sample, state_conv, state_gla, cache_k_g0, cache_v_g0, cache_k_g1, cache_v_g1, cache_k_g2, cache_v_g2, c_prompt, c_sample, w_ada, b_ada, g_pre, g_post, w_conv_in, w_dw, b_dw, g_conv_ln, b_conv_ln, w_conv_out, w_gla_in, w_gla_a1, w_gla_a2, b_gla_a, g_gla_norm, w_gla_out, w_att_in, w_att_out):
    depth = w_ada.shape[0]
    D = x_prompt.shape[-1]
    G = len(ATT_GROUPS)
    xs = [x_prompt, x_sample]
    nseq = [x_prompt.shape[0], x_sample.shape[0]]
    caches_k = (cache_k_g0, cache_k_g1, cache_k_g2)
    caches_v = (cache_v_g0, cache_v_g1, cache_v_g2)

    n_c = nseq[0] + nseq[1]
    pad = (-n_c) % SUBLANES
    c_all = jnp.concatenate([c_prompt, c_sample, jnp.zeros((pad, D), F32)], axis=0)
    mod_all = _modulation(c_all, w_ada, b_ada)
    mods = [[mod_all[l, :nseq[0]].reshape(nseq[0], 1, 3 * D),
             mod_all[l, nseq[0]:n_c].reshape(nseq[1], 1, 3 * D)] for l in range(depth)]

    pos = [jnp.arange(x_prompt.shape[1]), PAST_LEN + jnp.arange(x_sample.shape[1])]
    rope = [_rope_tables(p) for p in pos]

    conv_new, gla_new = ([], []), ([], [])
    k_new = ([[] for _ in range(G)], [[] for _ in range(G)])
    v_new = ([[] for _ in range(G)], [[] for _ in range(G)])
    no_extra = lambda nb, tl: []
    heads = (ATT_HEADS, ATT_HEAD_DIM)

    for l in range(depth):
        kind, j = l % N_MIXERS, l // N_MIXERS
        for grp in range(2):
            x, mod = xs[grp], mods[l][grp]
            B, L, _ = x.shape
            if kind == 0:
                C = w_dw.shape[-1]
                u, sz = _pre_call(_pre_conv_kernel, "pre_conv", x, mod, g_pre[l],
                                  [w_conv_in[j].astype(BF16)], no_extra, [], [C, C])
                buf = jnp.zeros((B, CONV_SIZE - 1, C), F32) if grp == 0 else state_conv[j]
                y, st = _conv_mixer(u, sz, buf, w_dw[j], b_dw[j], g_conv_ln[j], b_conv_ln[j])
                conv_new[grp].append(st)
                xs[grp] = _post_call(_post_kernel, "post_conv", [y], w_conv_out[j].astype(BF16), g_post[l], mod, x)
            elif kind == 1:
                qk = w_gla_a2.shape[-1]
                vw = w_gla_out.shape[1]
                wa1 = jnp.zeros((D, LANES), F32).at[:, :GLA_RANK].set(w_gla_a1[j]).astype(BF16)
                wa2 = jnp.zeros((LANES, qk), F32).at[:GLA_RANK].set(w_gla_a2[j]).astype(BF16)
                q, k, v, sr, la = _pre_call(
                    _pre_gla_kernel, "pre_gla", x, mod, g_pre[l],
                    [w_gla_in[j].astype(BF16), wa1, wa2, b_gla_a[j].reshape(1, qk)], no_extra, [],
                    [qk, qk, vw, vw, qk])
                dk, dv = qk // GLA_HEADS, vw // GLA_HEADS
                s0 = jnp.zeros((B, GLA_HEADS, dk, dv), F32) if grp == 0 else state_gla[j]
                y, st = _gla_mixer(q, k, v, sr, la, s0, g_gla_norm[j])
                gla_new[grp].append(st)
                xs[grp] = _post_call(_post_kernel, "post_gla", [y], w_gla_out[j].astype(BF16), g_post[l], mod, x)
            elif grp == 0:
                xp, xsmp = xs
                mod_p, mod_s = mods[l]
                Bp, Lp, _ = xp.shape
                Bs, Ls, _ = xsmp.shape
                w_out = w_att_out[j].astype(BF16)
                qkv, sz_p, tails = _pre_att_prompt(xp, mod_p, g_pre[l], w_att_in[j], pos[0])
                gw = G * ATT_WIDTH
                cos_t, sin_t = rope[1]
                rope_specs = lambda nb, tl: [pl.BlockSpec((tl, LANES), lambda b, i: (i, 0))] * 2
                q, k, v, sz_s = _pre_call(_pre_att_kernel, "pre_att", xsmp, mod_s, g_pre[l],
                                          [w_att_in[j].astype(BF16)], rope_specs, [cos_t, sin_t],
                                          [gw, gw, gw, ATT_WIDTH])
                ck = [jnp.transpose(c[j], (0, 2, 3, 1)) for c in caches_k]
                cv = [jnp.transpose(c[j], (0, 2, 3, 1)) for c in caches_v]
                q5, k5, v5 = (t.reshape(Bs, Ls, G, *heads) for t in (q, k, v))
                qt, kt, vt = (jnp.transpose(t, (0, 2, 3, 1, 4)) for t in (q5, k5, v5))
                att_outs, y = _attention(qkv, qt, kt, vt, ck, cv)
                y = jnp.transpose(y, (0, 2, 1, 3)).reshape(Bs, Ls, ATT_WIDTH)
                xs[0] = _post_att_prompt(att_outs, sz_p, w_out, g_post[l], mod_p, xp)
                xs[1] = _post_call(_post_gate_kernel, "post_att_sample", [y, sz_s], w_out, g_post[l], mod_s, xsmp)
                for g, (W, d) in enumerate(ATT_GROUPS):
                    keep = min(W, Lp)
                    for store, t in ((k_new, tails[g][0]), (v_new, tails[g][1])):
                        t = t[:, t.shape[1] - keep // d:, :]
                        store[0][g].append(t.reshape(Bp, keep, *heads))
                    k_new[1][g].append(k5[:, :, g])
                    v_new[1][g].append(v5[:, :, g])

    outs = [xs[0], xs[1], jnp.stack(conv_new[0]), jnp.stack(conv_new[1]),
            jnp.stack(gla_new[0]), jnp.stack(gla_new[1])]
    for grp in range(2):
        for g in range(G):
            outs += [jnp.stack(k_new[grp][g]), jnp.stack(v_new[grp][g])]
    return tuple(outs)
```

```python
import functools

import jax
import jax.numpy as jnp
from jax import lax
from jax.experimental import pallas as pl
from jax.experimental.pallas import tpu as pltpu

F32 = jnp.float32
BF16 = jnp.bfloat16
HIGHEST = lax.Precision.HIGHEST

PAST_LEN = 2048
N_MIXERS = 3
CONV_SIZE = 31
GLA_HEADS = 4
GLA_RANK = 16
GLA_TAU = 16.0
ATT_GROUPS = ((128, 1), (512, 4), (2048, 16))
ATT_HEADS = 8
ATT_HEAD_DIM = 64
ATT_WIDTH = ATT_HEADS * ATT_HEAD_DIM
ROPE_THETA = 10000.0
EPS = 1e-6
NEG_INF = -1e30

LANES = 128
SUBLANES = 8
VMEM_LIMIT_BYTES = 56 * 1024 * 1024

PROJ_ROWS = 512
CONV_ROWS = 256
CONV_CHUNK = 32
CONV_BLOCK_ROWS = 32
CONV_HALO = 32
GLA_ROWS = 256
GLA_CHUNK = 64
GLA_SUB = 8
GLA_SEQS = 4
ATT_BLOCK = 128
ATT_PROJ_ROWS = 256


def _params(*sem):
    return pltpu.CompilerParams(dimension_semantics=sem, vmem_limit_bytes=VMEM_LIMIT_BYTES)


def _log2(n):
    assert n > 0 and n & (n - 1) == 0, n
    return n.bit_length() - 1


def _token_tiling(B, L, rows=PROJ_ROWS):
    if L >= rows:
        return 1, rows
    nb = max(1, rows // L)
    while B % nb:
        nb //= 2
    return nb, L


def _mod_kernel(c_ref, w_ref, b_ref, o_ref):
    o_ref[0] = jnp.dot(c_ref[...], w_ref[0], precision=HIGHEST, preferred_element_type=F32) + b_ref[0]


def _modulation(c_all, w_ada, b_ada):
    depth, D, N = w_ada.shape
    R = c_all.shape[0]
    tn = 1024
    return pl.pallas_call(
        _mod_kernel,
        grid=(depth, N // tn),
        in_specs=[pl.BlockSpec((R, D), lambda l, j: (0, 0)),
                  pl.BlockSpec((1, D, tn), lambda l, j: (l, 0, j)),
                  pl.BlockSpec((1, 1, tn), lambda l, j: (l, 0, j))],
        out_specs=pl.BlockSpec((1, R, tn), lambda l, j: (l, 0, j)),
        out_shape=jax.ShapeDtypeStruct((depth, R, N), F32),
        compiler_params=_params("arbitrary", "arbitrary"),
        name="adaln_mod",
    )(c_all, w_ada, b_ada.reshape(depth, 1, N))


def _modulated_norm(x_ref, mod_ref, g_ref):
    x = x_ref[...]
    nb, tl, D = x.shape
    ms = jnp.mean(x * x, axis=-1, keepdims=True)
    y = x * lax.rsqrt(ms + EPS) * g_ref[...]
    shift = mod_ref[:, :, 0:D]
    scale = mod_ref[:, :, D:2 * D]
    h = y * (1.0 + scale) + shift
    return h.reshape(nb * tl, D).astype(BF16)


def _pre_conv_kernel(x_ref, mod_ref, g_ref, w_ref, u_ref, sz_ref):
    h = _modulated_norm(x_ref, mod_ref, g_ref)
    res = jnp.dot(h, w_ref[...], preferred_element_type=F32)
    C = u_ref.shape[-1]
    u = res[:, :C] * jax.nn.sigmoid(res[:, C:2 * C])
    z = res[:, 2 * C:]
    u_ref[...] = u.reshape(u_ref.shape)
    sz_ref[...] = (z * jax.nn.sigmoid(z)).reshape(sz_ref.shape)


def _pre_gla_kernel(x_ref, mod_ref, g_ref, w_ref, wa1_ref, wa2_ref, ba_ref,
                    q_ref, k_ref, v_ref, sr_ref, la_ref):
    h = _modulated_norm(x_ref, mod_ref, g_ref)
    res = jnp.dot(h, w_ref[...], preferred_element_type=F32)
    qk = q_ref.shape[-1]
    vw = v_ref.shape[-1]
    dk = qk // GLA_HEADS
    q_ref[...] = (res[:, :qk] * (dk ** -0.5)).reshape(q_ref.shape)
    k_ref[...] = res[:, qk:2 * qk].reshape(k_ref.shape)
    v_ref[...] = res[:, 2 * qk:2 * qk + vw].reshape(v_ref.shape)
    r = res[:, 2 * qk + vw:]
    sr_ref[...] = (r * jax.nn.sigmoid(r)).reshape(sr_ref.shape)
    low = jnp.dot(h, wa1_ref[...], preferred_element_type=F32)
    zg = jnp.dot(low.astype(BF16), wa2_ref[...], preferred_element_type=F32) + ba_ref[...]
    log_sig = jnp.minimum(zg, 0.0) - jnp.log(1.0 + jnp.exp(-jnp.abs(zg)))
    la_ref[...] = (log_sig * (1.0 / GLA_TAU)).reshape(la_ref.shape)


def _swap_halves(x):
    half = ATT_HEAD_DIM // 2
    lane = lax.broadcasted_iota(jnp.int32, x.shape, 1)
    lower = (lane % ATT_HEAD_DIM) < half
    return jnp.where(lower, pltpu.roll(x, LANES - half, 1), pltpu.roll(x, half, 1))


def _rope_cols(res, col0, width, cos, sin, mult):
    pieces = []
    for c in range(width // LANES):
        xc = res[:, col0 + c * LANES: col0 + (c + 1) * LANES]
        val = xc * cos + _swap_halves(xc) * sin
        pieces.append(val * mult if mult != 1.0 else val)
    return pieces


def _pre_att_kernel(x_ref, mod_ref, g_ref, w_ref, cos_ref, sin_ref, q_ref, k_ref, v_ref, sz_ref):
    h = _modulated_norm(x_ref, mod_ref, g_ref)
    res = jnp.dot(h, w_ref[...], preferred_element_type=F32)
    nb, tl, gw = q_ref.shape
    cos = jnp.broadcast_to(cos_ref[...], (nb, tl, LANES)).reshape(nb * tl, LANES)
    sin = jnp.broadcast_to(sin_ref[...], (nb, tl, LANES)).reshape(nb * tl, LANES)
    for c, val in enumerate(_rope_cols(res, 0, gw, cos, sin, ATT_HEAD_DIM ** -0.5)):
        q_ref[:, :, c * LANES:(c + 1) * LANES] = val.reshape(nb, tl, LANES)
    for c, val in enumerate(_rope_cols(res, gw, gw, cos, sin, 1.0)):
        k_ref[:, :, c * LANES:(c + 1) * LANES] = val.reshape(nb, tl, LANES)
    v_ref[...] = res[:, 2 * gw:3 * gw].reshape(v_ref.shape)
    z = res[:, 3 * gw:]
    sz_ref[...] = (z * jax.nn.sigmoid(z)).reshape(sz_ref.shape)


def _pre_call(kernel, name, x, mod, g_pre, weights, extra_specs, extras, out_widths):
    B, L, D = x.shape
    nb, tl = _token_tiling(B, L, PROJ_ROWS if sum(out_widths) <= 7 * D // 2 else PROJ_ROWS // 2)
    tok = lambda b, i: (b, i, 0)
    in_specs = [pl.BlockSpec((nb, tl, D), tok),
                pl.BlockSpec((nb, 1, mod.shape[-1]), lambda b, i: (b, 0, 0)),
                pl.BlockSpec((1, D), lambda b, i: (0, 0))]
    in_specs += [pl.BlockSpec(w.shape, lambda b, i: (0, 0)) for w in weights]
    in_specs += extra_specs(nb, tl)
    return pl.pallas_call(
        kernel,
        grid=(B // nb, L // tl),
        in_specs=in_specs,
        out_specs=[pl.BlockSpec((nb, tl, n), tok) for n in out_widths],
        out_shape=[jax.ShapeDtypeStruct((B, L, n), F32) for n in out_widths],
        compiler_params=_params("arbitrary", "arbitrary"),
        name=name,
    )(x, mod, g_pre.reshape(1, D), *weights, *extras)


def _class_order_source(idx, per, d):
    return (idx & (per - 1)) * d + (idx >> _log2(per))


def _pre_att_prompt_kernel(x_ref, mod_ref, g_ref, w0_ref, w1_ref, w2_ref, wz_ref,
                           c0_ref, s0_ref, c1_ref, s1_ref, c2_ref, s2_ref,
                           q0_ref, k0_ref, v0_ref, q1_ref, k1_ref, v1_ref, q2_ref, k2_ref, v2_ref, sz_ref,
                           kt0_ref, vt0_ref, kt1_ref, vt1_ref, kt2_ref, vt2_ref, *, first_tail):
    h = _modulated_norm(x_ref, mod_ref, g_ref)
    tm = h.shape[0]
    width = sz_ref.shape[-1]
    z = jnp.dot(h, wz_ref[...], preferred_element_type=F32)
    sz_ref[0] = z * jax.nn.sigmoid(z)
    step = pl.program_id(1)
    groups = ((w0_ref, c0_ref, s0_ref, q0_ref, k0_ref, v0_ref, kt0_ref, vt0_ref),
              (w1_ref, c1_ref, s1_ref, q1_ref, k1_ref, v1_ref, kt1_ref, vt1_ref),
              (w2_ref, c2_ref, s2_ref, q2_ref, k2_ref, v2_ref, kt2_ref, vt2_ref))
    for g, (w_ref, cos_ref, sin_ref, q_ref, k_ref, v_ref, kt_ref, vt_ref) in enumerate(groups):
        d = ATT_GROUPS[g][1]
        per = tm // d
        if d > 1:
            dst = lax.broadcasted_iota(jnp.int32, (tm, tm), 0)
            src = lax.broadcasted_iota(jnp.int32, (tm, tm), 1)
            perm = jnp.where(src == _class_order_source(dst, per, d), 1.0, 0.0).astype(BF16)
            hg = jnp.dot(perm, h, preferred_element_type=F32).astype(BF16)
        else:
            hg = h
        res = jnp.dot(hg, w_ref[...], preferred_element_type=F32)
        cos, sin = cos_ref[...], sin_ref[...]
        qf = jnp.concatenate(_rope_cols(res, 0, width, cos, sin, ATT_HEAD_DIM ** -0.5), axis=1)
        kf = jnp.concatenate(_rope_cols(res, width, width, cos, sin, 1.0), axis=1)
        vf = res[:, 2 * width:]
        qb, kb, vb = qf.astype(BF16), kf.astype(BF16), vf.astype(BF16)
        for r in range(d):
            rs = slice(r * per, (r + 1) * per)
            cs = slice(r * width, (r + 1) * width)
            q_ref[0, :, cs] = qb[rs]
            k_ref[0, :, cs] = kb[rs]
            v_ref[0, :, cs] = vb[rs]

        @pl.when(step >= first_tail[g])
        def _(kf=kf, vf=vf, kt_ref=kt_ref, vt_ref=vt_ref, d=d, per=per):
            for r in range(d):
                rs = slice(r * per, (r + 1) * per)
                cs = slice(r * width, (r + 1) * width)
                kt_ref[0, :, cs] = kf[rs]
                vt_ref[0, :, cs] = vf[rs]


def _pre_att_prompt(x, mod, g_pre, w_att_in, pos):
    B, L, D = x.shape
    G = len(ATT_GROUPS)
    width = ATT_WIDTH
    gw = G * width
    tm = ATT_PROJ_ROWS
    n_steps = L // tm
    w = w_att_in.astype(BF16)
    weights, tables, first_tail = [], [], []
    out_specs, out_shapes = [], []
    tail_specs, tail_shapes = [], []
    for g, (W, d) in enumerate(ATT_GROUPS):
        gs = slice(g * width, (g + 1) * width)
        weights.append(jnp.concatenate([w[:, gs], w[:, gw:][:, gs], w[:, 2 * gw:][:, gs]], axis=1))
        per = tm // d
        assert tm % d == 0 and per % 16 == 0 and L % tm == 0
        a = jnp.arange(tm)
        order = (jnp.arange(n_steps)[:, None] * tm + _class_order_source(a, per, d)[None, :]).reshape(-1)
        tables += list(_rope_tables(pos[order]))
        for _ in range(3):
            out_specs.append(pl.BlockSpec((1, per, d * width), lambda b, i: (b, i, 0)))
            out_shapes.append(jax.ShapeDtypeStruct((B, L // d, d * width), BF16))
        tail_rows = max(min(W, L) // d, per)
        n_tail = tail_rows // per
        first_tail.append(n_steps - n_tail)
        for _ in range(2):
            tail_specs.append(pl.BlockSpec(
                (1, per, d * width), lambda b, i, first=n_steps - n_tail: (b, jnp.maximum(i - first, 0), 0)))
            tail_shapes.append(jax.ShapeDtypeStruct((B, tail_rows, d * width), F32))
    weights.append(w[:, 3 * gw:])
    const = lambda b, i: (0, 0)
    in_specs = [pl.BlockSpec((1, tm, D), lambda b, i: (b, i, 0)),
                pl.BlockSpec((1, 1, mod.shape[-1]), lambda b, i: (b, 0, 0)),
                pl.BlockSpec((1, D), const)]
    in_specs += [pl.BlockSpec(wt.shape, const) for wt in weights]
    in_specs += [pl.BlockSpec((tm, LANES), lambda b, i: (i, 0)) for _ in tables]
    sz_spec = pl.BlockSpec((1, tm, width), lambda b, i: (b, i, 0))
    outs = pl.pallas_call(
        functools.partial(_pre_att_prompt_kernel, first_tail=tuple(first_tail)),
        grid=(B, n_steps),
        in_specs=in_specs,
        out_specs=out_specs + [sz_spec] + tail_specs,
        out_shape=out_shapes + [jax.ShapeDtypeStruct((B, L, width), F32)] + tail_shapes,
        compiler_params=_params("arbitrary", "arbitrary"),
        name="pre_att_prompt",
    )(x, mod, g_pre.reshape(1, D), *weights, *tables)
    qkv = [tuple(outs[3 * g:3 * g + 3]) for g in range(G)]
    sz = outs[3 * G]
    tails = [tuple(outs[3 * G + 1 + 2 * g: 3 * G + 3 + 2 * g]) for g in range(G)]
    return qkv, sz, tails


def _gated_residual(y_bf16, w_ref, g_ref, mod_ref, x_ref, o_ref):
    nb, tl, D = x_ref.shape
    o = jnp.dot(y_bf16, w_ref[...], preferred_element_type=F32)
    ms = jnp.mean(o * o, axis=-1, keepdims=True)
    o = (o * lax.rsqrt(ms + EPS) * g_ref[...]).reshape(nb, tl, D)
    gate = mod_ref[:, :, 2 * D:3 * D]
    o_ref[...] = x_ref[...] + gate * o


def _post_kernel(y_ref, w_ref, g_ref, mod_ref, x_ref, o_ref):
    nb, tl, K = y_ref.shape
    _gated_residual(y_ref[...].reshape(nb * tl, K).astype(BF16), w_ref, g_ref, mod_ref, x_ref, o_ref)


def _post_gate_kernel(y_ref, sz_ref, w_ref, g_ref, mod_ref, x_ref, o_ref):
    nb, tl, K = y_ref.shape
    y = y_ref[...] * sz_ref[...]
    _gated_residual(y.reshape(nb * tl, K).astype(BF16), w_ref, g_ref, mod_ref, x_ref, o_ref)


def _post_call(kernel, name, ys, w_out, g_post, mod, x):
    B, L, D = x.shape
    K = w_out.shape[0]
    nb, tl = _token_tiling(B, L)
    tok = lambda b, i: (b, i, 0)
    in_specs = [pl.BlockSpec((nb, tl, K), tok) for _ in ys]
    in_specs += [pl.BlockSpec((K, D), lambda b, i: (0, 0)),
                 pl.BlockSpec((1, D), lambda b, i: (0, 0)),
                 pl.BlockSpec((nb, 1, mod.shape[-1]), lambda b, i: (b, 0, 0)),
                 pl.BlockSpec((nb, tl, D), tok)]
    return pl.pallas_call(
        kernel,
        grid=(B // nb, L // tl),
        in_specs=in_specs,
        out_specs=pl.BlockSpec((nb, tl, D), tok),
        out_shape=jax.ShapeDtypeStruct((B, L, D), F32),
        compiler_params=_params("arbitrary", "arbitrary"),
        name=name,
    )(*ys, w_out, g_post.reshape(1, D), mod, x)


def _class_rows(ref, d):
    width = ref.shape[-1] // d
    return jnp.concatenate([ref[0, :, r * width:(r + 1) * width] for r in range(d)], axis=0)


def _post_att_kernel(o0_ref, l0_ref, o1_ref, l1_ref, o2_ref, l2_ref, sz_ref, w_ref, g_ref, mod_ref, x_ref, o_ref):
    _, tm, K = sz_ref.shape
    outs, lses = [], []
    for g, (o_g_ref, l_g_ref) in enumerate(((o0_ref, l0_ref), (o1_ref, l1_ref), (o2_ref, l2_ref))):
        d = ATT_GROUPS[g][1]
        if d == 1:
            outs.append(o_g_ref[0].astype(F32))
            lses.append(l_g_ref[0])
            continue
        per = tm // d
        dst = lax.broadcasted_iota(jnp.int32, (tm, tm), 0)
        src = lax.broadcasted_iota(jnp.int32, (tm, tm), 1)
        back = jnp.where(dst == _class_order_source(src, per, d), 1.0, 0.0).astype(BF16)
        outs.append(jnp.dot(back, _class_rows(o_g_ref, d), preferred_element_type=F32))
        lse = _class_rows(l_g_ref, d)
        hi = lse.astype(BF16)
        rest = lse - hi.astype(F32)
        mid = rest.astype(BF16)
        lo = (rest - mid.astype(F32)).astype(BF16)
        lses.append(jnp.dot(back, hi, preferred_element_type=F32)
                    + jnp.dot(back, mid, preferred_element_type=F32)
                    + jnp.dot(back, lo, preferred_element_type=F32))
    l0, l1, l2 = lses
    m = jnp.maximum(jnp.maximum(l0, l1), l2)
    e0, e1, e2 = jnp.exp(l0 - m), jnp.exp(l1 - m), jnp.exp(l2 - m)
    att = (e0 * outs[0] + e1 * outs[1] + e2 * outs[2]) / (e0 + e1 + e2)
    y = att * sz_ref[0]
    _gated_residual(y.astype(BF16), w_ref, g_ref, mod_ref, x_ref, o_ref)


def _post_att_prompt(att_outs, sz, w_out, g_post, mod, x):
    B, L, D = x.shape
    K = w_out.shape[0]
    tm = ATT_PROJ_ROWS
    tok = lambda b, i: (b, i, 0)
    in_specs, args = [], []
    for g, (W, d) in enumerate(ATT_GROUPS):
        assert tm % (16 * d) == 0
        for t in att_outs[g]:
            in_specs.append(pl.BlockSpec((1, tm // d, d * K), tok))
            args.append(t)
    in_specs += [pl.BlockSpec((1, tm, K), tok),
                 pl.BlockSpec((K, D), lambda b, i: (0, 0)),
                 pl.BlockSpec((1, D), lambda b, i: (0, 0)),
                 pl.BlockSpec((1, 1, mod.shape[-1]), lambda b, i: (b, 0, 0)),
                 pl.BlockSpec((1, tm, D), tok)]
    return pl.pallas_call(
        _post_att_kernel,
        grid=(B, L // tm),
        in_specs=in_specs,
        out_specs=pl.BlockSpec((1, tm, D), tok),
        out_shape=jax.ShapeDtypeStruct((B, L, D), F32),
        compiler_params=_params("arbitrary", "arbitrary"),
        name="post_att_prompt",
    )(*args, sz, w_out, g_post.reshape(1, D), mod, x)


def _conv_kernel(u_ref, sz_ref, buf_ref, w_ref, b_ref, g_ref, bl_ref, y_ref, st_ref, ext_ref, sh_ref, wb_ref):
    nb, tl, C = u_ref.shape
    keep = CONV_SIZE - 1
    lead = CONV_HALO - keep
    rows_sh = sh_ref.shape[1]

    @pl.when(pl.program_id(1) == 0)
    def _():
        ext_ref[:, lead:CONV_HALO, :] = buf_ref[...]
        for j in range(CONV_SIZE):
            wb_ref[j] = jnp.broadcast_to(w_ref[j:j + 1, :], (SUBLANES, C))

    ext_ref[:, CONV_HALO:CONV_HALO + tl, :] = u_ref[...]
    chunk = min(tl, CONV_CHUNK)
    blk_rows = min(tl, CONV_BLOCK_ROWS)
    blk_groups = blk_rows // SUBLANES
    bias = b_ref[...]
    gamma = g_ref[...]
    beta = bl_ref[...]
    for n in range(nb):
        for s in range(1, SUBLANES):
            sh_ref[s - 1] = ext_ref[n, s:s + rows_sh, :]

        def conv_rows(r0):
            acc = jnp.broadcast_to(bias, (blk_groups, SUBLANES, C))
            for j in range(CONV_SIZE):
                a, s = divmod(lead + j, SUBLANES)
                start = pl.multiple_of(r0 + a * SUBLANES, SUBLANES)
                if s == 0:
                    win = ext_ref[n, pl.ds(start, blk_rows), :]
                else:
                    win = sh_ref[s - 1, pl.ds(start, blk_rows), :]
                acc = acc + wb_ref[j] * win.reshape(blk_groups, SUBLANES, C)
            y_ref[n, pl.ds(pl.multiple_of(r0, SUBLANES), blk_rows), :] = acc.reshape(blk_rows, C)

        if tl == blk_rows:
            conv_rows(0)
        else:
            lax.fori_loop(0, tl // blk_rows, lambda ci, c: (conv_rows(ci * blk_rows), c)[1], 0)

        for r0 in range(0, tl, chunk):
            acc = y_ref[n, r0:r0 + chunk, :]
            mu = jnp.mean(acc, axis=-1, keepdims=True)
            xc = acc - mu
            var = jnp.mean(xc * xc, axis=-1, keepdims=True)
            ln = xc * lax.rsqrt(var + EPS) * gamma + beta
            y_ref[n, r0:r0 + chunk, :] = ln * jax.nn.sigmoid(ln) * sz_ref[n, r0:r0 + chunk, :]
    tail = ext_ref[:, lead + tl: CONV_HALO + tl, :]
    st_ref[...] = tail
    ext_ref[:, lead:CONV_HALO, :] = tail


def _conv_mixer(u, sz, buf, w_dw, b_dw, g_ln, b_ln):
    B, L, C = u.shape
    keep = CONV_SIZE - 1
    if L >= CONV_ROWS:
        nb, tl = 1, CONV_ROWS
    else:
        nb, tl = SUBLANES, L
    rows_sh = tl + CONV_HALO - SUBLANES
    tok = lambda b, i: (b, i, 0)
    vec = lambda b, i: (0, 0)
    return pl.pallas_call(
        _conv_kernel,
        grid=(B // nb, L // tl),
        in_specs=[pl.BlockSpec((nb, tl, C), tok),
                  pl.BlockSpec((nb, tl, C), tok),
                  pl.BlockSpec((nb, keep, C), lambda b, i: (b, 0, 0)),
                  pl.BlockSpec((CONV_SIZE, C), vec),
                  pl.BlockSpec((1, C), vec), pl.BlockSpec((1, C), vec), pl.BlockSpec((1, C), vec)],
        out_specs=[pl.BlockSpec((nb, tl, C), tok),
                   pl.BlockSpec((nb, keep, C), lambda b, i: (b, 0, 0))],
        out_shape=[jax.ShapeDtypeStruct((B, L, C), F32),
                   jax.ShapeDtypeStruct((B, keep, C), F32)],
        scratch_shapes=[pltpu.VMEM((nb, CONV_HALO + tl, C), F32),
                        pltpu.VMEM((SUBLANES - 1, rows_sh, C), F32),
                        pltpu.VMEM((CONV_SIZE, SUBLANES, C), F32)],
        compiler_params=_params("arbitrary", "arbitrary"),
        name="conv_mixer",
    )(u, sz, buf, w_dw, b_dw.reshape(1, C), g_ln.reshape(1, C), b_ln.reshape(1, C))


def _gla_kernel(q_ref, k_ref, v_ref, sr_ref, la_ref, s0_ref, g_ref, y_ref, sout_ref, S_ref, *, chunk):
    nb, tl, qk = q_ref.shape
    vw = v_ref.shape[-1]
    H = GLA_HEADS
    dk, dv = qk // H, vw // H
    sub = min(chunk, GLA_SUB)
    nblk = chunk // sub

    @pl.when(pl.program_id(1) == 0)
    def _():
        S_ref[...] = s0_ref[...]

    row = lax.broadcasted_iota(jnp.int32, (chunk, chunk), 0)
    col = lax.broadcasted_iota(jnp.int32, (chunk, chunk), 1)
    tril = jnp.where(row >= col, 1.0, 0.0).astype(BF16)
    row1 = lax.broadcasted_iota(jnp.int32, (chunk, 1), 0)
    sub_row = lax.broadcasted_iota(jnp.int32, (nblk, sub, 1), 1)
    ones = jnp.ones((chunk, LANES), BF16)
    assert dv % LANES == 0
    gamma = g_ref[...]
    tn = (((0,), (0,)), ((), ()))
    nt = (((1,), (1,)), ((), ()))
    halves = []
    m = chunk // 2
    while m >= sub:
        halves.append(m)
        m //= 2

    for n, c0 in [(n, c0) for n in range(nb) for c0 in range(0, tl, chunk)]:
        rows = slice(c0, c0 + chunk)
        la = la_ref[n, rows, :]
        la_hi = la.astype(BF16)
        la_lo = (la - la_hi.astype(F32)).astype(BF16)
        bc = (jnp.dot(tril, la_hi, preferred_element_type=F32)
              + jnp.dot(tril, la_lo, preferred_element_type=F32))
        btot = bc[chunk - 1:chunk, :]
        qq = q_ref[n, rows, :]
        kk = k_ref[n, rows, :]
        q0 = qq * jnp.exp(bc)
        kh = kk * jnp.exp(btot - bc)

        level_ops = []
        for m in halves:
            nb2 = chunk // (2 * m)
            bc3 = bc.reshape(nb2, 2 * m, qk)
            ref = jnp.broadcast_to(bc3[:, m - 1:m, :], (nb2, 2 * m, qk)).reshape(chunk, qk)
            e = jnp.exp(-jnp.abs(bc - ref))
            upper = ((row1 >> _log2(m)) & 1) == 1
            level_ops.append((jnp.where(upper, qq * e, 0.0), jnp.where(upper, 0.0, kk * e)))

        bc3 = bc.reshape(nblk, sub, qk)
        q3 = qq.reshape(nblk, sub, qk)
        k3 = kk.reshape(nblk, sub, qk)
        diag_terms = []
        for jp in range(sub):
            ref = jnp.broadcast_to(bc3[:, jp:jp + 1, :], (nblk, sub, qk))
            kj = jnp.broadcast_to(k3[:, jp:jp + 1, :], (nblk, sub, qk))
            wgt = jnp.where(sub_row >= jp, jnp.exp(jnp.minimum(bc3 - ref, 0.0)), 0.0)
            diag_terms.append((q3 * kj * wgt).reshape(chunk, qk))

        for h in range(H):
            ks = slice(h * dk, (h + 1) * dk)
            vs = slice(h * dv, (h + 1) * dv)
            vh = v_ref[n, rows, vs].astype(BF16)
            att = jnp.zeros((chunk, chunk), F32)
            for m, (qm, km) in zip(halves, level_ops):
                part = lax.dot_general(qm[:, ks].astype(BF16), km[:, ks].astype(BF16), nt,
                                       preferred_element_type=F32)
                if 2 * m < chunk:
                    same = (row >> _log2(2 * m)) == (col >> _log2(2 * m))
                    part = jnp.where(same, part, 0.0)
                att = att + part
            blk0 = (row >> _log2(sub)) << _log2(sub)
            for jp in range(sub):
                rs = jnp.sum(diag_terms[jp][:, ks], axis=-1, keepdims=True)
                att = att + jnp.where(col == blk0 + jp, rs, 0.0)
            S = S_ref[n, h]
            o = (jnp.dot(att.astype(BF16), vh, preferred_element_type=F32)
                 + jnp.dot(q0[:, ks].astype(BF16), S.astype(BF16), preferred_element_type=F32))
            tot = (lax.dot_general(la_hi[:, ks], ones, tn, preferred_element_type=F32)
                   + lax.dot_general(la_lo[:, ks], ones, tn, preferred_element_type=F32))
            decay = jnp.exp(jnp.concatenate([tot] * (dv // LANES), axis=1))
            S_ref[n, h] = decay * S + lax.dot_general(kh[:, ks].astype(BF16), vh, tn, preferred_element_type=F32)
            ms = jnp.mean(o * o, axis=-1, keepdims=True)
            y_ref[n, rows, vs] = o * lax.rsqrt(ms + EPS) * gamma * sr_ref[n, rows, vs]

    sout_ref[...] = S_ref[...]


def _gla_mixer(q, k, v, sr, la, s0, g_norm):
    B, L, qk = q.shape
    vw = v.shape[-1]
    H = GLA_HEADS
    dk, dv = qk // H, vw // H
    tl = min(L, GLA_ROWS)
    chunk = min(L, GLA_CHUNK)
    nb = 1 if L >= GLA_ROWS else GLA_SEQS
    assert B % nb == 0
    tok = lambda b, i: (b, i, 0)
    st = lambda b, i: (b, 0, 0, 0)
    return pl.pallas_call(
        functools.partial(_gla_kernel, chunk=chunk),
        grid=(B // nb, L // tl),
        in_specs=[pl.BlockSpec((nb, tl, qk), tok), pl.BlockSpec((nb, tl, qk), tok),
                  pl.BlockSpec((nb, tl, vw), tok), pl.BlockSpec((nb, tl, vw), tok),
                  pl.BlockSpec((nb, tl, qk), tok),
                  pl.BlockSpec((nb, H, dk, dv), st),
                  pl.BlockSpec((1, dv), lambda b, i: (0, 0))],
        out_specs=[pl.BlockSpec((nb, tl, vw), tok),
                   pl.BlockSpec((nb, H, dk, dv), st)],
        out_shape=[jax.ShapeDtypeStruct((B, L, vw), F32),
                   jax.ShapeDtypeStruct((B, H, dk, dv), F32)],
        scratch_shapes=[pltpu.VMEM((nb, H, dk, dv), F32)],
        compiler_params=_params("arbitrary", "arbitrary"),
        name="gla_mixer",
    )(q, k, v, sr, la, s0, g_norm.reshape(1, dv))


def _att_block(q_ref, kp_ref, kc_ref, vp_ref, vc_ref, o_ref, l_ref, first):
    blk = q_ref.shape[1]
    row = lax.broadcasted_iota(jnp.int32, (blk, 2 * blk), 0)
    col = lax.broadcasted_iota(jnp.int32, (blk, 2 * blk), 1)
    valid = (col >= row) & (col <= row + blk) & jnp.logical_or(col >= blk, jnp.logical_not(first))
    q = q_ref[0]
    kk = jnp.concatenate([kp_ref[0], kc_ref[0]], axis=0)
    vv = jnp.concatenate([vp_ref[0], vc_ref[0]], axis=0)
    nt = (((1,), (1,)), ((), ()))
    heads_per_tile = LANES // ATT_HEAD_DIM
    lane = lax.broadcasted_iota(jnp.int32, (1, LANES), 1)
    for t in range(q.shape[1] // LANES):
        ls = slice(t * LANES, (t + 1) * LANES)
        qp, kp, vp = q[:, ls], kk[:, ls], vv[:, ls]
        o_tile = jnp.zeros((blk, LANES), F32)
        l_tile = jnp.zeros((blk, LANES), F32)
        for hh in range(heads_per_tile):
            mine = (lane >= hh * ATT_HEAD_DIM) & (lane < (hh + 1) * ATT_HEAD_DIM)
            qm = qp * jnp.where(mine, 1.0, 0.0).astype(qp.dtype)
            s = lax.dot_general(qm, kp, nt, preferred_element_type=F32)
            s = jnp.where(valid, s, NEG_INF)
            m = jnp.max(s, axis=-1, keepdims=True)
            p = jnp.exp(s - m)
            den = jnp.sum(p, axis=-1, keepdims=True)
            o = jnp.dot(p.astype(BF16), vp, preferred_element_type=F32) / den
            o_tile = jnp.where(mine, o, o_tile)
            l_tile = jnp.where(mine, m + jnp.log(den), l_tile)
        o_ref[0, :, ls] = o_tile.astype(o_ref.dtype)
        l_ref[0, :, ls] = l_tile


def _att_sample_heads(q_ref, k_ref, v_ref, caches, y_ref, nbuf, heads):
    G, H, n_new, Dh = q_ref.shape[1:]
    nt = (((1,), (1,)), ((), ()))
    biases = []
    for g, (W, d) in enumerate(ATT_GROUPS):
        nb = nbuf[g]
        i = lax.broadcasted_iota(jnp.int32, (n_new, nb), 0)
        c = lax.broadcasted_iota(jnp.int32, (n_new, nb), 1)
        ok = ((c & (d - 1)) == (i & (d - 1))) & (c >= i - (W - nb))
        i2 = lax.broadcasted_iota(jnp.int32, (n_new, n_new), 0)
        j2 = lax.broadcasted_iota(jnp.int32, (n_new, n_new), 1)
        ok2 = (j2 <= i2) & (((i2 - j2) & (d - 1)) == 0)
        biases.append((jnp.where(ok, 0.0, NEG_INF).astype(F32), jnp.where(ok2, 0.0, NEG_INF).astype(F32)))
    for h in heads:
        scores = []
        for g in range(G):
            qh = q_ref[0, g, h].astype(BF16)
            kt = caches[g][0][0, h].astype(BF16)
            scores.append(jnp.dot(qh, kt, preferred_element_type=F32) + biases[g][0])
            kn = k_ref[0, g, h].astype(BF16)
            scores.append(lax.dot_general(qh, kn, nt, preferred_element_type=F32) + biases[g][1])
        m = functools.reduce(jnp.maximum, [jnp.max(s, axis=-1, keepdims=True) for s in scores])
        acc = jnp.zeros((n_new, Dh), F32)
        den = jnp.zeros((n_new, 1), F32)
        for g in range(G):
            pc = jnp.exp(scores[2 * g] - m).astype(BF16)
            pn = jnp.exp(scores[2 * g + 1] - m).astype(BF16)
            den = den + jnp.sum(pc.astype(F32), axis=-1, keepdims=True) + jnp.sum(pn.astype(F32), axis=-1, keepdims=True)
            vt = caches[g][1][0, h].astype(BF16)
            acc = acc + lax.dot_general(pc, vt, nt, preferred_element_type=F32)
            acc = acc + jnp.dot(pn, v_ref[0, g, h].astype(BF16), preferred_element_type=F32)
        y_ref[0, h] = acc / den


def _att_kernel(*refs, nbuf, blocks_per_class, head_phases):
    G = len(ATT_GROUPS)
    prompt_in = [refs[5 * g:5 * g + 5] for g in range(G)]
    q_ref, k_ref, v_ref = refs[5 * G:5 * G + 3]
    cache_refs = refs[5 * G + 3:7 * G + 3]
    outs = refs[7 * G + 3:]
    s = pl.program_id(0)
    for g in range(G):
        _att_block(*prompt_in[g], outs[2 * g], outs[2 * g + 1], (s % blocks_per_class[g]) == 0)

    caches = [(cache_refs[2 * g], cache_refs[2 * g + 1]) for g in range(G)]
    for phase, heads in enumerate(head_phases):
        @pl.when(s % len(head_phases) == phase)
        def _(heads=heads):
            _att_sample_heads(q_ref, k_ref, v_ref, caches, outs[2 * G], nbuf, heads)


def _attention(qkv, qs, ks, vs, caches_k, caches_v):
    G = len(ATT_GROUPS)
    blk, width = ATT_BLOCK, ATT_WIDTH
    B = qkv[0][0].shape[0]
    L = qkv[0][0].shape[1] * ATT_GROUPS[0][1]
    steps_g = B * L // blk
    Bs, _, H, n_new, Dh = qs.shape
    assert steps_g % Bs == 0
    per = steps_g // Bs
    head_phases = tuple(tuple(range(p * H // per, (p + 1) * H // per)) for p in range(per))

    in_specs, args, out_specs, out_shapes, blocks_per_class = [], [], [], [], []
    for g, (W, d) in enumerate(ATT_GROUPS):
        Ld = L // d
        assert min(W, L) // d == blk and Ld % blk == 0
        n_i = Ld // blk
        blocks_per_class.append(n_i)

        def decode(s, d=d, n_i=n_i):
            br = s // n_i
            return br // d, br % d, s % n_i

        def cur(s, decode=decode):
            b, r, i = decode(s)
            return b, i, r

        def prev(s, decode=decode):
            b, r, i = decode(s)
            return b, jnp.maximum(i - 1, 0), r

        q, k, v = qkv[g]
        in_specs += [pl.BlockSpec((1, blk, width), cur),
                     pl.BlockSpec((1, blk, width), prev), pl.BlockSpec((1, blk, width), cur),
                     pl.BlockSpec((1, blk, width), prev), pl.BlockSpec((1, blk, width), cur)]
        args += [q, k, k, v, v]
        out_specs += [pl.BlockSpec((1, blk, width), cur), pl.BlockSpec((1, blk, width), cur)]
        out_shapes += [jax.ShapeDtypeStruct((B, Ld, d * width), BF16),
                       jax.ShapeDtypeStruct((B, Ld, d * width), F32)]

    nbuf = []
    seq5 = lambda s: (s // per, 0, 0, 0, 0)
    seq4 = lambda s: (s // per, 0, 0, 0)
    in_specs += [pl.BlockSpec((1, G, H, n_new, Dh), seq5)] * 3
    args += [qs, ks, vs]
    for g, (W, d) in enumerate(ATT_GROUPS):
        nb = caches_k[g].shape[-1]
        assert d & (d - 1) == 0 and (W - nb) % d == 0
        nbuf.append(nb)
        in_specs += [pl.BlockSpec((1, H, Dh, nb), seq4)] * 2
        args += [caches_k[g], caches_v[g]]
    out_specs.append(pl.BlockSpec((1, H, n_new, Dh), seq4))
    out_shapes.append(jax.ShapeDtypeStruct((Bs, H, n_new, Dh), F32))

    outs = pl.pallas_call(
        functools.partial(_att_kernel, nbuf=tuple(nbuf),
                          blocks_per_class=tuple(blocks_per_class), head_phases=head_phases),
        grid=(steps_g,),
        in_specs=in_specs,
        out_specs=out_specs,
        out_shape=out_shapes,
        compiler_params=_params("arbitrary"),
        name="attention",
    )(*args)
    return [tuple(outs[2 * g:2 * g + 2]) for g in range(G)], outs[2 * G]


def _rope_tables(pos):
    half = ATT_HEAD_DIM // 2
    inv = ROPE_THETA ** (-jnp.arange(half, dtype=F32) / half)
    ang = pos.astype(F32)[:, None] * inv[None, :]
    cos, sin = jnp.cos(ang), jnp.sin(ang)
    reps = LANES // ATT_HEAD_DIM
    cos_t = jnp.tile(jnp.concatenate([cos, cos], axis=-1), (1, reps))
    sin_t = jnp.tile(jnp.concatenate([-sin, sin], axis=-1), (1, reps))
    return cos_t, sin_t


def kernel(x_prompt, x_sample, state_conv, state_gla, cache_k_g0, cache_v_g0, cache_k_g1, cache_v_g1, cache_k_g2, cache_v_g2, c_prompt, c_sample, w_ada, b_ada, g_pre, g_post, w_conv_in, w_dw, b_dw, g_conv_ln, b_conv_ln, w_conv_out, w_gla_in, w_gla_a1, w_gla_a2, b_gla_a, g_gla_norm, w_gla_out, w_att_in, w_att_out):
    depth = w_ada.shape[0]
    D = x_prompt.shape[-1]
    G = len(ATT_GROUPS)
    xs = [x_prompt, x_sample]
    nseq = [x_prompt.shape[0], x_sample.shape[0]]
    caches_k = (cache_k_g0, cache_k_g1, cache_k_g2)
    caches_v = (cache_v_g0, cache_v_g1, cache_v_g2)

    n_c = nseq[0] + nseq[1]
    pad = (-n_c) % SUBLANES
    c_all = jnp.concatenate([c_prompt, c_sample, jnp.zeros((pad, D), F32)], axis=0)
    mod_all = _modulation(c_all, w_ada, b_ada)
    mods = [[mod_all[l, :nseq[0]].reshape(nseq[0], 1, 3 * D),
             mod_all[l, nseq[0]:n_c].reshape(nseq[1], 1, 3 * D)] for l in range(depth)]

    pos = [jnp.arange(x_prompt.shape[1]), PAST_LEN + jnp.arange(x_sample.shape[1])]
    rope = [_rope_tables(p) for p in pos]

    conv_new, gla_new = ([], []), ([], [])
    k_new = ([[] for _ in range(G)], [[] for _ in range(G)])
    v_new = ([[] for _ in range(G)], [[] for _ in range(G)])
    no_extra = lambda nb, tl: []
    heads = (ATT_HEADS, ATT_HEAD_DIM)

    for l in range(depth):
        kind, j = l % N_MIXERS, l // N_MIXERS
        for grp in range(2):
            x, mod = xs[grp], mods[l][grp]
            B, L, _ = x.shape
            if kind == 0:
                C = w_dw.shape[-1]
                u, sz = _pre_call(_pre_conv_kernel, "pre_conv", x, mod, g_pre[l],
                                  [w_conv_in[j].astype(BF16)], no_extra, [], [C, C])
                buf = jnp.zeros((B, CONV_SIZE - 1, C), F32) if grp == 0 else state_conv[j]
                y, st = _conv_mixer(u, sz, buf, w_dw[j], b_dw[j], g_conv_ln[j], b_conv_ln[j])
                conv_new[grp].append(st)
                xs[grp] = _post_call(_post_kernel, "post_conv", [y], w_conv_out[j].astype(BF16), g_post[l], mod, x)
            elif kind == 1:
                qk = w_gla_a2.shape[-1]
                vw = w_gla_out.shape[1]
                wa1 = jnp.zeros((D, LANES), F32).at[:, :GLA_RANK].set(w_gla_a1[j]).astype(BF16)
                wa2 = jnp.zeros((LANES, qk), F32).at[:GLA_RANK].set(w_gla_a2[j]).astype(BF16)
                q, k, v, sr, la = _pre_call(
                    _pre_gla_kernel, "pre_gla", x, mod, g_pre[l],
                    [w_gla_in[j].astype(BF16), wa1, wa2, b_gla_a[j].reshape(1, qk)], no_extra, [],
                    [qk, qk, vw, vw, qk])
                dk, dv = qk // GLA_HEADS, vw // GLA_HEADS
                s0 = jnp.zeros((B, GLA_HEADS, dk, dv), F32) if grp == 0 else state_gla[j]
                y, st = _gla_mixer(q, k, v, sr, la, s0, g_gla_norm[j])
                gla_new[grp].append(st)
                xs[grp] = _post_call(_post_kernel, "post_gla", [y], w_gla_out[j].astype(BF16), g_post[l], mod, x)
            elif grp == 0:
                xp, xsmp = xs
                mod_p, mod_s = mods[l]
                Bp, Lp, _ = xp.shape
                Bs, Ls, _ = xsmp.shape
                w_out = w_att_out[j].astype(BF16)
                qkv, sz_p, tails = _pre_att_prompt(xp, mod_p, g_pre[l], w_att_in[j], pos[0])
                gw = G * ATT_WIDTH
                cos_t, sin_t = rope[1]
                rope_specs = lambda nb, tl: [pl.BlockSpec((tl, LANES), lambda b, i: (i, 0))] * 2
                q, k, v, sz_s = _pre_call(_pre_att_kernel, "pre_att", xsmp, mod_s, g_pre[l],
                                          [w_att_in[j].astype(BF16)], rope_specs, [cos_t, sin_t],
                                          [gw, gw, gw, ATT_WIDTH])
                ck = [jnp.transpose(c[j], (0, 2, 3, 1)) for c in caches_k]
                cv = [jnp.transpose(c[j], (0, 2, 3, 1)) for c in caches_v]
                q5, k5, v5 = (t.reshape(Bs, Ls, G, *heads) for t in (q, k, v))
                qt, kt, vt = (jnp.transpose(t, (0, 2, 3, 1, 4)) for t in (q5, k5, v5))
                att_outs, y = _attention(qkv, qt, kt, vt, ck, cv)
                y = jnp.transpose(y, (0, 2, 1, 3)).reshape(Bs, Ls, ATT_WIDTH)
                xs[0] = _post_att_prompt(att_outs, sz_p, w_out, g_post[l], mod_p, xp)
                xs[1] = _post_call(_post_gate_kernel, "post_att_sample", [y, sz_s], w_out, g_post[l], mod_s, xsmp)
                for g, (W, d) in enumerate(ATT_GROUPS):
                    keep = min(W, Lp)
                    for store, t in ((k_new, tails[g][0]), (v_new, tails[g][1])):
                        t = t[:, t.shape[1] - keep // d:, :]
                        store[0][g].append(t.reshape(Bp, keep, *heads))
                    k_new[1][g].append(k5[:, :, g])
                    v_new[1][g].append(v5[:, :, g])

    outs = [xs[0], xs[1], jnp.stack(conv_new[0]), jnp.stack(conv_new[1]),
            jnp.stack(gla_new[0]), jnp.stack(gla_new[1])]
    for grp in range(2):
        for g in range(G):
            outs += [jnp.stack(k_new[grp][g]), jnp.stack(v_new[grp][g])]
    return tuple(outs)
```

```python
import functools

import jax
import jax.numpy as jnp
from jax import lax
from jax.experimental import pallas as pl
from jax.experimental.pallas import tpu as pltpu

F32 = jnp.float32
BF16 = jnp.bfloat16
HIGHEST = lax.Precision.HIGHEST

PAST_LEN = 2048
N_MIXERS = 3
CONV_SIZE = 31
GLA_HEADS = 4
GLA_RANK = 16
GLA_TAU = 16.0
ATT_GROUPS = ((128, 1), (512, 4), (2048, 16))
ATT_HEADS = 8
ATT_HEAD_DIM = 64
ATT_WIDTH = ATT_HEADS * ATT_HEAD_DIM
ROPE_THETA = 10000.0
EPS = 1e-6
NEG_INF = -1e30

LANES = 128
SUBLANES = 8
VMEM_LIMIT_BYTES = 56 * 1024 * 1024

PROJ_ROWS = 512
CONV_ROWS = 256
CONV_CHUNK = 32
CONV_BLOCK_ROWS = 32
CONV_BLOCK_LANES = 512
CONV_HALO = 32
GLA_ROWS = 256
GLA_CHUNK = 64
GLA_SUB = 8
GLA_SEQS = 4
ATT_BLOCK = 128
ATT_PROJ_ROWS = 256


def _params(*sem):
    return pltpu.CompilerParams(dimension_semantics=sem, vmem_limit_bytes=VMEM_LIMIT_BYTES)


def _log2(n):
    assert n > 0 and n & (n - 1) == 0, n
    return n.bit_length() - 1


def _token_tiling(B, L, rows=PROJ_ROWS):
    if L >= rows:
        return 1, rows
    nb = max(1, rows // L)
    while B % nb:
        nb //= 2
    return nb, L


def _mod_kernel(c_ref, w_ref, b_ref, o_ref):
    o_ref[0] = jnp.dot(c_ref[...], w_ref[0], precision=HIGHEST, preferred_element_type=F32) + b_ref[0]


def _modulation(c_all, w_ada, b_ada):
    depth, D, N = w_ada.shape
    R = c_all.shape[0]
    tn = 1024
    return pl.pallas_call(
        _mod_kernel,
        grid=(depth, N // tn),
        in_specs=[pl.BlockSpec((R, D), lambda l, j: (0, 0)),
                  pl.BlockSpec((1, D, tn), lambda l, j: (l, 0, j)),
                  pl.BlockSpec((1, 1, tn), lambda l, j: (l, 0, j))],
        out_specs=pl.BlockSpec((1, R, tn), lambda l, j: (l, 0, j)),
        out_shape=jax.ShapeDtypeStruct((depth, R, N), F32),
        compiler_params=_params("arbitrary", "arbitrary"),
        name="adaln_mod",
    )(c_all, w_ada, b_ada.reshape(depth, 1, N))


def _modulated_norm(x_ref, mod_ref, g_ref):
    x = x_ref[...]
    nb, tl, D = x.shape
    ms = jnp.mean(x * x, axis=-1, keepdims=True)
    y = x * lax.rsqrt(ms + EPS) * g_ref[...]
    shift = mod_ref[:, :, 0:D]
    scale = mod_ref[:, :, D:2 * D]
    h = y * (1.0 + scale) + shift
    return h.reshape(nb * tl, D).astype(BF16)


def _pre_conv_kernel(x_ref, mod_ref, g_ref, w_ref, u_ref, sz_ref):
    h = _modulated_norm(x_ref, mod_ref, g_ref)
    res = jnp.dot(h, w_ref[...], preferred_element_type=F32)
    C = u_ref.shape[-1]
    u = res[:, :C] * jax.nn.sigmoid(res[:, C:2 * C])
    z = res[:, 2 * C:]
    u_ref[...] = u.reshape(u_ref.shape)
    sz_ref[...] = (z * jax.nn.sigmoid(z)).reshape(sz_ref.shape)


def _pre_gla_kernel(x_ref, mod_ref, g_ref, w_ref, wa1_ref, wa2_ref, ba_ref,
                    q_ref, k_ref, v_ref, sr_ref, la_ref):
    h = _modulated_norm(x_ref, mod_ref, g_ref)
    res = jnp.dot(h, w_ref[...], preferred_element_type=F32)
    qk = q_ref.shape[-1]
    vw = v_ref.shape[-1]
    dk = qk // GLA_HEADS
    q_ref[...] = (res[:, :qk] * (dk ** -0.5)).reshape(q_ref.shape)
    k_ref[...] = res[:, qk:2 * qk].reshape(k_ref.shape)
    v_ref[...] = res[:, 2 * qk:2 * qk + vw].reshape(v_ref.shape)
    r = res[:, 2 * qk + vw:]
    sr_ref[...] = (r * jax.nn.sigmoid(r)).reshape(sr_ref.shape)
    low = jnp.dot(h, wa1_ref[...], preferred_element_type=F32)
    zg = jnp.dot(low.astype(BF16), wa2_ref[...], preferred_element_type=F32) + ba_ref[...]
    log_sig = jnp.minimum(zg, 0.0) - jnp.log(1.0 + jnp.exp(-jnp.abs(zg)))
    la_ref[...] = (log_sig * (1.0 / GLA_TAU)).reshape(la_ref.shape)


def _swap_halves(x):
    half = ATT_HEAD_DIM // 2
    lane = lax.broadcasted_iota(jnp.int32, x.shape, 1)
    lower = (lane % ATT_HEAD_DIM) < half
    return jnp.where(lower, pltpu.roll(x, LANES - half, 1), pltpu.roll(x, half, 1))


def _rope_cols(res, col0, width, cos, sin, mult):
    pieces = []
    for c in range(width // LANES):
        xc = res[:, col0 + c * LANES: col0 + (c + 1) * LANES]
        val = xc * cos + _swap_halves(xc) * sin
        pieces.append(val * mult if mult != 1.0 else val)
    return pieces


def _pre_att_kernel(x_ref, mod_ref, g_ref, w_ref, cos_ref, sin_ref, q_ref, k_ref, v_ref, sz_ref):
    h = _modulated_norm(x_ref, mod_ref, g_ref)
    res = jnp.dot(h, w_ref[...], preferred_element_type=F32)
    nb, tl, gw = q_ref.shape
    cos = jnp.broadcast_to(cos_ref[...], (nb, tl, LANES)).reshape(nb * tl, LANES)
    sin = jnp.broadcast_to(sin_ref[...], (nb, tl, LANES)).reshape(nb * tl, LANES)
    for c, val in enumerate(_rope_cols(res, 0, gw, cos, sin, ATT_HEAD_DIM ** -0.5)):
        q_ref[:, :, c * LANES:(c + 1) * LANES] = val.reshape(nb, tl, LANES)
    for c, val in enumerate(_rope_cols(res, gw, gw, cos, sin, 1.0)):
        k_ref[:, :, c * LANES:(c + 1) * LANES] = val.reshape(nb, tl, LANES)
    v_ref[...] = res[:, 2 * gw:3 * gw].reshape(v_ref.shape)
    z = res[:, 3 * gw:]
    sz_ref[...] = (z * jax.nn.sigmoid(z)).reshape(sz_ref.shape)


def _pre_call(kernel, name, x, mod, g_pre, weights, extra_specs, extras, out_widths):
    B, L, D = x.shape
    nb, tl = _token_tiling(B, L, PROJ_ROWS if sum(out_widths) <= 7 * D // 2 else PROJ_ROWS // 2)
    tok = lambda b, i: (b, i, 0)
    in_specs = [pl.BlockSpec((nb, tl, D), tok),
                pl.BlockSpec((nb, 1, mod.shape[-1]), lambda b, i: (b, 0, 0)),
                pl.BlockSpec((1, D), lambda b, i: (0, 0))]
    in_specs += [pl.BlockSpec(w.shape, lambda b, i: (0, 0)) for w in weights]
    in_specs += extra_specs(nb, tl)
    return pl.pallas_call(
        kernel,
        grid=(B // nb, L // tl),
        in_specs=in_specs,
        out_specs=[pl.BlockSpec((nb, tl, n), tok) for n in out_widths],
        out_shape=[jax.ShapeDtypeStruct((B, L, n), F32) for n in out_widths],
        compiler_params=_params("arbitrary", "arbitrary"),
        name=name,
    )(x, mod, g_pre.reshape(1, D), *weights, *extras)


def _class_order_source(idx, per, d):
    return (idx & (per - 1)) * d + (idx >> _log2(per))


def _pre_att_prompt_kernel(x_ref, mod_ref, g_ref, w0_ref, w1_ref, w2_ref, wz_ref,
                           c0_ref, s0_ref, c1_ref, s1_ref, c2_ref, s2_ref,
                           q0_ref, k0_ref, v0_ref, q1_ref, k1_ref, v1_ref, q2_ref, k2_ref, v2_ref, sz_ref,
                           kt0_ref, vt0_ref, kt1_ref, vt1_ref, kt2_ref, vt2_ref, *, first_tail):
    h = _modulated_norm(x_ref, mod_ref, g_ref)
    tm = h.shape[0]
    width = sz_ref.shape[-1]
    z = jnp.dot(h, wz_ref[...], preferred_element_type=F32)
    sz_ref[0] = z * jax.nn.sigmoid(z)
    step = pl.program_id(1)
    groups = ((w0_ref, c0_ref, s0_ref, q0_ref, k0_ref, v0_ref, kt0_ref, vt0_ref),
              (w1_ref, c1_ref, s1_ref, q1_ref, k1_ref, v1_ref, kt1_ref, vt1_ref),
              (w2_ref, c2_ref, s2_ref, q2_ref, k2_ref, v2_ref, kt2_ref, vt2_ref))
    for g, (w_ref, cos_ref, sin_ref, q_ref, k_ref, v_ref, kt_ref, vt_ref) in enumerate(groups):
        d = ATT_GROUPS[g][1]
        per = tm // d
        if d > 1:
            dst = lax.broadcasted_iota(jnp.int32, (tm, tm), 0)
            src = lax.broadcasted_iota(jnp.int32, (tm, tm), 1)
            perm = jnp.where(src == _class_order_source(dst, per, d), 1.0, 0.0).astype(BF16)
            hg = jnp.dot(perm, h, preferred_element_type=F32).astype(BF16)
        else:
            hg = h
        res = jnp.dot(hg, w_ref[...], preferred_element_type=F32)
        cos, sin = cos_ref[...], sin_ref[...]
        qf = jnp.concatenate(_rope_cols(res, 0, width, cos, sin, ATT_HEAD_DIM ** -0.5), axis=1)
        kf = jnp.concatenate(_rope_cols(res, width, width, cos, sin, 1.0), axis=1)
        vf = res[:, 2 * width:]
        qb, kb, vb = qf.astype(BF16), kf.astype(BF16), vf.astype(BF16)
        for r in range(d):
            rs = slice(r * per, (r + 1) * per)
            cs = slice(r * width, (r + 1) * width)
            q_ref[0, :, cs] = qb[rs]
            k_ref[0, :, cs] = kb[rs]
            v_ref[0, :, cs] = vb[rs]

        @pl.when(step >= first_tail[g])
        def _(kf=kf, vf=vf, kt_ref=kt_ref, vt_ref=vt_ref, d=d, per=per):
            for r in range(d):
                rs = slice(r * per, (r + 1) * per)
                cs = slice(r * width, (r + 1) * width)
                kt_ref[0, :, cs] = kf[rs]
                vt_ref[0, :, cs] = vf[rs]


def _pre_att_prompt(x, mod, g_pre, w_att_in, pos):
    B, L, D = x.shape
    G = len(ATT_GROUPS)
    width = ATT_WIDTH
    gw = G * width
    tm = ATT_PROJ_ROWS
    n_steps = L // tm
    w = w_att_in.astype(BF16)
    weights, tables, first_tail = [], [], []
    out_specs, out_shapes = [], []
    tail_specs, tail_shapes = [], []
    for g, (W, d) in enumerate(ATT_GROUPS):
        gs = slice(g * width, (g + 1) * width)
        weights.append(jnp.concatenate([w[:, gs], w[:, gw:][:, gs], w[:, 2 * gw:][:, gs]], axis=1))
        per = tm // d
        assert tm % d == 0 and per % 16 == 0 and L % tm == 0
        a = jnp.arange(tm)
        order = (jnp.arange(n_steps)[:, None] * tm + _class_order_source(a, per, d)[None, :]).reshape(-1)
        tables += list(_rope_tables(pos[order]))
        for _ in range(3):
            out_specs.append(pl.BlockSpec((1, per, d * width), lambda b, i: (b, i, 0)))
            out_shapes.append(jax.ShapeDtypeStruct((B, L // d, d * width), BF16))
        tail_rows = max(min(W, L) // d, per)
        n_tail = tail_rows // per
        first_tail.append(n_steps - n_tail)
        for _ in range(2):
            tail_specs.append(pl.BlockSpec(
                (1, per, d * width), lambda b, i, first=n_steps - n_tail: (b, jnp.maximum(i - first, 0), 0)))
            tail_shapes.append(jax.ShapeDtypeStruct((B, tail_rows, d * width), F32))
    weights.append(w[:, 3 * gw:])
    const = lambda b, i: (0, 0)
    in_specs = [pl.BlockSpec((1, tm, D), lambda b, i: (b, i, 0)),
                pl.BlockSpec((1, 1, mod.shape[-1]), lambda b, i: (b, 0, 0)),
                pl.BlockSpec((1, D), const)]
    in_specs += [pl.BlockSpec(wt.shape, const) for wt in weights]
    in_specs += [pl.BlockSpec((tm, LANES), lambda b, i: (i, 0)) for _ in tables]
    sz_spec = pl.BlockSpec((1, tm, width), lambda b, i: (b, i, 0))
    outs = pl.pallas_call(
        functools.partial(_pre_att_prompt_kernel, first_tail=tuple(first_tail)),
        grid=(B, n_steps),
        in_specs=in_specs,
        out_specs=out_specs + [sz_spec] + tail_specs,
        out_shape=out_shapes + [jax.ShapeDtypeStruct((B, L, width), F32)] + tail_shapes,
        compiler_params=_params("arbitrary", "arbitrary"),
        name="pre_att_prompt",
    )(x, mod, g_pre.reshape(1, D), *weights, *tables)
    qkv = [tuple(outs[3 * g:3 * g + 3]) for g in range(G)]
    sz = outs[3 * G]
    tails = [tuple(outs[3 * G + 1 + 2 * g: 3 * G + 3 + 2 * g]) for g in range(G)]
    return qkv, sz, tails


def _gated_residual(y_bf16, w_ref, g_ref, mod_ref, x_ref, o_ref):
    nb, tl, D = x_ref.shape
    o = jnp.dot(y_bf16, w_ref[...], preferred_element_type=F32)
    ms = jnp.mean(o * o, axis=-1, keepdims=True)
    o = (o * lax.rsqrt(ms + EPS) * g_ref[...]).reshape(nb, tl, D)
    gate = mod_ref[:, :, 2 * D:3 * D]
    o_ref[...] = x_ref[...] + gate * o


def _post_kernel(y_ref, w_ref, g_ref, mod_ref, x_ref, o_ref):
    nb, tl, K = y_ref.shape
    _gated_residual(y_ref[...].reshape(nb * tl, K).astype(BF16), w_ref, g_ref, mod_ref, x_ref, o_ref)


def _post_gate_kernel(y_ref, sz_ref, w_ref, g_ref, mod_ref, x_ref, o_ref):
    nb, tl, K = y_ref.shape
    y = y_ref[...] * sz_ref[...]
    _gated_residual(y.reshape(nb * tl, K).astype(BF16), w_ref, g_ref, mod_ref, x_ref, o_ref)


def _post_call(kernel, name, ys, w_out, g_post, mod, x):
    B, L, D = x.shape
    K = w_out.shape[0]
    nb, tl = _token_tiling(B, L)
    tok = lambda b, i: (b, i, 0)
    in_specs = [pl.BlockSpec((nb, tl, K), tok) for _ in ys]
    in_specs += [pl.BlockSpec((K, D), lambda b, i: (0, 0)),
                 pl.BlockSpec((1, D), lambda b, i: (0, 0)),
                 pl.BlockSpec((nb, 1, mod.shape[-1]), lambda b, i: (b, 0, 0)),
                 pl.BlockSpec((nb, tl, D), tok)]
    return pl.pallas_call(
        kernel,
        grid=(B // nb, L // tl),
        in_specs=in_specs,
        out_specs=pl.BlockSpec((nb, tl, D), tok),
        out_shape=jax.ShapeDtypeStruct((B, L, D), F32),
        compiler_params=_params("arbitrary", "arbitrary"),
        name=name,
    )(*ys, w_out, g_post.reshape(1, D), mod, x)


def _class_rows(ref, d):
    width = ref.shape[-1] // d
    return jnp.concatenate([ref[0, :, r * width:(r + 1) * width] for r in range(d)], axis=0)


def _post_att_kernel(o0_ref, l0_ref, o1_ref, l1_ref, o2_ref, l2_ref, sz_ref, w_ref, g_ref, mod_ref, x_ref, o_ref):
    _, tm, K = sz_ref.shape
    outs, lses = [], []
    for g, (o_g_ref, l_g_ref) in enumerate(((o0_ref, l0_ref), (o1_ref, l1_ref), (o2_ref, l2_ref))):
        d = ATT_GROUPS[g][1]
        if d == 1:
            outs.append(o_g_ref[0].astype(F32))
            lses.append(l_g_ref[0])
            continue
        per = tm // d
        dst = lax.broadcasted_iota(jnp.int32, (tm, tm), 0)
        src = lax.broadcasted_iota(jnp.int32, (tm, tm), 1)
        back = jnp.where(dst == _class_order_source(src, per, d), 1.0, 0.0).astype(BF16)
        outs.append(jnp.dot(back, _class_rows(o_g_ref, d), preferred_element_type=F32))
        lse = _class_rows(l_g_ref, d)
        hi = lse.astype(BF16)
        rest = lse - hi.astype(F32)
        mid = rest.astype(BF16)
        lo = (rest - mid.astype(F32)).astype(BF16)
        lses.append(jnp.dot(back, hi, preferred_element_type=F32)
                    + jnp.dot(back, mid, preferred_element_type=F32)
                    + jnp.dot(back, lo, preferred_element_type=F32))
    l0, l1, l2 = lses
    m = jnp.maximum(jnp.maximum(l0, l1), l2)
    e0, e1, e2 = jnp.exp(l0 - m), jnp.exp(l1 - m), jnp.exp(l2 - m)
    att = (e0 * outs[0] + e1 * outs[1] + e2 * outs[2]) / (e0 + e1 + e2)
    y = att * sz_ref[0]
    _gated_residual(y.astype(BF16), w_ref, g_ref, mod_ref, x_ref, o_ref)


def _post_att_prompt(att_outs, sz, w_out, g_post, mod, x):
    B, L, D = x.shape
    K = w_out.shape[0]
    tm = ATT_PROJ_ROWS
    tok = lambda b, i: (b, i, 0)
    in_specs, args = [], []
    for g, (W, d) in enumerate(ATT_GROUPS):
        assert tm % (16 * d) == 0
        for t in att_outs[g]:
            in_specs.append(pl.BlockSpec((1, tm // d, d * K), tok))
            args.append(t)
    in_specs += [pl.BlockSpec((1, tm, K), tok),
                 pl.BlockSpec((K, D), lambda b, i: (0, 0)),
                 pl.BlockSpec((1, D), lambda b, i: (0, 0)),
                 pl.BlockSpec((1, 1, mod.shape[-1]), lambda b, i: (b, 0, 0)),
                 pl.BlockSpec((1, tm, D), tok)]
    return pl.pallas_call(
        _post_att_kernel,
        grid=(B, L // tm),
        in_specs=in_specs,
        out_specs=pl.BlockSpec((1, tm, D), tok),
        out_shape=jax.ShapeDtypeStruct((B, L, D), F32),
        compiler_params=_params("arbitrary", "arbitrary"),
        name="post_att_prompt",
    )(*args, sz, w_out, g_post.reshape(1, D), mod, x)


def _conv_kernel(u_ref, sz_ref, buf_ref, w_ref, b_ref, g_ref, bl_ref, y_ref, st_ref, ext_ref, sh_ref, wb_ref,
                 *, time_major_state):
    nb, tl, C = u_ref.shape
    keep = CONV_SIZE - 1
    lead = CONV_HALO - keep
    rows_sh = sh_ref.shape[1]

    @pl.when(pl.program_id(1) == 0)
    def _():
        if time_major_state:
            for n in range(nb):
                ext_ref[n, lead:CONV_HALO, :] = buf_ref[:, n, :]
        else:
            ext_ref[:, lead:CONV_HALO, :] = buf_ref[...]
        for j in range(CONV_SIZE):
            wb_ref[j] = jnp.broadcast_to(w_ref[j:j + 1, :], (SUBLANES, C))

    ext_ref[:, CONV_HALO:CONV_HALO + tl, :] = u_ref[...]
    chunk = min(tl, CONV_CHUNK)
    blk_rows = min(tl, CONV_BLOCK_ROWS)
    blk_groups = blk_rows // SUBLANES
    bias = b_ref[...]
    gamma = g_ref[...]
    beta = bl_ref[...]
    for n in range(nb):
        for s in range(1, SUBLANES):
            sh_ref[s - 1] = ext_ref[n, s:s + rows_sh, :]

        def conv_rows(r0):
            for c0 in range(0, C, CONV_BLOCK_LANES):
                cl = slice(c0, min(C, c0 + CONV_BLOCK_LANES))
                wc = cl.stop - cl.start
                acc = jnp.broadcast_to(bias[:, cl], (blk_groups, SUBLANES, wc))
                for j in range(CONV_SIZE):
                    a, s = divmod(lead + j, SUBLANES)
                    start = pl.multiple_of(r0 + a * SUBLANES, SUBLANES)
                    if s == 0:
                        win = ext_ref[n, pl.ds(start, blk_rows), cl]
                    else:
                        win = sh_ref[s - 1, pl.ds(start, blk_rows), cl]
                    acc = acc + wb_ref[j, :, cl] * win.reshape(blk_groups, SUBLANES, wc)
                y_ref[n, pl.ds(pl.multiple_of(r0, SUBLANES), blk_rows), cl] = acc.reshape(blk_rows, wc)

        if tl == blk_rows:
            conv_rows(0)
        else:
            lax.fori_loop(0, tl // blk_rows, lambda ci, c: (conv_rows(ci * blk_rows), c)[1], 0)

        for r0 in range(0, tl, chunk):
            acc = y_ref[n, r0:r0 + chunk, :]
            mu = jnp.mean(acc, axis=-1, keepdims=True)
            xc = acc - mu
            var = jnp.mean(xc * xc, axis=-1, keepdims=True)
            ln = xc * lax.rsqrt(var + EPS) * gamma + beta
            y_ref[n, r0:r0 + chunk, :] = ln * jax.nn.sigmoid(ln) * sz_ref[n, r0:r0 + chunk, :]
    tail = ext_ref[:, lead + tl: CONV_HALO + tl, :]
    st_ref[...] = tail
    ext_ref[:, lead:CONV_HALO, :] = tail


def _conv_mixer(u, sz, buf, w_dw, b_dw, g_ln, b_ln, time_major_state=False, layer=0):
    B, L, C = u.shape
    keep = CONV_SIZE - 1
    if L >= CONV_ROWS:
        nb, tl = 1, CONV_ROWS
    else:
        nb, tl = SUBLANES, L
    rows_sh = tl + CONV_HALO - SUBLANES
    tok = lambda b, i: (b, i, 0)
    vec = lambda b, i: (0, 0)
    if time_major_state:
        assert nb % SUBLANES == 0
        buf_spec = pl.BlockSpec((None, keep, nb, C), lambda b, i: (layer, 0, b, 0))
    else:
        buf_spec = pl.BlockSpec((nb, keep, C), lambda b, i: (b, 0, 0))
    return pl.pallas_call(
        functools.partial(_conv_kernel, time_major_state=time_major_state),
        grid=(B // nb, L // tl),
        in_specs=[pl.BlockSpec((nb, tl, C), tok),
                  pl.BlockSpec((nb, tl, C), tok),
                  buf_spec,
                  pl.BlockSpec((CONV_SIZE, C), vec),
                  pl.BlockSpec((1, C), vec), pl.BlockSpec((1, C), vec), pl.BlockSpec((1, C), vec)],
        out_specs=[pl.BlockSpec((nb, tl, C), tok),
                   pl.BlockSpec((nb, keep, C), lambda b, i: (b, 0, 0))],
        out_shape=[jax.ShapeDtypeStruct((B, L, C), F32),
                   jax.ShapeDtypeStruct((B, keep, C), F32)],
        scratch_shapes=[pltpu.VMEM((nb, CONV_HALO + tl, C), F32),
                        pltpu.VMEM((SUBLANES - 1, rows_sh, C), F32),
                        pltpu.VMEM((CONV_SIZE, SUBLANES, C), F32)],
        compiler_params=_params("arbitrary", "arbitrary"),
        name="conv_mixer",
    )(u, sz, buf, w_dw, b_dw.reshape(1, C), g_ln.reshape(1, C), b_ln.reshape(1, C))


def _gla_kernel(q_ref, k_ref, v_ref, sr_ref, la_ref, s0_ref, g_ref, y_ref, sout_ref, S_ref, *, chunk):
    nb, tl, qk = q_ref.shape
    vw = v_ref.shape[-1]
    H = GLA_HEADS
    dk, dv = qk // H, vw // H
    sub = min(chunk, GLA_SUB)
    nblk = chunk // sub

    @pl.when(pl.program_id(1) == 0)
    def _():
        S_ref[...] = s0_ref[...]

    row = lax.broadcasted_iota(jnp.int32, (chunk, chunk), 0)
    col = lax.broadcasted_iota(jnp.int32, (chunk, chunk), 1)
    tril = jnp.where(row >= col, 1.0, 0.0).astype(BF16)
    row1 = lax.broadcasted_iota(jnp.int32, (chunk, 1), 0)
    sub_row = lax.broadcasted_iota(jnp.int32, (nblk, sub, 1), 1)
    ones = jnp.ones((chunk, LANES), BF16)
    assert dv % LANES == 0
    gamma = g_ref[...]
    tn = (((0,), (0,)), ((), ()))
    nt = (((1,), (1,)), ((), ()))
    halves = []
    m = chunk // 2
    while m >= sub:
        halves.append(m)
        m //= 2

    for n, c0 in [(n, c0) for n in range(nb) for c0 in range(0, tl, chunk)]:
        rows = slice(c0, c0 + chunk)
        la = la_ref[n, rows, :]
        la_hi = la.astype(BF16)
        la_lo = (la - la_hi.astype(F32)).astype(BF16)
        bc = (jnp.dot(tril, la_hi, preferred_element_type=F32)
              + jnp.dot(tril, la_lo, preferred_element_type=F32))
        btot = bc[chunk - 1:chunk, :]
        qq = q_ref[n, rows, :]
        kk = k_ref[n, rows, :]
        q0 = qq * jnp.exp(bc)
        kh = kk * jnp.exp(btot - bc)

        level_ops = []
        for m in halves:
            nb2 = chunk // (2 * m)
            bc3 = bc.reshape(nb2, 2 * m, qk)
            ref = jnp.broadcast_to(bc3[:, m - 1:m, :], (nb2, 2 * m, qk)).reshape(chunk, qk)
            e = jnp.exp(-jnp.abs(bc - ref))
            upper = ((row1 >> _log2(m)) & 1) == 1
            level_ops.append((jnp.where(upper, qq * e, 0.0), jnp.where(upper, 0.0, kk * e)))

        bc3 = bc.reshape(nblk, sub, qk)
        q3 = qq.reshape(nblk, sub, qk)
        k3 = kk.reshape(nblk, sub, qk)
        diag_terms = []
        for jp in range(sub):
            ref = jnp.broadcast_to(bc3[:, jp:jp + 1, :], (nblk, sub, qk))
            kj = jnp.broadcast_to(k3[:, jp:jp + 1, :], (nblk, sub, qk))
            wgt = jnp.where(sub_row >= jp, jnp.exp(jnp.minimum(bc3 - ref, 0.0)), 0.0)
            diag_terms.append((q3 * kj * wgt).reshape(chunk, qk))

        for h in range(H):
            ks = slice(h * dk, (h + 1) * dk)
            vs = slice(h * dv, (h + 1) * dv)
            vh = v_ref[n, rows, vs].astype(BF16)
            att = jnp.zeros((chunk, chunk), F32)
            for m, (qm, km) in zip(halves, level_ops):
                part = lax.dot_general(qm[:, ks].astype(BF16), km[:, ks].astype(BF16), nt,
                                       preferred_element_type=F32)
                if 2 * m < chunk:
                    same = (row >> _log2(2 * m)) == (col >> _log2(2 * m))
                    part = jnp.where(same, part, 0.0)
                att = att + part
            blk0 = (row >> _log2(sub)) << _log2(sub)
            for jp in range(sub):
                rs = jnp.sum(diag_terms[jp][:, ks], axis=-1, keepdims=True)
                att = att + jnp.where(col == blk0 + jp, rs, 0.0)
            S = S_ref[n, h]
            o = (jnp.dot(att.astype(BF16), vh, preferred_element_type=F32)
                 + jnp.dot(q0[:, ks].astype(BF16), S.astype(BF16), preferred_element_type=F32))
            tot = (lax.dot_general(la_hi[:, ks], ones, tn, preferred_element_type=F32)
                   + lax.dot_general(la_lo[:, ks], ones, tn, preferred_element_type=F32))
            decay = jnp.exp(jnp.concatenate([tot] * (dv // LANES), axis=1))
            S_ref[n, h] = decay * S + lax.dot_general(kh[:, ks].astype(BF16), vh, tn, preferred_element_type=F32)
            ms = jnp.mean(o * o, axis=-1, keepdims=True)
            y_ref[n, rows, vs] = o * lax.rsqrt(ms + EPS) * gamma * sr_ref[n, rows, vs]

    sout_ref[...] = S_ref[...]


def _gla_mixer(q, k, v, sr, la, s0, g_norm):
    B, L, qk = q.shape
    vw = v.shape[-1]
    H = GLA_HEADS
    dk, dv = qk // H, vw // H
    tl = min(L, GLA_ROWS)
    chunk = min(L, GLA_CHUNK)
    nb = 1 if L >= GLA_ROWS else GLA_SEQS
    assert B % nb == 0
    tok = lambda b, i: (b, i, 0)
    st = lambda b, i: (b, 0, 0, 0)
    return pl.pallas_call(
        functools.partial(_gla_kernel, chunk=chunk),
        grid=(B // nb, L // tl),
        in_specs=[pl.BlockSpec((nb, tl, qk), tok), pl.BlockSpec((nb, tl, qk), tok),
                  pl.BlockSpec((nb, tl, vw), tok), pl.BlockSpec((nb, tl, vw), tok),
                  pl.BlockSpec((nb, tl, qk), tok),
                  pl.BlockSpec((nb, H, dk, dv), st),
                  pl.BlockSpec((1, dv), lambda b, i: (0, 0))],
        out_specs=[pl.BlockSpec((nb, tl, vw), tok),
                   pl.BlockSpec((nb, H, dk, dv), st)],
        out_shape=[jax.ShapeDtypeStruct((B, L, vw), F32),
                   jax.ShapeDtypeStruct((B, H, dk, dv), F32)],
        scratch_shapes=[pltpu.VMEM((nb, H, dk, dv), F32)],
        compiler_params=_params("arbitrary", "arbitrary"),
        name="gla_mixer",
    )(q, k, v, sr, la, s0, g_norm.reshape(1, dv))


def _att_block(q_ref, kp_ref, kc_ref, vp_ref, vc_ref, o_ref, l_ref, first):
    blk = q_ref.shape[1]
    row = lax.broadcasted_iota(jnp.int32, (blk, 2 * blk), 0)
    col = lax.broadcasted_iota(jnp.int32, (blk, 2 * blk), 1)
    valid = (col >= row) & (col <= row + blk) & jnp.logical_or(col >= blk, jnp.logical_not(first))
    q = q_ref[0]
    kk = jnp.concatenate([kp_ref[0], kc_ref[0]], axis=0)
    vv = jnp.concatenate([vp_ref[0], vc_ref[0]], axis=0)
    nt = (((1,), (1,)), ((), ()))
    heads_per_tile = LANES // ATT_HEAD_DIM
    lane = lax.broadcasted_iota(jnp.int32, (1, LANES), 1)
    for t in range(q.shape[1] // LANES):
        ls = slice(t * LANES, (t + 1) * LANES)
        qp, kp, vp = q[:, ls], kk[:, ls], vv[:, ls]
        o_tile = jnp.zeros((blk, LANES), F32)
        l_tile = jnp.zeros((blk, LANES), F32)
        for hh in range(heads_per_tile):
            mine = (lane >= hh * ATT_HEAD_DIM) & (lane < (hh + 1) * ATT_HEAD_DIM)
            qm = qp * jnp.where(mine, 1.0, 0.0).astype(qp.dtype)
            s = lax.dot_general(qm, kp, nt, preferred_element_type=F32)
            s = jnp.where(valid, s, NEG_INF)
            m = jnp.max(s, axis=-1, keepdims=True)
            p = jnp.exp(s - m)
            den = jnp.sum(p, axis=-1, keepdims=True)
            o = jnp.dot(p.astype(BF16), vp, preferred_element_type=F32) / den
            o_tile = jnp.where(mine, o, o_tile)
            l_tile = jnp.where(mine, m + jnp.log(den), l_tile)
        o_ref[0, :, ls] = o_tile.astype(o_ref.dtype)
        l_ref[0, :, ls] = l_tile


def _att_sample_heads(q_ref, k_ref, v_ref, caches, y_ref, nbuf, heads):
    G, H, n_new, Dh = q_ref.shape[1:]
    nt = (((1,), (1,)), ((), ()))
    biases = []
    for g, (W, d) in enumerate(ATT_GROUPS):
        nb = nbuf[g]
        i = lax.broadcasted_iota(jnp.int32, (n_new, nb), 0)
        c = lax.broadcasted_iota(jnp.int32, (n_new, nb), 1)
        ok = ((c & (d - 1)) == (i & (d - 1))) & (c >= i - (W - nb))
        i2 = lax.broadcasted_iota(jnp.int32, (n_new, n_new), 0)
        j2 = lax.broadcasted_iota(jnp.int32, (n_new, n_new), 1)
        ok2 = (j2 <= i2) & (((i2 - j2) & (d - 1)) == 0)
        biases.append((jnp.where(ok, 0.0, NEG_INF).astype(F32), jnp.where(ok2, 0.0, NEG_INF).astype(F32)))
    for h in heads:
        scores = []
        for g in range(G):
            qh = q_ref[0, g, h].astype(BF16)
            kt = caches[g][0][0, h].astype(BF16)
            scores.append(jnp.dot(qh, kt, preferred_element_type=F32) + biases[g][0])
            kn = k_ref[0, g, h].astype(BF16)
            scores.append(lax.dot_general(qh, kn, nt, preferred_element_type=F32) + biases[g][1])
        m = functools.reduce(jnp.maximum, [jnp.max(s, axis=-1, keepdims=True) for s in scores])
        acc = jnp.zeros((n_new, Dh), F32)
        den = jnp.zeros((n_new, 1), F32)
        for g in range(G):
            pc = jnp.exp(scores[2 * g] - m).astype(BF16)
            pn = jnp.exp(scores[2 * g + 1] - m).astype(BF16)
            den = den + jnp.sum(pc.astype(F32), axis=-1, keepdims=True) + jnp.sum(pn.astype(F32), axis=-1, keepdims=True)
            vt = caches[g][1][0, h].astype(BF16)
            acc = acc + lax.dot_general(pc, vt, nt, preferred_element_type=F32)
            acc = acc + jnp.dot(pn, v_ref[0, g, h].astype(BF16), preferred_element_type=F32)
        y_ref[0, h] = acc / den


def _att_kernel(*refs, nbuf, blocks_per_class, head_phases):
    G = len(ATT_GROUPS)
    prompt_in = [refs[5 * g:5 * g + 5] for g in range(G)]
    q_ref, k_ref, v_ref = refs[5 * G:5 * G + 3]
    cache_refs = refs[5 * G + 3:7 * G + 3]
    outs = refs[7 * G + 3:]
    s = pl.program_id(0)
    for g in range(G):
        _att_block(*prompt_in[g], outs[2 * g], outs[2 * g + 1], (s % blocks_per_class[g]) == 0)

    caches = [(cache_refs[2 * g], cache_refs[2 * g + 1]) for g in range(G)]
    for phase, heads in enumerate(head_phases):
        @pl.when(s % len(head_phases) == phase)
        def _(heads=heads):
            _att_sample_heads(q_ref, k_ref, v_ref, caches, outs[2 * G], nbuf, heads)


def _attention(qkv, qs, ks, vs, caches_k, caches_v):
    G = len(ATT_GROUPS)
    blk, width = ATT_BLOCK, ATT_WIDTH
    B = qkv[0][0].shape[0]
    L = qkv[0][0].shape[1] * ATT_GROUPS[0][1]
    steps_g = B * L // blk
    Bs, _, H, n_new, Dh = qs.shape
    assert steps_g % Bs == 0
    per = steps_g // Bs
    head_phases = tuple(tuple(range(p * H // per, (p + 1) * H // per)) for p in range(per))

    in_specs, args, out_specs, out_shapes, blocks_per_class = [], [], [], [], []
    for g, (W, d) in enumerate(ATT_GROUPS):
        Ld = L // d
        assert min(W, L) // d == blk and Ld % blk == 0
        n_i = Ld // blk
        blocks_per_class.append(n_i)

        def decode(s, d=d, n_i=n_i):
            br = s // n_i
            return br // d, br % d, s % n_i

        def cur(s, decode=decode):
            b, r, i = decode(s)
            return b, i, r

        def prev(s, decode=decode):
            b, r, i = decode(s)
            return b, jnp.maximum(i - 1, 0), r

        q, k, v = qkv[g]
        in_specs += [pl.BlockSpec((1, blk, width), cur),
                     pl.BlockSpec((1, blk, width), prev), pl.BlockSpec((1, blk, width), cur),
                     pl.BlockSpec((1, blk, width), prev), pl.BlockSpec((1, blk, width), cur)]
        args += [q, k, k, v, v]
        out_specs += [pl.BlockSpec((1, blk, width), cur), pl.BlockSpec((1, blk, width), cur)]
        out_shapes += [jax.ShapeDtypeStruct((B, Ld, d * width), BF16),
                       jax.ShapeDtypeStruct((B, Ld, d * width), F32)]

    nbuf = []
    seq5 = lambda s: (s // per, 0, 0, 0, 0)
    seq4 = lambda s: (s // per, 0, 0, 0)
    in_specs += [pl.BlockSpec((1, G, H, n_new, Dh), seq5)] * 3
    args += [qs, ks, vs]
    for g, (W, d) in enumerate(ATT_GROUPS):
        nb = caches_k[g].shape[-1]
        assert d & (d - 1) == 0 and (W - nb) % d == 0
        nbuf.append(nb)
        in_specs += [pl.BlockSpec((1, H, Dh, nb), seq4)] * 2
        args += [caches_k[g], caches_v[g]]
    out_specs.append(pl.BlockSpec((1, H, n_new, Dh), seq4))
    out_shapes.append(jax.ShapeDtypeStruct((Bs, H, n_new, Dh), F32))

    outs = pl.pallas_call(
        functools.partial(_att_kernel, nbuf=tuple(nbuf),
                          blocks_per_class=tuple(blocks_per_class), head_phases=head_phases),
        grid=(steps_g,),
        in_specs=in_specs,
        out_specs=out_specs,
        out_shape=out_shapes,
        compiler_params=_params("arbitrary"),
        name="attention",
    )(*args)
    return [tuple(outs[2 * g:2 * g + 2]) for g in range(G)], outs[2 * G]


def _rope_tables(pos):
    half = ATT_HEAD_DIM // 2
    inv = ROPE_THETA ** (-jnp.arange(half, dtype=F32) / half)
    ang = pos.astype(F32)[:, None] * inv[None, :]
    cos, sin = jnp.cos(ang), jnp.sin(ang)
    reps = LANES // ATT_HEAD_DIM
    cos_t = jnp.tile(jnp.concatenate([cos, cos], axis=-1), (1, reps))
    sin_t = jnp.tile(jnp.concatenate([-sin, sin], axis=-1), (1, reps))
    return cos_t, sin_t


def kernel(x_prompt, x_sample, state_conv, state_gla, cache_k_g0, cache_v_g0, cache_k_g1, cache_v_g1, cache_k_g2, cache_v_g2, c_prompt, c_sample, w_ada, b_ada, g_pre, g_post, w_conv_in, w_dw, b_dw, g_conv_ln, b_conv_ln, w_conv_out, w_gla_in, w_gla_a1, w_gla_a2, b_gla_a, g_gla_norm, w_gla_out, w_att_in, w_att_out):
    depth = w_ada.shape[0]
    D = x_prompt.shape[-1]
    G = len(ATT_GROUPS)
    xs = [x_prompt, x_sample]
    nseq = [x_prompt.shape[0], x_sample.shape[0]]
    caches_k = (cache_k_g0, cache_k_g1, cache_k_g2)
    caches_v = (cache_v_g0, cache_v_g1, cache_v_g2)

    n_c = nseq[0] + nseq[1]
    pad = (-n_c) % SUBLANES
    c_all = jnp.concatenate([c_prompt, c_sample, jnp.zeros((pad, D), F32)], axis=0)
    mod_all = _modulation(c_all, w_ada, b_ada)
    mods = [[mod_all[l, :nseq[0]].reshape(nseq[0], 1, 3 * D),
             mod_all[l, nseq[0]:n_c].reshape(nseq[1], 1, 3 * D)] for l in range(depth)]

    pos = [jnp.arange(x_prompt.shape[1]), PAST_LEN + jnp.arange(x_sample.shape[1])]
    rope = [_rope_tables(p) for p in pos]

    conv_new, gla_new = ([], []), ([], [])
    k_new = ([[] for _ in range(G)], [[] for _ in range(G)])
    v_new = ([[] for _ in range(G)], [[] for _ in range(G)])
    no_extra = lambda nb, tl: []
    heads = (ATT_HEADS, ATT_HEAD_DIM)

    for l in range(depth):
        kind, j = l % N_MIXERS, l // N_MIXERS
        for grp in range(2):
            x, mod = xs[grp], mods[l][grp]
            B, L, _ = x.shape
            if kind == 0:
                C = w_dw.shape[-1]
                u, sz = _pre_call(_pre_conv_kernel, "pre_conv", x, mod, g_pre[l],
                                  [w_conv_in[j].astype(BF16)], no_extra, [], [C, C])
                if grp == 0:
                    buf = jnp.zeros((B, CONV_SIZE - 1, C), F32)
                else:
                    buf = jnp.transpose(state_conv, (0, 2, 1, 3))
                y, st = _conv_mixer(u, sz, buf, w_dw[j], b_dw[j], g_conv_ln[j], b_conv_ln[j],
                                    time_major_state=grp == 1, layer=j)
                conv_new[grp].append(st)
                xs[grp] = _post_call(_post_kernel, "post_conv", [y], w_conv_out[j].astype(BF16), g_post[l], mod, x)
            elif kind == 1:
                qk = w_gla_a2.shape[-1]
                vw = w_gla_out.shape[1]
                wa1 = jnp.zeros((D, LANES), F32).at[:, :GLA_RANK].set(w_gla_a1[j]).astype(BF16)
                wa2 = jnp.zeros((LANES, qk), F32).at[:GLA_RANK].set(w_gla_a2[j]).astype(BF16)
                q, k, v, sr, la = _pre_call(
                    _pre_gla_kernel, "pre_gla", x, mod, g_pre[l],
                    [w_gla_in[j].astype(BF16), wa1, wa2, b_gla_a[j].reshape(1, qk)], no_extra, [],
                    [qk, qk, vw, vw, qk])
                dk, dv = qk // GLA_HEADS, vw // GLA_HEADS
                s0 = jnp.zeros((B, GLA_HEADS, dk, dv), F32) if grp == 0 else state_gla[j]
                y, st = _gla_mixer(q, k, v, sr, la, s0, g_gla_norm[j])
                gla_new[grp].append(st)
                xs[grp] = _post_call(_post_kernel, "post_gla", [y], w_gla_out[j].astype(BF16), g_post[l], mod, x)
            elif grp == 0:
                xp, xsmp = xs
                mod_p, mod_s = mods[l]
                Bp, Lp, _ = xp.shape
                Bs, Ls, _ = xsmp.shape
                w_out = w_att_out[j].astype(BF16)
                qkv, sz_p, tails = _pre_att_prompt(xp, mod_p, g_pre[l], w_att_in[j], pos[0])
                gw = G * ATT_WIDTH
                cos_t, sin_t = rope[1]
                rope_specs = lambda nb, tl: [pl.BlockSpec((tl, LANES), lambda b, i: (i, 0))] * 2
                q, k, v, sz_s = _pre_call(_pre_att_kernel, "pre_att", xsmp, mod_s, g_pre[l],
                                          [w_att_in[j].astype(BF16)], rope_specs, [cos_t, sin_t],
                                          [gw, gw, gw, ATT_WIDTH])
                ck = [jnp.transpose(c[j], (0, 2, 3, 1)) for c in caches_k]
                cv = [jnp.transpose(c[j], (0, 2, 3, 1)) for c in caches_v]
                q5, k5, v5 = (t.reshape(Bs, Ls, G, *heads) for t in (q, k, v))
                qt, kt, vt = (jnp.transpose(t, (0, 2, 3, 1, 4)) for t in (q5, k5, v5))
                att_outs, y = _attention(qkv, qt, kt, vt, ck, cv)
                y = jnp.transpose(y, (0, 2, 1, 3)).reshape(Bs, Ls, ATT_WIDTH)
                xs[0] = _post_att_prompt(att_outs, sz_p, w_out, g_post[l], mod_p, xp)
                xs[1] = _post_call(_post_gate_kernel, "post_att_sample", [y, sz_s], w_out, g_post[l], mod_s, xsmp)
                for g, (W, d) in enumerate(ATT_GROUPS):
                    keep = min(W, Lp)
                    for store, t in ((k_new, tails[g][0]), (v_new, tails[g][1])):
                        t = t[:, t.shape[1] - keep // d:, :]
                        store[0][g].append(t.reshape(Bp, keep, *heads))
                    k_new[1][g].append(k5[:, :, g])
                    v_new[1][g].append(v5[:, :, g])

    outs = [xs[0], xs[1], jnp.stack(conv_new[0]), jnp.stack(conv_new[1]),
            jnp.stack(gla_new[0]), jnp.stack(gla_new[1])]
    for grp in range(2):
        for g in range(G):
            outs += [jnp.stack(k_new[grp][g]), jnp.stack(v_new[grp][g])]
    return tuple(outs)
```

```python
import functools

import jax
import jax.numpy as jnp
from jax import lax
from jax.experimental import pallas as pl
from jax.experimental.pallas import tpu as pltpu

F32 = jnp.float32
BF16 = jnp.bfloat16
HIGHEST = lax.Precision.HIGHEST

PAST_LEN = 2048
N_MIXERS = 3
CONV_SIZE = 31
GLA_HEADS = 4
GLA_RANK = 16
GLA_TAU = 16.0
ATT_GROUPS = ((128, 1), (512, 4), (2048, 16))
ATT_HEADS = 8
ATT_HEAD_DIM = 64
ATT_WIDTH = ATT_HEADS * ATT_HEAD_DIM
ROPE_THETA = 10000.0
EPS = 1e-6
NEG_INF = -1e30

LANES = 128
SUBLANES = 8
VMEM_LIMIT_BYTES = 56 * 1024 * 1024

PROJ_ROWS = 512
CONV_ROWS = 256
CONV_CHUNK = 32
CONV_BLOCK_ROWS = 32
CONV_BLOCK_LANES = 512
CONV_HALO = 32
GLA_ROWS = 256
GLA_CHUNK = 64
GLA_SUB = 8
GLA_SEQS = 4
ATT_BLOCK = 128
ATT_PROJ_ROWS = 256


def _params(*sem):
    return pltpu.CompilerParams(dimension_semantics=sem, vmem_limit_bytes=VMEM_LIMIT_BYTES)


def _log2(n):
    assert n > 0 and n & (n - 1) == 0, n
    return n.bit_length() - 1


def _token_tiling(B, L, rows=PROJ_ROWS):
    if L >= rows:
        return 1, rows
    nb = max(1, rows // L)
    while B % nb:
        nb //= 2
    return nb, L


def _mod_kernel(c_ref, w_ref, b_ref, o_ref):
    o_ref[0] = jnp.dot(c_ref[...], w_ref[0], precision=HIGHEST, preferred_element_type=F32) + b_ref[0]


def _modulation(c_all, w_ada, b_ada):
    depth, D, N = w_ada.shape
    R = c_all.shape[0]
    tn = 1024
    return pl.pallas_call(
        _mod_kernel,
        grid=(depth, N // tn),
        in_specs=[pl.BlockSpec((R, D), lambda l, j: (0, 0)),
                  pl.BlockSpec((1, D, tn), lambda l, j: (l, 0, j)),
                  pl.BlockSpec((1, 1, tn), lambda l, j: (l, 0, j))],
        out_specs=pl.BlockSpec((1, R, tn), lambda l, j: (l, 0, j)),
        out_shape=jax.ShapeDtypeStruct((depth, R, N), F32),
        compiler_params=_params("arbitrary", "arbitrary"),
        name="adaln_mod",
    )(c_all, w_ada, b_ada.reshape(depth, 1, N))


def _modulated_norm(x_ref, mod_ref, g_ref):
    x = x_ref[...]
    nb, tl, D = x.shape
    ms = jnp.mean(x * x, axis=-1, keepdims=True)
    y = x * lax.rsqrt(ms + EPS) * g_ref[...]
    shift = mod_ref[:, :, 0:D]
    scale = mod_ref[:, :, D:2 * D]
    h = y * (1.0 + scale) + shift
    return h.reshape(nb * tl, D).astype(BF16)


def _pre_conv_kernel(x_ref, mod_ref, g_ref, w_ref, u_ref, sz_ref):
    h = _modulated_norm(x_ref, mod_ref, g_ref)
    res = jnp.dot(h, w_ref[...], preferred_element_type=F32)
    C = u_ref.shape[-1]
    u = res[:, :C] * jax.nn.sigmoid(res[:, C:2 * C])
    z = res[:, 2 * C:]
    u_ref[...] = u.reshape(u_ref.shape)
    sz_ref[...] = (z * jax.nn.sigmoid(z)).reshape(sz_ref.shape)


def _pre_gla_kernel(x_ref, mod_ref, g_ref, w_ref, wa1_ref, wa2_ref, ba_ref,
                    q_ref, k_ref, v_ref, sr_ref, la_ref):
    h = _modulated_norm(x_ref, mod_ref, g_ref)
    res = jnp.dot(h, w_ref[...], preferred_element_type=F32)
    qk = q_ref.shape[-1]
    vw = v_ref.shape[-1]
    dk = qk // GLA_HEADS
    q_ref[...] = (res[:, :qk] * (dk ** -0.5)).reshape(q_ref.shape)
    k_ref[...] = res[:, qk:2 * qk].reshape(k_ref.shape)
    v_ref[...] = res[:, 2 * qk:2 * qk + vw].reshape(v_ref.shape)
    r = res[:, 2 * qk + vw:]
    sr_ref[...] = (r * jax.nn.sigmoid(r)).reshape(sr_ref.shape)
    low = jnp.dot(h, wa1_ref[...], preferred_element_type=F32)
    zg = jnp.dot(low.astype(BF16), wa2_ref[...], preferred_element_type=F32) + ba_ref[...]
    log_sig = jnp.minimum(zg, 0.0) - jnp.log(1.0 + jnp.exp(-jnp.abs(zg)))
    la_ref[...] = (log_sig * (1.0 / GLA_TAU)).reshape(la_ref.shape)


def _swap_halves(x):
    half = ATT_HEAD_DIM // 2
    lane = lax.broadcasted_iota(jnp.int32, x.shape, 1)
    lower = (lane % ATT_HEAD_DIM) < half
    return jnp.where(lower, pltpu.roll(x, LANES - half, 1), pltpu.roll(x, half, 1))


def _rope_cols(res, col0, width, cos, sin, mult):
    pieces = []
    for c in range(width // LANES):
        xc = res[:, col0 + c * LANES: col0 + (c + 1) * LANES]
        val = xc * cos + _swap_halves(xc) * sin
        pieces.append(val * mult if mult != 1.0 else val)
    return pieces


def _pre_att_kernel(x_ref, mod_ref, g_ref, w_ref, cos_ref, sin_ref, q_ref, k_ref, v_ref, sz_ref):
    h = _modulated_norm(x_ref, mod_ref, g_ref)
    res = jnp.dot(h, w_ref[...], preferred_element_type=F32)
    nb, tl, gw = q_ref.shape
    cos = jnp.broadcast_to(cos_ref[...], (nb, tl, LANES)).reshape(nb * tl, LANES)
    sin = jnp.broadcast_to(sin_ref[...], (nb, tl, LANES)).reshape(nb * tl, LANES)
    for c, val in enumerate(_rope_cols(res, 0, gw, cos, sin, ATT_HEAD_DIM ** -0.5)):
        q_ref[:, :, c * LANES:(c + 1) * LANES] = val.reshape(nb, tl, LANES)
    for c, val in enumerate(_rope_cols(res, gw, gw, cos, sin, 1.0)):
        k_ref[:, :, c * LANES:(c + 1) * LANES] = val.reshape(nb, tl, LANES)
    v_ref[...] = res[:, 2 * gw:3 * gw].reshape(v_ref.shape)
    z = res[:, 3 * gw:]
    sz_ref[...] = (z * jax.nn.sigmoid(z)).reshape(sz_ref.shape)


def _pre_call(kernel, name, x, mod, g_pre, weights, extra_specs, extras, out_widths):
    B, L, D = x.shape
    nb, tl = _token_tiling(B, L, PROJ_ROWS if sum(out_widths) <= 7 * D // 2 else PROJ_ROWS // 2)
    tok = lambda b, i: (b, i, 0)
    in_specs = [pl.BlockSpec((nb, tl, D), tok),
                pl.BlockSpec((nb, 1, mod.shape[-1]), lambda b, i: (b, 0, 0)),
                pl.BlockSpec((1, D), lambda b, i: (0, 0))]
    in_specs += [pl.BlockSpec(w.shape, lambda b, i: (0, 0)) for w in weights]
    in_specs += extra_specs(nb, tl)
    return pl.pallas_call(
        kernel,
        grid=(B // nb, L // tl),
        in_specs=in_specs,
        out_specs=[pl.BlockSpec((nb, tl, n), tok) for n in out_widths],
        out_shape=[jax.ShapeDtypeStruct((B, L, n), F32) for n in out_widths],
        compiler_params=_params("arbitrary", "arbitrary"),
        name=name,
    )(x, mod, g_pre.reshape(1, D), *weights, *extras)


def _class_order_source(idx, per, d):
    return (idx & (per - 1)) * d + (idx >> _log2(per))


def _pre_att_prompt_kernel(x_ref, mod_ref, g_ref, w0_ref, w1_ref, w2_ref, wz_ref,
                           c0_ref, s0_ref, c1_ref, s1_ref, c2_ref, s2_ref,
                           q0_ref, k0_ref, v0_ref, q1_ref, k1_ref, v1_ref, q2_ref, k2_ref, v2_ref, sz_ref,
                           kt0_ref, vt0_ref, kt1_ref, vt1_ref, kt2_ref, vt2_ref, *, first_tail):
    h = _modulated_norm(x_ref, mod_ref, g_ref)
    tm = h.shape[0]
    width = sz_ref.shape[-1]
    z = jnp.dot(h, wz_ref[...], preferred_element_type=F32)
    sz_ref[0] = z * jax.nn.sigmoid(z)
    step = pl.program_id(1)
    groups = ((w0_ref, c0_ref, s0_ref, q0_ref, k0_ref, v0_ref, kt0_ref, vt0_ref),
              (w1_ref, c1_ref, s1_ref, q1_ref, k1_ref, v1_ref, kt1_ref, vt1_ref),
              (w2_ref, c2_ref, s2_ref, q2_ref, k2_ref, v2_ref, kt2_ref, vt2_ref))
    for g, (w_ref, cos_ref, sin_ref, q_ref, k_ref, v_ref, kt_ref, vt_ref) in enumerate(groups):
        d = ATT_GROUPS[g][1]
        per = tm // d
        if d > 1:
            dst = lax.broadcasted_iota(jnp.int32, (tm, tm), 0)
            src = lax.broadcasted_iota(jnp.int32, (tm, tm), 1)
            perm = jnp.where(src == _class_order_source(dst, per, d), 1.0, 0.0).astype(BF16)
            hg = jnp.dot(perm, h, preferred_element_type=F32).astype(BF16)
        else:
            hg = h
        res = jnp.dot(hg, w_ref[...], preferred_element_type=F32)
        cos, sin = cos_ref[...], sin_ref[...]
        qf = jnp.concatenate(_rope_cols(res, 0, width, cos, sin, ATT_HEAD_DIM ** -0.5), axis=1)
        kf = jnp.concatenate(_rope_cols(res, width, width, cos, sin, 1.0), axis=1)
        vf = res[:, 2 * width:]
        qb, kb, vb = qf.astype(BF16), kf.astype(BF16), vf.astype(BF16)
        for r in range(d):
            rs = slice(r * per, (r + 1) * per)
            cs = slice(r * width, (r + 1) * width)
            q_ref[0, :, cs] = qb[rs]
            k_ref[0, :, cs] = kb[rs]
            v_ref[0, :, cs] = vb[rs]

        @pl.when(step >= first_tail[g])
        def _(kf=kf, vf=vf, kt_ref=kt_ref, vt_ref=vt_ref, d=d, per=per):
            for r in range(d):
                rs = slice(r * per, (r + 1) * per)
                cs = slice(r * width, (r + 1) * width)
                kt_ref[0, :, cs] = kf[rs]
                vt_ref[0, :, cs] = vf[rs]


def _pre_att_prompt(x, mod, g_pre, w_att_in, pos):
    B, L, D = x.shape
    G = len(ATT_GROUPS)
    width = ATT_WIDTH
    gw = G * width
    tm = ATT_PROJ_ROWS
    n_steps = L // tm
    w = w_att_in.astype(BF16)
    weights, tables, first_tail = [], [], []
    out_specs, out_shapes = [], []
    tail_specs, tail_shapes = [], []
    for g, (W, d) in enumerate(ATT_GROUPS):
        gs = slice(g * width, (g + 1) * width)
        weights.append(jnp.concatenate([w[:, gs], w[:, gw:][:, gs], w[:, 2 * gw:][:, gs]], axis=1))
        per = tm // d
        assert tm % d == 0 and per % 16 == 0 and L % tm == 0
        a = jnp.arange(tm)
        order = (jnp.arange(n_steps)[:, None] * tm + _class_order_source(a, per, d)[None, :]).reshape(-1)
        tables += list(_rope_tables(pos[order]))
        for _ in range(3):
            out_specs.append(pl.BlockSpec((1, per, d * width), lambda b, i: (b, i, 0)))
            out_shapes.append(jax.ShapeDtypeStruct((B, L // d, d * width), BF16))
        tail_rows = max(min(W, L) // d, per)
        n_tail = tail_rows // per
        first_tail.append(n_steps - n_tail)
        for _ in range(2):
            tail_specs.append(pl.BlockSpec(
                (1, per, d * width), lambda b, i, first=n_steps - n_tail: (b, jnp.maximum(i - first, 0), 0)))
            tail_shapes.append(jax.ShapeDtypeStruct((B, tail_rows, d * width), F32))
    weights.append(w[:, 3 * gw:])
    const = lambda b, i: (0, 0)
    in_specs = [pl.BlockSpec((1, tm, D), lambda b, i: (b, i, 0)),
                pl.BlockSpec((1, 1, mod.shape[-1]), lambda b, i: (b, 0, 0)),
                pl.BlockSpec((1, D), const)]
    in_specs += [pl.BlockSpec(wt.shape, const) for wt in weights]
    in_specs += [pl.BlockSpec((tm, LANES), lambda b, i: (i, 0)) for _ in tables]
    sz_spec = pl.BlockSpec((1, tm, width), lambda b, i: (b, i, 0))
    outs = pl.pallas_call(
        functools.partial(_pre_att_prompt_kernel, first_tail=tuple(first_tail)),
        grid=(B, n_steps),
        in_specs=in_specs,
        out_specs=out_specs + [sz_spec] + tail_specs,
        out_shape=out_shapes + [jax.ShapeDtypeStruct((B, L, width), F32)] + tail_shapes,
        compiler_params=_params("arbitrary", "arbitrary"),
        name="pre_att_prompt",
    )(x, mod, g_pre.reshape(1, D), *weights, *tables)
    qkv = [tuple(outs[3 * g:3 * g + 3]) for g in range(G)]
    sz = outs[3 * G]
    tails = [tuple(outs[3 * G + 1 + 2 * g: 3 * G + 3 + 2 * g]) for g in range(G)]
    return qkv, sz, tails


def _gated_residual(y_bf16, w_ref, g_ref, mod_ref, x_ref, o_ref):
    nb, tl, D = x_ref.shape
    o = jnp.dot(y_bf16, w_ref[...], preferred_element_type=F32)
    ms = jnp.mean(o * o, axis=-1, keepdims=True)
    o = (o * lax.rsqrt(ms + EPS) * g_ref[...]).reshape(nb, tl, D)
    gate = mod_ref[:, :, 2 * D:3 * D]
    o_ref[...] = x_ref[...] + gate * o


def _post_kernel(y_ref, w_ref, g_ref, mod_ref, x_ref, o_ref):
    nb, tl, K = y_ref.shape
    _gated_residual(y_ref[...].reshape(nb * tl, K).astype(BF16), w_ref, g_ref, mod_ref, x_ref, o_ref)


def _post_gate_kernel(y_ref, sz_ref, w_ref, g_ref, mod_ref, x_ref, o_ref):
    nb, tl, K = y_ref.shape
    y = y_ref[...] * sz_ref[...]
    _gated_residual(y.reshape(nb * tl, K).astype(BF16), w_ref, g_ref, mod_ref, x_ref, o_ref)


def _post_call(kernel, name, ys, w_out, g_post, mod, x):
    B, L, D = x.shape
    K = w_out.shape[0]
    nb, tl = _token_tiling(B, L)
    tok = lambda b, i: (b, i, 0)
    in_specs = [pl.BlockSpec((nb, tl, K), tok) for _ in ys]
    in_specs += [pl.BlockSpec((K, D), lambda b, i: (0, 0)),
                 pl.BlockSpec((1, D), lambda b, i: (0, 0)),
                 pl.BlockSpec((nb, 1, mod.shape[-1]), lambda b, i: (b, 0, 0)),
                 pl.BlockSpec((nb, tl, D), tok)]
    return pl.pallas_call(
        kernel,
        grid=(B // nb, L // tl),
        in_specs=in_specs,
        out_specs=pl.BlockSpec((nb, tl, D), tok),
        out_shape=jax.ShapeDtypeStruct((B, L, D), F32),
        compiler_params=_params("arbitrary", "arbitrary"),
        name=name,
    )(*ys, w_out, g_post.reshape(1, D), mod, x)


def _class_rows(ref, d):
    width = ref.shape[-1] // d
    return jnp.concatenate([ref[0, :, r * width:(r + 1) * width] for r in range(d)], axis=0)


def _post_att_kernel(o0_ref, l0_ref, o1_ref, l1_ref, o2_ref, l2_ref, sz_ref, w_ref, g_ref, mod_ref, x_ref, o_ref):
    _, tm, K = sz_ref.shape
    outs, lses = [], []
    for g, (o_g_ref, l_g_ref) in enumerate(((o0_ref, l0_ref), (o1_ref, l1_ref), (o2_ref, l2_ref))):
        d = ATT_GROUPS[g][1]
        if d == 1:
            outs.append(o_g_ref[0].astype(F32))
            lses.append(l_g_ref[0])
            continue
        per = tm // d
        dst = lax.broadcasted_iota(jnp.int32, (tm, tm), 0)
        src = lax.broadcasted_iota(jnp.int32, (tm, tm), 1)
        back = jnp.where(dst == _class_order_source(src, per, d), 1.0, 0.0).astype(BF16)
        outs.append(jnp.dot(back, _class_rows(o_g_ref, d), preferred_element_type=F32))
        lse = _class_rows(l_g_ref, d)
        hi = lse.astype(BF16)
        rest = lse - hi.astype(F32)
        mid = rest.astype(BF16)
        lo = (rest - mid.astype(F32)).astype(BF16)
        lses.append(jnp.dot(back, hi, preferred_element_type=F32)
                    + jnp.dot(back, mid, preferred_element_type=F32)
                    + jnp.dot(back, lo, preferred_element_type=F32))
    l0, l1, l2 = lses
    m = jnp.maximum(jnp.maximum(l0, l1), l2)
    e0, e1, e2 = jnp.exp(l0 - m), jnp.exp(l1 - m), jnp.exp(l2 - m)
    att = (e0 * outs[0] + e1 * outs[1] + e2 * outs[2]) / (e0 + e1 + e2)
    y = att * sz_ref[0]
    _gated_residual(y.astype(BF16), w_ref, g_ref, mod_ref, x_ref, o_ref)


def _post_att_prompt(att_outs, sz, w_out, g_post, mod, x):
    B, L, D = x.shape
    K = w_out.shape[0]
    tm = ATT_PROJ_ROWS
    tok = lambda b, i: (b, i, 0)
    in_specs, args = [], []
    for g, (W, d) in enumerate(ATT_GROUPS):
        assert tm % (16 * d) == 0
        for t in att_outs[g]:
            in_specs.append(pl.BlockSpec((1, tm // d, d * K), tok))
            args.append(t)
    in_specs += [pl.BlockSpec((1, tm, K), tok),
                 pl.BlockSpec((K, D), lambda b, i: (0, 0)),
                 pl.BlockSpec((1, D), lambda b, i: (0, 0)),
                 pl.BlockSpec((1, 1, mod.shape[-1]), lambda b, i: (b, 0, 0)),
                 pl.BlockSpec((1, tm, D), tok)]
    return pl.pallas_call(
        _post_att_kernel,
        grid=(B, L // tm),
        in_specs=in_specs,
        out_specs=pl.BlockSpec((1, tm, D), tok),
        out_shape=jax.ShapeDtypeStruct((B, L, D), F32),
        compiler_params=_params("arbitrary", "arbitrary"),
        name="post_att_prompt",
    )(*args, sz, w_out, g_post.reshape(1, D), mod, x)


def _conv_kernel(u_ref, sz_ref, buf_ref, w_ref, b_ref, g_ref, bl_ref, y_ref, st_ref, ext_ref, sh_ref, wb_ref,
                 acc_ref, *, time_major_state):
    nb, tl, C = u_ref.shape
    keep = CONV_SIZE - 1
    lead = CONV_HALO - keep
    rows_sh = sh_ref.shape[1]

    @pl.when(pl.program_id(1) == 0)
    def _():
        if time_major_state:
            for n in range(nb):
                ext_ref[n, lead:CONV_HALO, :] = buf_ref[:, n, :]
        else:
            ext_ref[:, lead:CONV_HALO, :] = buf_ref[...]
        for j in range(CONV_SIZE):
            wb_ref[j] = jnp.broadcast_to(w_ref[j:j + 1, :], (SUBLANES, C))

    ext_ref[:, CONV_HALO:CONV_HALO + tl, :] = u_ref[...]
    chunk = min(tl, CONV_CHUNK)
    blk_rows = min(tl, CONV_BLOCK_ROWS)
    blk_groups = blk_rows // SUBLANES
    bias = b_ref[...]
    gamma = g_ref[...]
    beta = bl_ref[...]
    for n in range(nb):
        for s in range(1, SUBLANES):
            sh_ref[s - 1] = ext_ref[n, s:s + rows_sh, :]

        def conv_rows(r0):
            for c0 in range(0, C, CONV_BLOCK_LANES):
                cl = slice(c0, min(C, c0 + CONV_BLOCK_LANES))
                wc = cl.stop - cl.start
                acc = jnp.broadcast_to(bias[:, cl], (blk_groups, SUBLANES, wc))
                for j in range(CONV_SIZE):
                    a, s = divmod(lead + j, SUBLANES)
                    start = pl.multiple_of(r0 + a * SUBLANES, SUBLANES)
                    if s == 0:
                        win = ext_ref[n, pl.ds(start, blk_rows), cl]
                    else:
                        win = sh_ref[s - 1, pl.ds(start, blk_rows), cl]
                    acc = acc + wb_ref[j, :, cl] * win.reshape(blk_groups, SUBLANES, wc)
                acc_ref[n, pl.ds(pl.multiple_of(r0, SUBLANES), blk_rows), cl] = acc.reshape(blk_rows, wc)

        if tl == blk_rows:
            conv_rows(0)
        else:
            lax.fori_loop(0, tl // blk_rows, lambda ci, c: (conv_rows(ci * blk_rows), c)[1], 0)

        for r0 in range(0, tl, chunk):
            acc = acc_ref[n, r0:r0 + chunk, :]
            mu = jnp.mean(acc, axis=-1, keepdims=True)
            xc = acc - mu
            var = jnp.mean(xc * xc, axis=-1, keepdims=True)
            ln = xc * lax.rsqrt(var + EPS) * gamma + beta
            y_ref[n, r0:r0 + chunk, :] = (ln * jax.nn.sigmoid(ln) * sz_ref[n, r0:r0 + chunk, :]).astype(y_ref.dtype)
    tail = ext_ref[:, lead + tl: CONV_HALO + tl, :]
    st_ref[...] = tail
    ext_ref[:, lead:CONV_HALO, :] = tail


def _conv_mixer(u, sz, buf, w_dw, b_dw, g_ln, b_ln, time_major_state=False, layer=0):
    B, L, C = u.shape
    keep = CONV_SIZE - 1
    if L >= CONV_ROWS:
        nb, tl = 1, CONV_ROWS
    else:
        nb, tl = SUBLANES, L
    rows_sh = tl + CONV_HALO - SUBLANES
    tok = lambda b, i: (b, i, 0)
    vec = lambda b, i: (0, 0)
    if time_major_state:
        assert nb % SUBLANES == 0
        buf_spec = pl.BlockSpec((None, keep, nb, C), lambda b, i: (layer, 0, b, 0))
    else:
        buf_spec = pl.BlockSpec((nb, keep, C), lambda b, i: (b, 0, 0))
    return pl.pallas_call(
        functools.partial(_conv_kernel, time_major_state=time_major_state),
        grid=(B // nb, L // tl),
        in_specs=[pl.BlockSpec((nb, tl, C), tok),
                  pl.BlockSpec((nb, tl, C), tok),
                  buf_spec,
                  pl.BlockSpec((CONV_SIZE, C), vec),
                  pl.BlockSpec((1, C), vec), pl.BlockSpec((1, C), vec), pl.BlockSpec((1, C), vec)],
        out_specs=[pl.BlockSpec((nb, tl, C), tok),
                   pl.BlockSpec((nb, keep, C), lambda b, i: (b, 0, 0))],
        out_shape=[jax.ShapeDtypeStruct((B, L, C), BF16 if L >= CONV_ROWS else F32),
                   jax.ShapeDtypeStruct((B, keep, C), F32)],
        scratch_shapes=[pltpu.VMEM((nb, CONV_HALO + tl, C), F32),
                        pltpu.VMEM((SUBLANES - 1, rows_sh, C), F32),
                        pltpu.VMEM((CONV_SIZE, SUBLANES, C), F32),
                        pltpu.VMEM((nb, tl, C), F32)],
        compiler_params=_params("arbitrary", "arbitrary"),
        name="conv_mixer",
    )(u, sz, buf, w_dw, b_dw.reshape(1, C), g_ln.reshape(1, C), b_ln.reshape(1, C))


def _gla_kernel(q_ref, k_ref, v_ref, sr_ref, la_ref, s0_ref, g_ref, y_ref, sout_ref, S_ref, *, chunk):
    nb, tl, qk = q_ref.shape
    vw = v_ref.shape[-1]
    H = GLA_HEADS
    dk, dv = qk // H, vw // H
    sub = min(chunk, GLA_SUB)
    nblk = chunk // sub

    @pl.when(pl.program_id(1) == 0)
    def _():
        S_ref[...] = s0_ref[...]

    row = lax.broadcasted_iota(jnp.int32, (chunk, chunk), 0)
    col = lax.broadcasted_iota(jnp.int32, (chunk, chunk), 1)
    tril = jnp.where(row >= col, 1.0, 0.0).astype(BF16)
    row1 = lax.broadcasted_iota(jnp.int32, (chunk, 1), 0)
    sub_row = lax.broadcasted_iota(jnp.int32, (nblk, sub, 1), 1)
    ones = jnp.ones((chunk, LANES), BF16)
    assert dv % LANES == 0
    gamma = g_ref[...]
    tn = (((0,), (0,)), ((), ()))
    nt = (((1,), (1,)), ((), ()))
    halves = []
    m = chunk // 2
    while m >= sub:
        halves.append(m)
        m //= 2

    for n, c0 in [(n, c0) for n in range(nb) for c0 in range(0, tl, chunk)]:
        rows = slice(c0, c0 + chunk)
        la = la_ref[n, rows, :]
        la_hi = la.astype(BF16)
        la_lo = (la - la_hi.astype(F32)).astype(BF16)
        bc = (jnp.dot(tril, la_hi, preferred_element_type=F32)
              + jnp.dot(tril, la_lo, preferred_element_type=F32))
        btot = bc[chunk - 1:chunk, :]
        qq = q_ref[n, rows, :]
        kk = k_ref[n, rows, :]
        q0 = qq * jnp.exp(bc)
        kh = kk * jnp.exp(btot - bc)

        level_ops = []
        for m in halves:
            nb2 = chunk // (2 * m)
            bc3 = bc.reshape(nb2, 2 * m, qk)
            ref = jnp.broadcast_to(bc3[:, m - 1:m, :], (nb2, 2 * m, qk)).reshape(chunk, qk)
            e = jnp.exp(-jnp.abs(bc - ref))
            upper = ((row1 >> _log2(m)) & 1) == 1
            level_ops.append((jnp.where(upper, qq * e, 0.0), jnp.where(upper, 0.0, kk * e)))

        bc3 = bc.reshape(nblk, sub, qk)
        q3 = qq.reshape(nblk, sub, qk)
        k3 = kk.reshape(nblk, sub, qk)
        diag_terms = []
        for jp in range(sub):
            ref = jnp.broadcast_to(bc3[:, jp:jp + 1, :], (nblk, sub, qk))
            kj = jnp.broadcast_to(k3[:, jp:jp + 1, :], (nblk, sub, qk))
            wgt = jnp.where(sub_row >= jp, jnp.exp(jnp.minimum(bc3 - ref, 0.0)), 0.0)
            diag_terms.append((q3 * kj * wgt).reshape(chunk, qk))

        for h in range(H):
            ks = slice(h * dk, (h + 1) * dk)
            vs = slice(h * dv, (h + 1) * dv)
            vh = v_ref[n, rows, vs].astype(BF16)
            att = jnp.zeros((chunk, chunk), F32)
            for m, (qm, km) in zip(halves, level_ops):
                part = lax.dot_general(qm[:, ks].astype(BF16), km[:, ks].astype(BF16), nt,
                                       preferred_element_type=F32)
                if 2 * m < chunk:
                    same = (row >> _log2(2 * m)) == (col >> _log2(2 * m))
                    part = jnp.where(same, part, 0.0)
                att = att + part
            blk0 = (row >> _log2(sub)) << _log2(sub)
            for jp in range(sub):
                rs = jnp.sum(diag_terms[jp][:, ks], axis=-1, keepdims=True)
                att = att + jnp.where(col == blk0 + jp, rs, 0.0)
            S = S_ref[n, h]
            o = (jnp.dot(att.astype(BF16), vh, preferred_element_type=F32)
                 + jnp.dot(q0[:, ks].astype(BF16), S.astype(BF16), preferred_element_type=F32))
            tot = (lax.dot_general(la_hi[:, ks], ones, tn, preferred_element_type=F32)
                   + lax.dot_general(la_lo[:, ks], ones, tn, preferred_element_type=F32))
            decay = jnp.exp(jnp.concatenate([tot] * (dv // LANES), axis=1))
            S_ref[n, h] = decay * S + lax.dot_general(kh[:, ks].astype(BF16), vh, tn, preferred_element_type=F32)
            ms = jnp.mean(o * o, axis=-1, keepdims=True)
            y_ref[n, rows, vs] = (o * lax.rsqrt(ms + EPS) * gamma * sr_ref[n, rows, vs]).astype(y_ref.dtype)

    sout_ref[...] = S_ref[...]


def _gla_mixer(q, k, v, sr, la, s0, g_norm):
    B, L, qk = q.shape
    vw = v.shape[-1]
    H = GLA_HEADS
    dk, dv = qk // H, vw // H
    tl = min(L, GLA_ROWS)
    chunk = min(L, GLA_CHUNK)
    nb = 1 if L >= GLA_ROWS else GLA_SEQS
    assert B % nb == 0
    tok = lambda b, i: (b, i, 0)
    st = lambda b, i: (b, 0, 0, 0)
    return pl.pallas_call(
        functools.partial(_gla_kernel, chunk=chunk),
        grid=(B // nb, L // tl),
        in_specs=[pl.BlockSpec((nb, tl, qk), tok), pl.BlockSpec((nb, tl, qk), tok),
                  pl.BlockSpec((nb, tl, vw), tok), pl.BlockSpec((nb, tl, vw), tok),
                  pl.BlockSpec((nb, tl, qk), tok),
                  pl.BlockSpec((nb, H, dk, dv), st),
                  pl.BlockSpec((1, dv), lambda b, i: (0, 0))],
        out_specs=[pl.BlockSpec((nb, tl, vw), tok),
                   pl.BlockSpec((nb, H, dk, dv), st)],
        out_shape=[jax.ShapeDtypeStruct((B, L, vw), BF16 if L >= GLA_ROWS else F32),
                   jax.ShapeDtypeStruct((B, H, dk, dv), F32)],
        scratch_shapes=[pltpu.VMEM((nb, H, dk, dv), F32)],
        compiler_params=_params("arbitrary", "arbitrary"),
        name="gla_mixer",
    )(q, k, v, sr, la, s0, g_norm.reshape(1, dv))


def _att_block(q_ref, kp_ref, kc_ref, vp_ref, vc_ref, o_ref, l_ref, first):
    blk = q_ref.shape[1]
    row = lax.broadcasted_iota(jnp.int32, (blk, 2 * blk), 0)
    col = lax.broadcasted_iota(jnp.int32, (blk, 2 * blk), 1)
    valid = (col >= row) & (col <= row + blk) & jnp.logical_or(col >= blk, jnp.logical_not(first))
    q = q_ref[0]
    kk = jnp.concatenate([kp_ref[0], kc_ref[0]], axis=0)
    vv = jnp.concatenate([vp_ref[0], vc_ref[0]], axis=0)
    nt = (((1,), (1,)), ((), ()))
    heads_per_tile = LANES // ATT_HEAD_DIM
    lane = lax.broadcasted_iota(jnp.int32, (1, LANES), 1)
    for t in range(q.shape[1] // LANES):
        ls = slice(t * LANES, (t + 1) * LANES)
        qp, kp, vp = q[:, ls], kk[:, ls], vv[:, ls]
        o_tile = jnp.zeros((blk, LANES), F32)
        l_tile = jnp.zeros((blk, LANES), F32)
        for hh in range(heads_per_tile):
            mine = (lane >= hh * ATT_HEAD_DIM) & (lane < (hh + 1) * ATT_HEAD_DIM)
            qm = qp * jnp.where(mine, 1.0, 0.0).astype(qp.dtype)
            s = lax.dot_general(qm, kp, nt, preferred_element_type=F32)
            s = jnp.where(valid, s, NEG_INF)
            m = jnp.max(s, axis=-1, keepdims=True)
            p = jnp.exp(s - m)
            den = jnp.sum(p, axis=-1, keepdims=True)
            o = jnp.dot(p.astype(BF16), vp, preferred_element_type=F32) / den
            o_tile = jnp.where(mine, o, o_tile)
            l_tile = jnp.where(mine, m + jnp.log(den), l_tile)
        o_ref[0, :, ls] = o_tile.astype(o_ref.dtype)
        l_ref[0, :, ls] = l_tile


def _att_sample_heads(q_ref, k_ref, v_ref, caches, y_ref, nbuf, heads):
    G, H, n_new, Dh = q_ref.shape[1:]
    nt = (((1,), (1,)), ((), ()))
    biases = []
    for g, (W, d) in enumerate(ATT_GROUPS):
        nb = nbuf[g]
        i = lax.broadcasted_iota(jnp.int32, (n_new, nb), 0)
        c = lax.broadcasted_iota(jnp.int32, (n_new, nb), 1)
        ok = ((c & (d - 1)) == (i & (d - 1))) & (c >= i - (W - nb))
        i2 = lax.broadcasted_iota(jnp.int32, (n_new, n_new), 0)
        j2 = lax.broadcasted_iota(jnp.int32, (n_new, n_new), 1)
        ok2 = (j2 <= i2) & (((i2 - j2) & (d - 1)) == 0)
        biases.append((jnp.where(ok, 0.0, NEG_INF).astype(F32), jnp.where(ok2, 0.0, NEG_INF).astype(F32)))
    for h in heads:
        scores = []
        for g in range(G):
            qh = q_ref[0, g, h].astype(BF16)
            kt = caches[g][0][0, h].astype(BF16)
            scores.append(jnp.dot(qh, kt, preferred_element_type=F32) + biases[g][0])
            kn = k_ref[0, g, h].astype(BF16)
            scores.append(lax.dot_general(qh, kn, nt, preferred_element_type=F32) + biases[g][1])
        m = functools.reduce(jnp.maximum, [jnp.max(s, axis=-1, keepdims=True) for s in scores])
        acc = jnp.zeros((n_new, Dh), F32)
        den = jnp.zeros((n_new, 1), F32)
        for g in range(G):
            pc = jnp.exp(scores[2 * g] - m).astype(BF16)
            pn = jnp.exp(scores[2 * g + 1] - m).astype(BF16)
            den = den + jnp.sum(pc.astype(F32), axis=-1, keepdims=True) + jnp.sum(pn.astype(F32), axis=-1, keepdims=True)
            vt = caches[g][1][0, h].astype(BF16)
            acc = acc + lax.dot_general(pc, vt, nt, preferred_element_type=F32)
            acc = acc + jnp.dot(pn, v_ref[0, g, h].astype(BF16), preferred_element_type=F32)
        y_ref[0, h] = acc / den


def _att_kernel(*refs, nbuf, blocks_per_class, head_phases):
    G = len(ATT_GROUPS)
    prompt_in = [refs[5 * g:5 * g + 5] for g in range(G)]
    q_ref, k_ref, v_ref = refs[5 * G:5 * G + 3]
    cache_refs = refs[5 * G + 3:7 * G + 3]
    outs = refs[7 * G + 3:]
    s = pl.program_id(0)
    for g in range(G):
        _att_block(*prompt_in[g], outs[2 * g], outs[2 * g + 1], (s % blocks_per_class[g]) == 0)

    caches = [(cache_refs[2 * g], cache_refs[2 * g + 1]) for g in range(G)]
    for phase, heads in enumerate(head_phases):
        @pl.when(s % len(head_phases) == phase)
        def _(heads=heads):
            _att_sample_heads(q_ref, k_ref, v_ref, caches, outs[2 * G], nbuf, heads)


def _attention(qkv, qs, ks, vs, caches_k, caches_v):
    G = len(ATT_GROUPS)
    blk, width = ATT_BLOCK, ATT_WIDTH
    B = qkv[0][0].shape[0]
    L = qkv[0][0].shape[1] * ATT_GROUPS[0][1]
    steps_g = B * L // blk
    Bs, _, H, n_new, Dh = qs.shape
    assert steps_g % Bs == 0
    per = steps_g // Bs
    head_phases = tuple(tuple(range(p * H // per, (p + 1) * H // per)) for p in range(per))

    in_specs, args, out_specs, out_shapes, blocks_per_class = [], [], [], [], []
    for g, (W, d) in enumerate(ATT_GROUPS):
        Ld = L // d
        assert min(W, L) // d == blk and Ld % blk == 0
        n_i = Ld // blk
        blocks_per_class.append(n_i)

        def decode(s, d=d, n_i=n_i):
            br = s // n_i
            return br // d, br % d, s % n_i

        def cur(s, decode=decode):
            b, r, i = decode(s)
            return b, i, r

        def prev(s, decode=decode):
            b, r, i = decode(s)
            return b, jnp.maximum(i - 1, 0), r

        q, k, v = qkv[g]
        in_specs += [pl.BlockSpec((1, blk, width), cur),
                     pl.BlockSpec((1, blk, width), prev), pl.BlockSpec((1, blk, width), cur),
                     pl.BlockSpec((1, blk, width), prev), pl.BlockSpec((1, blk, width), cur)]
        args += [q, k, k, v, v]
        out_specs += [pl.BlockSpec((1, blk, width), cur), pl.BlockSpec((1, blk, width), cur)]
        out_shapes += [jax.ShapeDtypeStruct((B, Ld, d * width), BF16),
                       jax.ShapeDtypeStruct((B, Ld, d * width), F32)]

    nbuf = []
    seq5 = lambda s: (s // per, 0, 0, 0, 0)
    seq4 = lambda s: (s // per, 0, 0, 0)
    in_specs += [pl.BlockSpec((1, G, H, n_new, Dh), seq5)] * 3
    args += [qs, ks, vs]
    for g, (W, d) in enumerate(ATT_GROUPS):
        nb = caches_k[g].shape[-1]
        assert d & (d - 1) == 0 and (W - nb) % d == 0
        nbuf.append(nb)
        in_specs += [pl.BlockSpec((1, H, Dh, nb), seq4)] * 2
        args += [caches_k[g], caches_v[g]]
    out_specs.append(pl.BlockSpec((1, H, n_new, Dh), seq4))
    out_shapes.append(jax.ShapeDtypeStruct((Bs, H, n_new, Dh), F32))

    outs = pl.pallas_call(
        functools.partial(_att_kernel, nbuf=tuple(nbuf),
                          blocks_per_class=tuple(blocks_per_class), head_phases=head_phases),
        grid=(steps_g,),
        in_specs=in_specs,
        out_specs=out_specs,
        out_shape=out_shapes,
        compiler_params=_params("arbitrary"),
        name="attention",
    )(*args)
    return [tuple(outs[2 * g:2 * g + 2]) for g in range(G)], outs[2 * G]


def _rope_tables(pos):
    half = ATT_HEAD_DIM // 2
    inv = ROPE_THETA ** (-jnp.arange(half, dtype=F32) / half)
    ang = pos.astype(F32)[:, None] * inv[None, :]
    cos, sin = jnp.cos(ang), jnp.sin(ang)
    reps = LANES // ATT_HEAD_DIM
    cos_t = jnp.tile(jnp.concatenate([cos, cos], axis=-1), (1, reps))
    sin_t = jnp.tile(jnp.concatenate([-sin, sin], axis=-1), (1, reps))
    return cos_t, sin_t


def kernel(x_prompt, x_sample, state_conv, state_gla, cache_k_g0, cache_v_g0, cache_k_g1, cache_v_g1, cache_k_g2, cache_v_g2, c_prompt, c_sample, w_ada, b_ada, g_pre, g_post, w_conv_in, w_dw, b_dw, g_conv_ln, b_conv_ln, w_conv_out, w_gla_in, w_gla_a1, w_gla_a2, b_gla_a, g_gla_norm, w_gla_out, w_att_in, w_att_out):
    depth = w_ada.shape[0]
    D = x_prompt.shape[-1]
    G = len(ATT_GROUPS)
    xs = [x_prompt, x_sample]
    nseq = [x_prompt.shape[0], x_sample.shape[0]]
    caches_k = (cache_k_g0, cache_k_g1, cache_k_g2)
    caches_v = (cache_v_g0, cache_v_g1, cache_v_g2)

    n_c = nseq[0] + nseq[1]
    pad = (-n_c) % SUBLANES
    c_all = jnp.concatenate([c_prompt, c_sample, jnp.zeros((pad, D), F32)], axis=0)
    mod_all = _modulation(c_all, w_ada, b_ada)
    mods = [[mod_all[l, :nseq[0]].reshape(nseq[0], 1, 3 * D),
             mod_all[l, nseq[0]:n_c].reshape(nseq[1], 1, 3 * D)] for l in range(depth)]

    pos = [jnp.arange(x_prompt.shape[1]), PAST_LEN + jnp.arange(x_sample.shape[1])]
    rope = [_rope_tables(p) for p in pos]

    conv_new, gla_new = ([], []), ([], [])
    k_new = ([[] for _ in range(G)], [[] for _ in range(G)])
    v_new = ([[] for _ in range(G)], [[] for _ in range(G)])
    no_extra = lambda nb, tl: []
    heads = (ATT_HEADS, ATT_HEAD_DIM)

    for l in range(depth):
        kind, j = l % N_MIXERS, l // N_MIXERS
        for grp in range(2):
            x, mod = xs[grp], mods[l][grp]
            B, L, _ = x.shape
            if kind == 0:
                C = w_dw.shape[-1]
                u, sz = _pre_call(_pre_conv_kernel, "pre_conv", x, mod, g_pre[l],
                                  [w_conv_in[j].astype(BF16)], no_extra, [], [C, C])
                if grp == 0:
                    buf = jnp.zeros((B, CONV_SIZE - 1, C), F32)
                else:
                    buf = jnp.transpose(state_conv, (0, 2, 1, 3))
                y, st = _conv_mixer(u, sz, buf, w_dw[j], b_dw[j], g_conv_ln[j], b_conv_ln[j],
                                    time_major_state=grp == 1, layer=j)
                conv_new[grp].append(st)
                xs[grp] = _post_call(_post_kernel, "post_conv", [y], w_conv_out[j].astype(BF16), g_post[l], mod, x)
            elif kind == 1:
                qk = w_gla_a2.shape[-1]
                vw = w_gla_out.shape[1]
                wa1 = jnp.zeros((D, LANES), F32).at[:, :GLA_RANK].set(w_gla_a1[j]).astype(BF16)
                wa2 = jnp.zeros((LANES, qk), F32).at[:GLA_RANK].set(w_gla_a2[j]).astype(BF16)
                q, k, v, sr, la = _pre_call(
                    _pre_gla_kernel, "pre_gla", x, mod, g_pre[l],
                    [w_gla_in[j].astype(BF16), wa1, wa2, b_gla_a[j].reshape(1, qk)], no_extra, [],
                    [qk, qk, vw, vw, qk])
                dk, dv = qk // GLA_HEADS, vw // GLA_HEADS
                s0 = jnp.zeros((B, GLA_HEADS, dk, dv), F32) if grp == 0 else state_gla[j]
                y, st = _gla_mixer(q, k, v, sr, la, s0, g_gla_norm[j])
                gla_new[grp].append(st)
                xs[grp] = _post_call(_post_kernel, "post_gla", [y], w_gla_out[j].astype(BF16), g_post[l], mod, x)
            elif grp == 0:
                xp, xsmp = xs
                mod_p, mod_s = mods[l]
                Bp, Lp, _ = xp.shape
                Bs, Ls, _ = xsmp.shape
                w_out = w_att_out[j].astype(BF16)
                qkv, sz_p, tails = _pre_att_prompt(xp, mod_p, g_pre[l], w_att_in[j], pos[0])
                gw = G * ATT_WIDTH
                cos_t, sin_t = rope[1]
                rope_specs = lambda nb, tl: [pl.BlockSpec((tl, LANES), lambda b, i: (i, 0))] * 2
                q, k, v, sz_s = _pre_call(_pre_att_kernel, "pre_att", xsmp, mod_s, g_pre[l],
                                          [w_att_in[j].astype(BF16)], rope_specs, [cos_t, sin_t],
                                          [gw, gw, gw, ATT_WIDTH])
                ck = [jnp.transpose(c[j], (0, 2, 3, 1)) for c in caches_k]
                cv = [jnp.transpose(c[j], (0, 2, 3, 1)) for c in caches_v]
                q5, k5, v5 = (t.reshape(Bs, Ls, G, *heads) for t in (q, k, v))
                qt, kt, vt = (jnp.transpose(t, (0, 2, 3, 1, 4)) for t in (q5, k5, v5))
                att_outs, y = _attention(qkv, qt, kt, vt, ck, cv)
                y = jnp.transpose(y, (0, 2, 1, 3)).reshape(Bs, Ls, ATT_WIDTH)
                xs[0] = _post_att_prompt(att_outs, sz_p, w_out, g_post[l], mod_p, xp)
                xs[1] = _post_call(_post_gate_kernel, "post_att_sample", [y, sz_s], w_out, g_post[l], mod_s, xsmp)
                for g, (W, d) in enumerate(ATT_GROUPS):
                    keep = min(W, Lp)
                    for store, t in ((k_new, tails[g][0]), (v_new, tails[g][1])):
                        t = t[:, t.shape[1] - keep // d:, :]
                        store[0][g].append(t.reshape(Bp, keep, *heads))
                    k_new[1][g].append(k5[:, :, g])
                    v_new[1][g].append(v5[:, :, g])

    outs = [xs[0], xs[1], jnp.stack(conv_new[0]), jnp.stack(conv_new[1]),
            jnp.stack(gla_new[0]), jnp.stack(gla_new[1])]
    for grp in range(2):
        for g in range(G):
            outs += [jnp.stack(k_new[grp][g]), jnp.stack(v_new[grp][g])]
    return tuple(outs)
```

```python
import functools

import jax
import jax.numpy as jnp
from jax import lax
from jax.experimental import pallas as pl
from jax.experimental.pallas import tpu as pltpu

F32 = jnp.float32
BF16 = jnp.bfloat16
HIGHEST = lax.Precision.HIGHEST

PAST_LEN = 2048
N_MIXERS = 3
CONV_SIZE = 31
GLA_HEADS = 4
GLA_RANK = 16
GLA_TAU = 16.0
ATT_GROUPS = ((128, 1), (512, 4), (2048, 16))
ATT_HEADS = 8
ATT_HEAD_DIM = 64
ATT_WIDTH = ATT_HEADS * ATT_HEAD_DIM
ROPE_THETA = 10000.0
EPS = 1e-6
NEG_INF = -1e30

LANES = 128
SUBLANES = 8
VMEM_LIMIT_BYTES = 56 * 1024 * 1024

PROJ_ROWS = 512
CONV_ROWS = 512
CONV_CHUNK = 32
CONV_BLOCK_ROWS = 32
CONV_BLOCK_LANES = 512
CONV_HALO = 32
GLA_ROWS = 256
GLA_CHUNK = 64
GLA_SUB = 8
GLA_SEQS = 4
ATT_BLOCK = 128
ATT_PROJ_ROWS = 256


def _params(*sem):
    return pltpu.CompilerParams(dimension_semantics=sem, vmem_limit_bytes=VMEM_LIMIT_BYTES)


def _log2(n):
    assert n > 0 and n & (n - 1) == 0, n
    return n.bit_length() - 1


def _token_tiling(B, L, rows=PROJ_ROWS):
    if L >= rows:
        return 1, rows
    nb = max(1, rows // L)
    while B % nb:
        nb //= 2
    return nb, L


def _mod_kernel(c_ref, w_ref, b_ref, o_ref):
    o_ref[0] = jnp.dot(c_ref[...], w_ref[0], precision=HIGHEST, preferred_element_type=F32) + b_ref[0]


def _modulation(c_all, w_ada, b_ada):
    depth, D, N = w_ada.shape
    R = c_all.shape[0]
    tn = 1024
    return pl.pallas_call(
        _mod_kernel,
        grid=(depth, N // tn),
        in_specs=[pl.BlockSpec((R, D), lambda l, j: (0, 0)),
                  pl.BlockSpec((1, D, tn), lambda l, j: (l, 0, j)),
                  pl.BlockSpec((1, 1, tn), lambda l, j: (l, 0, j))],
        out_specs=pl.BlockSpec((1, R, tn), lambda l, j: (l, 0, j)),
        out_shape=jax.ShapeDtypeStruct((depth, R, N), F32),
        compiler_params=_params("arbitrary", "arbitrary"),
        name="adaln_mod",
    )(c_all, w_ada, b_ada.reshape(depth, 1, N))


def _modulated_norm(x_ref, mod_ref, g_ref):
    x = x_ref[...]
    nb, tl, D = x.shape
    ms = jnp.mean(x * x, axis=-1, keepdims=True)
    y = x * lax.rsqrt(ms + EPS) * g_ref[...]
    shift = mod_ref[:, :, 0:D]
    scale = mod_ref[:, :, D:2 * D]
    h = y * (1.0 + scale) + shift
    return h.reshape(nb * tl, D).astype(BF16)


def _pre_conv_kernel(x_ref, mod_ref, g_ref, w_ref, u_ref, sz_ref):
    h = _modulated_norm(x_ref, mod_ref, g_ref)
    res = jnp.dot(h, w_ref[...], preferred_element_type=F32)
    C = u_ref.shape[-1]
    u = res[:, :C] * jax.nn.sigmoid(res[:, C:2 * C])
    z = res[:, 2 * C:]
    u_ref[...] = u.reshape(u_ref.shape)
    sz_ref[...] = (z * jax.nn.sigmoid(z)).reshape(sz_ref.shape)


def _pre_gla_kernel(x_ref, mod_ref, g_ref, w_ref, wa1_ref, wa2_ref, ba_ref,
                    q_ref, k_ref, v_ref, sr_ref, la_ref):
    h = _modulated_norm(x_ref, mod_ref, g_ref)
    res = jnp.dot(h, w_ref[...], preferred_element_type=F32)
    qk = q_ref.shape[-1]
    vw = v_ref.shape[-1]
    dk = qk // GLA_HEADS
    q_ref[...] = (res[:, :qk] * (dk ** -0.5)).reshape(q_ref.shape)
    k_ref[...] = res[:, qk:2 * qk].reshape(k_ref.shape)
    v_ref[...] = res[:, 2 * qk:2 * qk + vw].astype(v_ref.dtype).reshape(v_ref.shape)
    r = res[:, 2 * qk + vw:]
    sr_ref[...] = (r * jax.nn.sigmoid(r)).reshape(sr_ref.shape)
    low = jnp.dot(h, wa1_ref[...], preferred_element_type=F32)
    zg = jnp.dot(low.astype(BF16), wa2_ref[...], preferred_element_type=F32) + ba_ref[...]
    log_sig = jnp.minimum(zg, 0.0) - jnp.log(1.0 + jnp.exp(-jnp.abs(zg)))
    la_ref[...] = (log_sig * (1.0 / GLA_TAU)).reshape(la_ref.shape)


def _swap_halves(x):
    half = ATT_HEAD_DIM // 2
    lane = lax.broadcasted_iota(jnp.int32, x.shape, 1)
    lower = (lane % ATT_HEAD_DIM) < half
    return jnp.where(lower, pltpu.roll(x, LANES - half, 1), pltpu.roll(x, half, 1))


def _rope_cols(res, col0, width, cos, sin, mult):
    pieces = []
    for c in range(width // LANES):
        xc = res[:, col0 + c * LANES: col0 + (c + 1) * LANES]
        val = xc * cos + _swap_halves(xc) * sin
        pieces.append(val * mult if mult != 1.0 else val)
    return pieces


def _pre_att_kernel(x_ref, mod_ref, g_ref, w_ref, cos_ref, sin_ref, q_ref, k_ref, v_ref, sz_ref):
    h = _modulated_norm(x_ref, mod_ref, g_ref)
    res = jnp.dot(h, w_ref[...], preferred_element_type=F32)
    nb, tl, gw = q_ref.shape
    cos = jnp.broadcast_to(cos_ref[...], (nb, tl, LANES)).reshape(nb * tl, LANES)
    sin = jnp.broadcast_to(sin_ref[...], (nb, tl, LANES)).reshape(nb * tl, LANES)
    for c, val in enumerate(_rope_cols(res, 0, gw, cos, sin, ATT_HEAD_DIM ** -0.5)):
        q_ref[:, :, c * LANES:(c + 1) * LANES] = val.reshape(nb, tl, LANES)
    for c, val in enumerate(_rope_cols(res, gw, gw, cos, sin, 1.0)):
        k_ref[:, :, c * LANES:(c + 1) * LANES] = val.reshape(nb, tl, LANES)
    v_ref[...] = res[:, 2 * gw:3 * gw].reshape(v_ref.shape)
    z = res[:, 3 * gw:]
    sz_ref[...] = (z * jax.nn.sigmoid(z)).reshape(sz_ref.shape)


def _pre_call(kernel, name, x, mod, g_pre, weights, extra_specs, extras, out_widths, out_dtypes=None):
    B, L, D = x.shape
    out_dtypes = out_dtypes or [F32] * len(out_widths)
    nb, tl = _token_tiling(B, L, PROJ_ROWS if sum(out_widths) <= 7 * D // 2 else PROJ_ROWS // 2)
    tok = lambda b, i: (b, i, 0)
    in_specs = [pl.BlockSpec((nb, tl, D), tok),
                pl.BlockSpec((nb, 1, mod.shape[-1]), lambda b, i: (b, 0, 0)),
                pl.BlockSpec((1, D), lambda b, i: (0, 0))]
    in_specs += [pl.BlockSpec(w.shape, lambda b, i: (0, 0)) for w in weights]
    in_specs += extra_specs(nb, tl)
    return pl.pallas_call(
        kernel,
        grid=(B // nb, L // tl),
        in_specs=in_specs,
        out_specs=[pl.BlockSpec((nb, tl, n), tok) for n in out_widths],
        out_shape=[jax.ShapeDtypeStruct((B, L, n), dt) for n, dt in zip(out_widths, out_dtypes)],
        compiler_params=_params("arbitrary", "arbitrary"),
        name=name,
    )(x, mod, g_pre.reshape(1, D), *weights, *extras)


def _class_order_source(idx, per, d):
    return (idx & (per - 1)) * d + (idx >> _log2(per))


def _pre_att_prompt_kernel(x_ref, mod_ref, g_ref, w0_ref, w1_ref, w2_ref, wz_ref,
                           c0_ref, s0_ref, c1_ref, s1_ref, c2_ref, s2_ref,
                           q0_ref, k0_ref, v0_ref, q1_ref, k1_ref, v1_ref, q2_ref, k2_ref, v2_ref, sz_ref,
                           kt0_ref, vt0_ref, kt1_ref, vt1_ref, kt2_ref, vt2_ref, *, first_tail):
    h = _modulated_norm(x_ref, mod_ref, g_ref)
    tm = h.shape[0]
    width = sz_ref.shape[-1]
    z = jnp.dot(h, wz_ref[...], preferred_element_type=F32)
    sz_ref[0] = z * jax.nn.sigmoid(z)
    step = pl.program_id(1)
    groups = ((w0_ref, c0_ref, s0_ref, q0_ref, k0_ref, v0_ref, kt0_ref, vt0_ref),
              (w1_ref, c1_ref, s1_ref, q1_ref, k1_ref, v1_ref, kt1_ref, vt1_ref),
              (w2_ref, c2_ref, s2_ref, q2_ref, k2_ref, v2_ref, kt2_ref, vt2_ref))
    for g, (w_ref, cos_ref, sin_ref, q_ref, k_ref, v_ref, kt_ref, vt_ref) in enumerate(groups):
        d = ATT_GROUPS[g][1]
        per = tm // d
        if d > 1:
            dst = lax.broadcasted_iota(jnp.int32, (tm, tm), 0)
            src = lax.broadcasted_iota(jnp.int32, (tm, tm), 1)
            perm = jnp.where(src == _class_order_source(dst, per, d), 1.0, 0.0).astype(BF16)
            hg = jnp.dot(perm, h, preferred_element_type=F32).astype(BF16)
        else:
            hg = h
        res = jnp.dot(hg, w_ref[...], preferred_element_type=F32)
        cos, sin = cos_ref[...], sin_ref[...]
        qf = jnp.concatenate(_rope_cols(res, 0, width, cos, sin, ATT_HEAD_DIM ** -0.5), axis=1)
        kf = jnp.concatenate(_rope_cols(res, width, width, cos, sin, 1.0), axis=1)
        vf = res[:, 2 * width:]
        qb, kb, vb = qf.astype(BF16), kf.astype(BF16), vf.astype(BF16)
        for r in range(d):
            rs = slice(r * per, (r + 1) * per)
            cs = slice(r * width, (r + 1) * width)
            q_ref[0, :, cs] = qb[rs]
            k_ref[0, :, cs] = kb[rs]
            v_ref[0, :, cs] = vb[rs]

        @pl.when(step >= first_tail[g])
        def _(kf=kf, vf=vf, kt_ref=kt_ref, vt_ref=vt_ref, d=d, per=per):
            for r in range(d):
                rs = slice(r * per, (r + 1) * per)
                cs = slice(r * width, (r + 1) * width)
                kt_ref[0, :, cs] = kf[rs]
                vt_ref[0, :, cs] = vf[rs]


def _pre_att_prompt(x, mod, g_pre, w_att_in, pos):
    B, L, D = x.shape
    G = len(ATT_GROUPS)
    width = ATT_WIDTH
    gw = G * width
    tm = ATT_PROJ_ROWS
    n_steps = L // tm
    w = w_att_in.astype(BF16)
    weights, tables, first_tail = [], [], []
    out_specs, out_shapes = [], []
    tail_specs, tail_shapes = [], []
    for g, (W, d) in enumerate(ATT_GROUPS):
        gs = slice(g * width, (g + 1) * width)
        weights.append(jnp.concatenate([w[:, gs], w[:, gw:][:, gs], w[:, 2 * gw:][:, gs]], axis=1))
        per = tm // d
        assert tm % d == 0 and per % 16 == 0 and L % tm == 0
        a = jnp.arange(tm)
        order = (jnp.arange(n_steps)[:, None] * tm + _class_order_source(a, per, d)[None, :]).reshape(-1)
        tables += list(_rope_tables(pos[order]))
        for _ in range(3):
            out_specs.append(pl.BlockSpec((1, per, d * width), lambda b, i: (b, i, 0)))
            out_shapes.append(jax.ShapeDtypeStruct((B, L // d, d * width), BF16))
        tail_rows = max(min(W, L) // d, per)
        n_tail = tail_rows // per
        first_tail.append(n_steps - n_tail)
        for _ in range(2):
            tail_specs.append(pl.BlockSpec(
                (1, per, d * width), lambda b, i, first=n_steps - n_tail: (b, jnp.maximum(i - first, 0), 0)))
            tail_shapes.append(jax.ShapeDtypeStruct((B, tail_rows, d * width), F32))
    weights.append(w[:, 3 * gw:])
    const = lambda b, i: (0, 0)
    in_specs = [pl.BlockSpec((1, tm, D), lambda b, i: (b, i, 0)),
                pl.BlockSpec((1, 1, mod.shape[-1]), lambda b, i: (b, 0, 0)),
                pl.BlockSpec((1, D), const)]
    in_specs += [pl.BlockSpec(wt.shape, const) for wt in weights]
    in_specs += [pl.BlockSpec((tm, LANES), lambda b, i: (i, 0)) for _ in tables]
    sz_spec = pl.BlockSpec((1, tm, width), lambda b, i: (b, i, 0))
    outs = pl.pallas_call(
        functools.partial(_pre_att_prompt_kernel, first_tail=tuple(first_tail)),
        grid=(B, n_steps),
        in_specs=in_specs,
        out_specs=out_specs + [sz_spec] + tail_specs,
        out_shape=out_shapes + [jax.ShapeDtypeStruct((B, L, width), F32)] + tail_shapes,
        compiler_params=_params("arbitrary", "arbitrary"),
        name="pre_att_prompt",
    )(x, mod, g_pre.reshape(1, D), *weights, *tables)
    qkv = [tuple(outs[3 * g:3 * g + 3]) for g in range(G)]
    sz = outs[3 * G]
    tails = [tuple(outs[3 * G + 1 + 2 * g: 3 * G + 3 + 2 * g]) for g in range(G)]
    return qkv, sz, tails


def _gated_residual(y_bf16, w_ref, g_ref, mod_ref, x_ref, o_ref):
    nb, tl, D = x_ref.shape
    o = jnp.dot(y_bf16, w_ref[...], preferred_element_type=F32)
    ms = jnp.mean(o * o, axis=-1, keepdims=True)
    o = (o * lax.rsqrt(ms + EPS) * g_ref[...]).reshape(nb, tl, D)
    gate = mod_ref[:, :, 2 * D:3 * D]
    o_ref[...] = x_ref[...] + gate * o


def _post_kernel(y_ref, w_ref, g_ref, mod_ref, x_ref, o_ref):
    nb, tl, K = y_ref.shape
    _gated_residual(y_ref[...].reshape(nb * tl, K).astype(BF16), w_ref, g_ref, mod_ref, x_ref, o_ref)


def _post_gate_kernel(y_ref, sz_ref, w_ref, g_ref, mod_ref, x_ref, o_ref):
    nb, tl, K = y_ref.shape
    y = y_ref[...] * sz_ref[...]
    _gated_residual(y.reshape(nb * tl, K).astype(BF16), w_ref, g_ref, mod_ref, x_ref, o_ref)


def _post_call(kernel, name, ys, w_out, g_post, mod, x):
    B, L, D = x.shape
    K = w_out.shape[0]
    nb, tl = _token_tiling(B, L)
    tok = lambda b, i: (b, i, 0)
    in_specs = [pl.BlockSpec((nb, tl, K), tok) for _ in ys]
    in_specs += [pl.BlockSpec((K, D), lambda b, i: (0, 0)),
                 pl.BlockSpec((1, D), lambda b, i: (0, 0)),
                 pl.BlockSpec((nb, 1, mod.shape[-1]), lambda b, i: (b, 0, 0)),
                 pl.BlockSpec((nb, tl, D), tok)]
    return pl.pallas_call(
        kernel,
        grid=(B // nb, L // tl),
        in_specs=in_specs,
        out_specs=pl.BlockSpec((nb, tl, D), tok),
        out_shape=jax.ShapeDtypeStruct((B, L, D), F32),
        compiler_params=_params("arbitrary", "arbitrary"),
        name=name,
    )(*ys, w_out, g_post.reshape(1, D), mod, x)


def _class_rows(ref, d):
    width = ref.shape[-1] // d
    return jnp.concatenate([ref[0, :, r * width:(r + 1) * width] for r in range(d)], axis=0)


def _post_att_kernel(o0_ref, l0_ref, o1_ref, l1_ref, o2_ref, l2_ref, sz_ref, w_ref, g_ref, mod_ref, x_ref, o_ref):
    _, tm, K = sz_ref.shape
    outs, lses = [], []
    for g, (o_g_ref, l_g_ref) in enumerate(((o0_ref, l0_ref), (o1_ref, l1_ref), (o2_ref, l2_ref))):
        d = ATT_GROUPS[g][1]
        if d == 1:
            outs.append(o_g_ref[0].astype(F32))
            lses.append(l_g_ref[0])
            continue
        per = tm // d
        dst = lax.broadcasted_iota(jnp.int32, (tm, tm), 0)
        src = lax.broadcasted_iota(jnp.int32, (tm, tm), 1)
        back = jnp.where(dst == _class_order_source(src, per, d), 1.0, 0.0).astype(BF16)
        outs.append(jnp.dot(back, _class_rows(o_g_ref, d), preferred_element_type=F32))
        lse = _class_rows(l_g_ref, d)
        hi = lse.astype(BF16)
        rest = lse - hi.astype(F32)
        mid = rest.astype(BF16)
        lo = (rest - mid.astype(F32)).astype(BF16)
        lses.append(jnp.dot(back, hi, preferred_element_type=F32)
                    + jnp.dot(back, mid, preferred_element_type=F32)
                    + jnp.dot(back, lo, preferred_element_type=F32))
    l0, l1, l2 = lses
    m = jnp.maximum(jnp.maximum(l0, l1), l2)
    e0, e1, e2 = jnp.exp(l0 - m), jnp.exp(l1 - m), jnp.exp(l2 - m)
    att = (e0 * outs[0] + e1 * outs[1] + e2 * outs[2]) / (e0 + e1 + e2)
    y = att * sz_ref[0]
    _gated_residual(y.astype(BF16), w_ref, g_ref, mod_ref, x_ref, o_ref)


def _post_att_prompt(att_outs, sz, w_out, g_post, mod, x):
    B, L, D = x.shape
    K = w_out.shape[0]
    tm = ATT_PROJ_ROWS
    tok = lambda b, i: (b, i, 0)
    in_specs, args = [], []
    for g, (W, d) in enumerate(ATT_GROUPS):
        assert tm % (16 * d) == 0
        for t in att_outs[g]:
            in_specs.append(pl.BlockSpec((1, tm // d, d * K), tok))
            args.append(t)
    in_specs += [pl.BlockSpec((1, tm, K), tok),
                 pl.BlockSpec((K, D), lambda b, i: (0, 0)),
                 pl.BlockSpec((1, D), lambda b, i: (0, 0)),
                 pl.BlockSpec((1, 1, mod.shape[-1]), lambda b, i: (b, 0, 0)),
                 pl.BlockSpec((1, tm, D), tok)]
    return pl.pallas_call(
        _post_att_kernel,
        grid=(B, L // tm),
        in_specs=in_specs,
        out_specs=pl.BlockSpec((1, tm, D), tok),
        out_shape=jax.ShapeDtypeStruct((B, L, D), F32),
        compiler_params=_params("arbitrary", "arbitrary"),
        name="post_att_prompt",
    )(*args, sz, w_out, g_post.reshape(1, D), mod, x)


def _conv_kernel(u_ref, sz_ref, buf_ref, w_ref, b_ref, g_ref, bl_ref, y_ref, st_ref, ext_ref, sh_ref, wb_ref,
                 acc_ref, *, time_major_state):
    nb, tl, C = u_ref.shape
    keep = CONV_SIZE - 1
    lead = CONV_HALO - keep
    rows_sh = sh_ref.shape[1]

    @pl.when(pl.program_id(1) == 0)
    def _():
        if time_major_state:
            for n in range(nb):
                ext_ref[n, lead:CONV_HALO, :] = buf_ref[:, n, :]
        else:
            ext_ref[:, lead:CONV_HALO, :] = buf_ref[...]
        for j in range(CONV_SIZE):
            wb_ref[j] = jnp.broadcast_to(w_ref[j:j + 1, :], (SUBLANES, C))

    ext_ref[:, CONV_HALO:CONV_HALO + tl, :] = u_ref[...]
    chunk = min(tl, CONV_CHUNK)
    blk_rows = min(tl, CONV_BLOCK_ROWS)
    blk_groups = blk_rows // SUBLANES
    bias = b_ref[...]
    gamma = g_ref[...]
    beta = bl_ref[...]
    for n in range(nb):
        for s in range(1, SUBLANES):
            sh_ref[s - 1] = ext_ref[n, s:s + rows_sh, :]

        def conv_rows(r0):
            for c0 in range(0, C, CONV_BLOCK_LANES):
                cl = slice(c0, min(C, c0 + CONV_BLOCK_LANES))
                wc = cl.stop - cl.start
                acc = jnp.broadcast_to(bias[:, cl], (blk_groups, SUBLANES, wc))
                for j in range(CONV_SIZE):
                    a, s = divmod(lead + j, SUBLANES)
                    start = pl.multiple_of(r0 + a * SUBLANES, SUBLANES)
                    if s == 0:
                        win = ext_ref[n, pl.ds(start, blk_rows), cl]
                    else:
                        win = sh_ref[s - 1, pl.ds(start, blk_rows), cl]
                    acc = acc + wb_ref[j, :, cl] * win.reshape(blk_groups, SUBLANES, wc)
                acc_ref[n, pl.ds(pl.multiple_of(r0, SUBLANES), blk_rows), cl] = acc.reshape(blk_rows, wc)

        if tl == blk_rows:
            conv_rows(0)
        else:
            lax.fori_loop(0, tl // blk_rows, lambda ci, c: (conv_rows(ci * blk_rows), c)[1], 0)

        for r0 in range(0, tl, chunk):
            acc = acc_ref[n, r0:r0 + chunk, :]
            mu = jnp.mean(acc, axis=-1, keepdims=True)
            xc = acc - mu
            var = jnp.mean(xc * xc, axis=-1, keepdims=True)
            ln = xc * lax.rsqrt(var + EPS) * gamma + beta
            y_ref[n, r0:r0 + chunk, :] = (ln * jax.nn.sigmoid(ln) * sz_ref[n, r0:r0 + chunk, :]).astype(y_ref.dtype)
    tail = ext_ref[:, lead + tl: CONV_HALO + tl, :]
    st_ref[...] = tail
    ext_ref[:, lead:CONV_HALO, :] = tail


def _conv_mixer(u, sz, buf, w_dw, b_dw, g_ln, b_ln, time_major_state=False, layer=0):
    B, L, C = u.shape
    keep = CONV_SIZE - 1
    if L >= CONV_ROWS:
        nb, tl = 1, CONV_ROWS
    else:
        nb, tl = SUBLANES, L
    rows_sh = tl + CONV_HALO - SUBLANES
    tok = lambda b, i: (b, i, 0)
    vec = lambda b, i: (0, 0)
    if time_major_state:
        assert nb % SUBLANES == 0
        buf_spec = pl.BlockSpec((None, keep, nb, C), lambda b, i: (layer, 0, b, 0))
    else:
        buf_spec = pl.BlockSpec((nb, keep, C), lambda b, i: (b, 0, 0))
    return pl.pallas_call(
        functools.partial(_conv_kernel, time_major_state=time_major_state),
        grid=(B // nb, L // tl),
        in_specs=[pl.BlockSpec((nb, tl, C), tok),
                  pl.BlockSpec((nb, tl, C), tok),
                  buf_spec,
                  pl.BlockSpec((CONV_SIZE, C), vec),
                  pl.BlockSpec((1, C), vec), pl.BlockSpec((1, C), vec), pl.BlockSpec((1, C), vec)],
        out_specs=[pl.BlockSpec((nb, tl, C), tok),
                   pl.BlockSpec((nb, keep, C), lambda b, i: (b, 0, 0))],
        out_shape=[jax.ShapeDtypeStruct((B, L, C), BF16 if L >= CONV_ROWS else F32),
                   jax.ShapeDtypeStruct((B, keep, C), F32)],
        scratch_shapes=[pltpu.VMEM((nb, CONV_HALO + tl, C), F32),
                        pltpu.VMEM((SUBLANES - 1, rows_sh, C), F32),
                        pltpu.VMEM((CONV_SIZE, SUBLANES, C), F32),
                        pltpu.VMEM((nb, tl, C), F32)],
        compiler_params=_params("arbitrary", "arbitrary"),
        name="conv_mixer",
    )(u, sz, buf, w_dw, b_dw.reshape(1, C), g_ln.reshape(1, C), b_ln.reshape(1, C))


def _gla_kernel(q_ref, k_ref, v_ref, sr_ref, la_ref, s0_ref, g_ref, y_ref, sout_ref, S_ref, *, chunk):
    nb, tl, qk = q_ref.shape
    vw = v_ref.shape[-1]
    H = GLA_HEADS
    dk, dv = qk // H, vw // H
    sub = min(chunk, GLA_SUB)
    nblk = chunk // sub

    @pl.when(pl.program_id(1) == 0)
    def _():
        S_ref[...] = s0_ref[...]

    row = lax.broadcasted_iota(jnp.int32, (chunk, chunk), 0)
    col = lax.broadcasted_iota(jnp.int32, (chunk, chunk), 1)
    tril = jnp.where(row >= col, 1.0, 0.0).astype(BF16)
    row1 = lax.broadcasted_iota(jnp.int32, (chunk, 1), 0)
    sub_row = lax.broadcasted_iota(jnp.int32, (nblk, sub, 1), 1)
    ones = jnp.ones((chunk, LANES), BF16)
    assert dv % LANES == 0
    gamma = g_ref[...]
    tn = (((0,), (0,)), ((), ()))
    nt = (((1,), (1,)), ((), ()))
    halves = []
    m = chunk // 2
    while m >= sub:
        halves.append(m)
        m //= 2

    for n, c0 in [(n, c0) for n in range(nb) for c0 in range(0, tl, chunk)]:
        rows = slice(c0, c0 + chunk)
        la = la_ref[n, rows, :]
        la_hi = la.astype(BF16)
        la_lo = (la - la_hi.astype(F32)).astype(BF16)
        bc = (jnp.dot(tril, la_hi, preferred_element_type=F32)
              + jnp.dot(tril, la_lo, preferred_element_type=F32))
        btot = bc[chunk - 1:chunk, :]
        qq = q_ref[n, rows, :]
        kk = k_ref[n, rows, :]
        q0 = qq * jnp.exp(bc)
        kh = kk * jnp.exp(btot - bc)

        level_ops = []
        for m in halves:
            nb2 = chunk // (2 * m)
            bc3 = bc.reshape(nb2, 2 * m, qk)
            ref = jnp.broadcast_to(bc3[:, m - 1:m, :], (nb2, 2 * m, qk)).reshape(chunk, qk)
            e = jnp.exp(-jnp.abs(bc - ref))
            upper = ((row1 >> _log2(m)) & 1) == 1
            level_ops.append((jnp.where(upper, qq * e, 0.0), jnp.where(upper, 0.0, kk * e)))

        bc3 = bc.reshape(nblk, sub, qk)
        q3 = qq.reshape(nblk, sub, qk)
        k3 = kk.reshape(nblk, sub, qk)
        diag_terms = []
        for jp in range(sub):
            ref = jnp.broadcast_to(bc3[:, jp:jp + 1, :], (nblk, sub, qk))
            kj = jnp.broadcast_to(k3[:, jp:jp + 1, :], (nblk, sub, qk))
            wgt = jnp.where(sub_row >= jp, jnp.exp(jnp.minimum(bc3 - ref, 0.0)), 0.0)
            diag_terms.append((q3 * kj * wgt).reshape(chunk, qk))

        for h in range(H):
            ks = slice(h * dk, (h + 1) * dk)
            vs = slice(h * dv, (h + 1) * dv)
            vh = v_ref[n, rows, vs].astype(BF16)
            att = jnp.zeros((chunk, chunk), F32)
            for m, (qm, km) in zip(halves, level_ops):
                part = lax.dot_general(qm[:, ks].astype(BF16), km[:, ks].astype(BF16), nt,
                                       preferred_element_type=F32)
                if 2 * m < chunk:
                    same = (row >> _log2(2 * m)) == (col >> _log2(2 * m))
                    part = jnp.where(same, part, 0.0)
                att = att + part
            blk0 = (row >> _log2(sub)) << _log2(sub)
            for jp in range(sub):
                rs = jnp.sum(diag_terms[jp][:, ks], axis=-1, keepdims=True)
                att = att + jnp.where(col == blk0 + jp, rs, 0.0)
            S = S_ref[n, h]
            o = (jnp.dot(att.astype(BF16), vh, preferred_element_type=F32)
                 + jnp.dot(q0[:, ks].astype(BF16), S.astype(BF16), preferred_element_type=F32))
            tot = (lax.dot_general(la_hi[:, ks], ones, tn, preferred_element_type=F32)
                   + lax.dot_general(la_lo[:, ks], ones, tn, preferred_element_type=F32))
            decay = jnp.exp(jnp.concatenate([tot] * (dv // LANES), axis=1))
            S_ref[n, h] = decay * S + lax.dot_general(kh[:, ks].astype(BF16), vh, tn, preferred_element_type=F32)
            ms = jnp.mean(o * o, axis=-1, keepdims=True)
            y_ref[n, rows, vs] = (o * lax.rsqrt(ms + EPS) * gamma * sr_ref[n, rows, vs]).astype(y_ref.dtype)

    sout_ref[...] = S_ref[...]


def _gla_mixer(q, k, v, sr, la, s0, g_norm):
    B, L, qk = q.shape
    vw = v.shape[-1]
    H = GLA_HEADS
    dk, dv = qk // H, vw // H
    tl = min(L, GLA_ROWS)
    chunk = min(L, GLA_CHUNK)
    nb = 1 if L >= GLA_ROWS else GLA_SEQS
    assert B % nb == 0
    tok = lambda b, i: (b, i, 0)
    st = lambda b, i: (b, 0, 0, 0)
    return pl.pallas_call(
        functools.partial(_gla_kernel, chunk=chunk),
        grid=(B // nb, L // tl),
        in_specs=[pl.BlockSpec((nb, tl, qk), tok), pl.BlockSpec((nb, tl, qk), tok),
                  pl.BlockSpec((nb, tl, vw), tok), pl.BlockSpec((nb, tl, vw), tok),
                  pl.BlockSpec((nb, tl, qk), tok),
                  pl.BlockSpec((nb, H, dk, dv), st),
                  pl.BlockSpec((1, dv), lambda b, i: (0, 0))],
        out_specs=[pl.BlockSpec((nb, tl, vw), tok),
                   pl.BlockSpec((nb, H, dk, dv), st)],
        out_shape=[jax.ShapeDtypeStruct((B, L, vw), BF16 if L >= GLA_ROWS else F32),
                   jax.ShapeDtypeStruct((B, H, dk, dv), F32)],
        scratch_shapes=[pltpu.VMEM((nb, H, dk, dv), F32)],
        compiler_params=_params("arbitrary", "arbitrary"),
        name="gla_mixer",
    )(q, k, v, sr, la, s0, g_norm.reshape(1, dv))


def _att_block(q_ref, kp_ref, kc_ref, vp_ref, vc_ref, o_ref, l_ref, first):
    blk = q_ref.shape[1]
    row = lax.broadcasted_iota(jnp.int32, (blk, 2 * blk), 0)
    col = lax.broadcasted_iota(jnp.int32, (blk, 2 * blk), 1)
    valid = (col >= row) & (col <= row + blk) & jnp.logical_or(col >= blk, jnp.logical_not(first))
    q = q_ref[0]
    kk = jnp.concatenate([kp_ref[0], kc_ref[0]], axis=0)
    vv = jnp.concatenate([vp_ref[0], vc_ref[0]], axis=0)
    nt = (((1,), (1,)), ((), ()))
    heads_per_tile = LANES // ATT_HEAD_DIM
    lane = lax.broadcasted_iota(jnp.int32, (1, LANES), 1)
    for t in range(q.shape[1] // LANES):
        ls = slice(t * LANES, (t + 1) * LANES)
        qp, kp, vp = q[:, ls], kk[:, ls], vv[:, ls]
        o_tile = jnp.zeros((blk, LANES), F32)
        l_tile = jnp.zeros((blk, LANES), F32)
        for hh in range(heads_per_tile):
            mine = (lane >= hh * ATT_HEAD_DIM) & (lane < (hh + 1) * ATT_HEAD_DIM)
            qm = qp * jnp.where(mine, 1.0, 0.0).astype(qp.dtype)
            s = lax.dot_general(qm, kp, nt, preferred_element_type=F32)
            s = jnp.where(valid, s, NEG_INF)
            m = jnp.max(s, axis=-1, keepdims=True)
            p = jnp.exp(s - m)
            den = jnp.sum(p, axis=-1, keepdims=True)
            o = jnp.dot(p.astype(BF16), vp, preferred_element_type=F32) / den
            o_tile = jnp.where(mine, o, o_tile)
            l_tile = jnp.where(mine, m + jnp.log(den), l_tile)
        o_ref[0, :, ls] = o_tile.astype(o_ref.dtype)
        l_ref[0, :, ls] = l_tile


def _att_sample_heads(q_ref, k_ref, v_ref, caches, y_ref, nbuf, heads):
    G, H, n_new, Dh = q_ref.shape[1:]
    nt = (((1,), (1,)), ((), ()))
    biases = []
    for g, (W, d) in enumerate(ATT_GROUPS):
        nb = nbuf[g]
        i = lax.broadcasted_iota(jnp.int32, (n_new, nb), 0)
        c = lax.broadcasted_iota(jnp.int32, (n_new, nb), 1)
        ok = ((c & (d - 1)) == (i & (d - 1))) & (c >= i - (W - nb))
        i2 = lax.broadcasted_iota(jnp.int32, (n_new, n_new), 0)
        j2 = lax.broadcasted_iota(jnp.int32, (n_new, n_new), 1)
        ok2 = (j2 <= i2) & (((i2 - j2) & (d - 1)) == 0)
        biases.append((jnp.where(ok, 0.0, NEG_INF).astype(F32), jnp.where(ok2, 0.0, NEG_INF).astype(F32)))
    for h in heads:
        scores = []
        for g in range(G):
            qh = q_ref[0, g, h].astype(BF16)
            kt = caches[g][0][0, h].astype(BF16)
            scores.append(jnp.dot(qh, kt, preferred_element_type=F32) + biases[g][0])
            kn = k_ref[0, g, h].astype(BF16)
            scores.append(lax.dot_general(qh, kn, nt, preferred_element_type=F32) + biases[g][1])
        m = functools.reduce(jnp.maximum, [jnp.max(s, axis=-1, keepdims=True) for s in scores])
        acc = jnp.zeros((n_new, Dh), F32)
        den = jnp.zeros((n_new, 1), F32)
        for g in range(G):
            pc = jnp.exp(scores[2 * g] - m).astype(BF16)
            pn = jnp.exp(scores[2 * g + 1] - m).astype(BF16)
            den = den + jnp.sum(pc.astype(F32), axis=-1, keepdims=True) + jnp.sum(pn.astype(F32), axis=-1, keepdims=True)
            vt = caches[g][1][0, h].astype(BF16)
            acc = acc + lax.dot_general(pc, vt, nt, preferred_element_type=F32)
            acc = acc + jnp.dot(pn, v_ref[0, g, h].astype(BF16), preferred_element_type=F32)
        y_ref[0, h] = acc / den


def _att_kernel(*refs, nbuf, blocks_per_class, head_phases):
    G = len(ATT_GROUPS)
    prompt_in = [refs[5 * g:5 * g + 5] for g in range(G)]
    q_ref, k_ref, v_ref = refs[5 * G:5 * G + 3]
    cache_refs = refs[5 * G + 3:7 * G + 3]
    outs = refs[7 * G + 3:]
    s = pl.program_id(0)
    for g in range(G):
        _att_block(*prompt_in[g], outs[2 * g], outs[2 * g + 1], (s % blocks_per_class[g]) == 0)

    caches = [(cache_refs[2 * g], cache_refs[2 * g + 1]) for g in range(G)]
    for phase, heads in enumerate(head_phases):
        @pl.when(s % len(head_phases) == phase)
        def _(heads=heads):
            _att_sample_heads(q_ref, k_ref, v_ref, caches, outs[2 * G], nbuf, heads)


def _attention(qkv, qs, ks, vs, caches_k, caches_v):
    G = len(ATT_GROUPS)
    blk, width = ATT_BLOCK, ATT_WIDTH
    B = qkv[0][0].shape[0]
    L = qkv[0][0].shape[1] * ATT_GROUPS[0][1]
    steps_g = B * L // blk
    Bs, _, H, n_new, Dh = qs.shape
    assert steps_g % Bs == 0
    per = steps_g // Bs
    head_phases = tuple(tuple(range(p * H // per, (p + 1) * H // per)) for p in range(per))

    in_specs, args, out_specs, out_shapes, blocks_per_class = [], [], [], [], []
    for g, (W, d) in enumerate(ATT_GROUPS):
        Ld = L // d
        assert min(W, L) // d == blk and Ld % blk == 0
        n_i = Ld // blk
        blocks_per_class.append(n_i)

        def decode(s, d=d, n_i=n_i):
            br = s // n_i
            return br // d, br % d, s % n_i

        def cur(s, decode=decode):
            b, r, i = decode(s)
            return b, i, r

        def prev(s, decode=decode):
            b, r, i = decode(s)
            return b, jnp.maximum(i - 1, 0), r

        q, k, v = qkv[g]
        in_specs += [pl.BlockSpec((1, blk, width), cur),
                     pl.BlockSpec((1, blk, width), prev), pl.BlockSpec((1, blk, width), cur),
                     pl.BlockSpec((1, blk, width), prev), pl.BlockSpec((1, blk, width), cur)]
        args += [q, k, k, v, v]
        out_specs += [pl.BlockSpec((1, blk, width), cur), pl.BlockSpec((1, blk, width), cur)]
        out_shapes += [jax.ShapeDtypeStruct((B, Ld, d * width), BF16),
                       jax.ShapeDtypeStruct((B, Ld, d * width), F32)]

    nbuf = []
    seq5 = lambda s: (s // per, 0, 0, 0, 0)
    seq4 = lambda s: (s // per, 0, 0, 0)
    in_specs += [pl.BlockSpec((1, G, H, n_new, Dh), seq5)] * 3
    args += [qs, ks, vs]
    for g, (W, d) in enumerate(ATT_GROUPS):
        nb = caches_k[g].shape[-1]
        assert d & (d - 1) == 0 and (W - nb) % d == 0
        nbuf.append(nb)
        in_specs += [pl.BlockSpec((1, H, Dh, nb), seq4)] * 2
        args += [caches_k[g], caches_v[g]]
    out_specs.append(pl.BlockSpec((1, H, n_new, Dh), seq4))
    out_shapes.append(jax.ShapeDtypeStruct((Bs, H, n_new, Dh), F32))

    outs = pl.pallas_call(
        functools.partial(_att_kernel, nbuf=tuple(nbuf),
                          blocks_per_class=tuple(blocks_per_class), head_phases=head_phases),
        grid=(steps_g,),
        in_specs=in_specs,
        out_specs=out_specs,
        out_shape=out_shapes,
        compiler_params=_params("arbitrary"),
        name="attention",
    )(*args)
    return [tuple(outs[2 * g:2 * g + 2]) for g in range(G)], outs[2 * G]


def _rope_tables(pos):
    half = ATT_HEAD_DIM // 2
    inv = ROPE_THETA ** (-jnp.arange(half, dtype=F32) / half)
    ang = pos.astype(F32)[:, None] * inv[None, :]
    cos, sin = jnp.cos(ang), jnp.sin(ang)
    reps = LANES // ATT_HEAD_DIM
    cos_t = jnp.tile(jnp.concatenate([cos, cos], axis=-1), (1, reps))
    sin_t = jnp.tile(jnp.concatenate([-sin, sin], axis=-1), (1, reps))
    return cos_t, sin_t


def kernel(x_prompt, x_sample, state_conv, state_gla, cache_k_g0, cache_v_g0, cache_k_g1, cache_v_g1, cache_k_g2, cache_v_g2, c_prompt, c_sample, w_ada, b_ada, g_pre, g_post, w_conv_in, w_dw, b_dw, g_conv_ln, b_conv_ln, w_conv_out, w_gla_in, w_gla_a1, w_gla_a2, b_gla_a, g_gla_norm, w_gla_out, w_att_in, w_att_out):
    depth = w_ada.shape[0]
    D = x_prompt.shape[-1]
    G = len(ATT_GROUPS)
    xs = [x_prompt, x_sample]
    nseq = [x_prompt.shape[0], x_sample.shape[0]]
    caches_k = (cache_k_g0, cache_k_g1, cache_k_g2)
    caches_v = (cache_v_g0, cache_v_g1, cache_v_g2)

    n_c = nseq[0] + nseq[1]
    pad = (-n_c) % SUBLANES
    c_all = jnp.concatenate([c_prompt, c_sample, jnp.zeros((pad, D), F32)], axis=0)
    mod_all = _modulation(c_all, w_ada, b_ada)
    mods = [[mod_all[l, :nseq[0]].reshape(nseq[0], 1, 3 * D),
             mod_all[l, nseq[0]:n_c].reshape(nseq[1], 1, 3 * D)] for l in range(depth)]

    pos = [jnp.arange(x_prompt.shape[1]), PAST_LEN + jnp.arange(x_sample.shape[1])]
    rope = [_rope_tables(p) for p in pos]

    conv_new, gla_new = ([], []), ([], [])
    k_new = ([[] for _ in range(G)], [[] for _ in range(G)])
    v_new = ([[] for _ in range(G)], [[] for _ in range(G)])
    no_extra = lambda nb, tl: []
    heads = (ATT_HEADS, ATT_HEAD_DIM)

    for l in range(depth):
        kind, j = l % N_MIXERS, l // N_MIXERS
        for grp in range(2):
            x, mod = xs[grp], mods[l][grp]
            B, L, _ = x.shape
            if kind == 0:
                C = w_dw.shape[-1]
                u, sz = _pre_call(_pre_conv_kernel, "pre_conv", x, mod, g_pre[l],
                                  [w_conv_in[j].astype(BF16)], no_extra, [], [C, C])
                if grp == 0:
                    buf = jnp.zeros((B, CONV_SIZE - 1, C), F32)
                else:
                    buf = jnp.transpose(state_conv, (0, 2, 1, 3))
                y, st = _conv_mixer(u, sz, buf, w_dw[j], b_dw[j], g_conv_ln[j], b_conv_ln[j],
                                    time_major_state=grp == 1, layer=j)
                conv_new[grp].append(st)
                xs[grp] = _post_call(_post_kernel, "post_conv", [y], w_conv_out[j].astype(BF16), g_post[l], mod, x)
            elif kind == 1:
                qk = w_gla_a2.shape[-1]
                vw = w_gla_out.shape[1]
                wa1 = jnp.zeros((D, LANES), F32).at[:, :GLA_RANK].set(w_gla_a1[j]).astype(BF16)
                wa2 = jnp.zeros((LANES, qk), F32).at[:GLA_RANK].set(w_gla_a2[j]).astype(BF16)
                q, k, v, sr, la = _pre_call(
                    _pre_gla_kernel, "pre_gla", x, mod, g_pre[l],
                    [w_gla_in[j].astype(BF16), wa1, wa2, b_gla_a[j].reshape(1, qk)], no_extra, [],
                    [qk, qk, vw, vw, qk],
                    [F32, F32, BF16 if L >= GLA_ROWS else F32, F32, F32])
                dk, dv = qk // GLA_HEADS, vw // GLA_HEADS
                s0 = jnp.zeros((B, GLA_HEADS, dk, dv), F32) if grp == 0 else state_gla[j]
                y, st = _gla_mixer(q, k, v, sr, la, s0, g_gla_norm[j])
                gla_new[grp].append(st)
                xs[grp] = _post_call(_post_kernel, "post_gla", [y], w_gla_out[j].astype(BF16), g_post[l], mod, x)
            elif grp == 0:
                xp, xsmp = xs
                mod_p, mod_s = mods[l]
                Bp, Lp, _ = xp.shape
                Bs, Ls, _ = xsmp.shape
                w_out = w_att_out[j].astype(BF16)
                qkv, sz_p, tails = _pre_att_prompt(xp, mod_p, g_pre[l], w_att_in[j], pos[0])
                gw = G * ATT_WIDTH
                cos_t, sin_t = rope[1]
                rope_specs = lambda nb, tl: [pl.BlockSpec((tl, LANES), lambda b, i: (i, 0))] * 2
                q, k, v, sz_s = _pre_call(_pre_att_kernel, "pre_att", xsmp, mod_s, g_pre[l],
                                          [w_att_in[j].astype(BF16)], rope_specs, [cos_t, sin_t],
                                          [gw, gw, gw, ATT_WIDTH])
                ck = [jnp.transpose(c[j], (0, 2, 3, 1)) for c in caches_k]
                cv = [jnp.transpose(c[j], (0, 2, 3, 1)) for c in caches_v]
                q5, k5, v5 = (t.reshape(Bs, Ls, G, *heads) for t in (q, k, v))
                qt, kt, vt = (jnp.transpose(t, (0, 2, 3, 1, 4)) for t in (q5, k5, v5))
                att_outs, y = _attention(qkv, qt, kt, vt, ck, cv)
                y = jnp.transpose(y, (0, 2, 1, 3)).reshape(Bs, Ls, ATT_WIDTH)
                xs[0] = _post_att_prompt(att_outs, sz_p, w_out, g_post[l], mod_p, xp)
                xs[1] = _post_call(_post_gate_kernel, "post_att_sample", [y, sz_s], w_out, g_post[l], mod_s, xsmp)
                for g, (W, d) in enumerate(ATT_GROUPS):
                    keep = min(W, Lp)
                    for store, t in ((k_new, tails[g][0]), (v_new, tails[g][1])):
                        t = t[:, t.shape[1] - keep // d:, :]
                        store[0][g].append(t.reshape(Bp, keep, *heads))
                    k_new[1][g].append(k5[:, :, g])
                    v_new[1][g].append(v5[:, :, g])

    outs = [xs[0], xs[1], jnp.stack(conv_new[0]), jnp.stack(conv_new[1]),
            jnp.stack(gla_new[0]), jnp.stack(gla_new[1])]
    for grp in range(2):
        for g in range(G):
            outs += [jnp.stack(k_new[grp][g]), jnp.stack(v_new[grp][g])]
    return tuple(outs)
```
